```python
import jax, jax.numpy as jnp
from jax import lax
import numpy as np

D_MODEL = 2048
BATCH = 16
SEQ = 256
DEPTH = 4
DEC_BATCH = 2
DEC_SEQ = 4096
PAST_LEN = 512

GRID_W = 64
HEAD_DIM = 128
N_HEADS_A = 8
N_KV_A = 2
A_WIDTH = N_HEADS_A * HEAD_DIM
KV_WIDTH = N_KV_A * HEAD_DIM
WINDOW = 128
BLOCK = 128
ROPE_BASE = 10000.0
ROT_AXIS = HEAD_DIM // 2
ATTN_SCALE = HEAD_DIM ** -0.5
NEG_MASK = -1e9
CONF_W = 512
CONF_K = 31
SC_W = 512
SC_K = 3
HG_HEADS = 4
HG_DK = 128
HG_DV = 128
HG_W = HG_HEADS * HG_DK
HG_CHUNK = 32
N_BRANCH = 4
IN_COLS = A_WIDTH + 2 * KV_WIDTH + 2 * CONF_W + 3 * SC_W + 5 * HG_W
D_FF = -(-8 * D_MODEL // (3 * 256)) * 256
EPS = 1e-6

kernel_name = 'hybrid_diffusion_parallel_trunk_step'


def rms_norm(x, g):
    xf = x.astype(jnp.float32)
    y = xf * lax.rsqrt(jnp.mean(xf * xf, axis=-1, keepdims=True) + EPS)
    return (y * g.astype(jnp.float32)).astype(x.dtype)


def layer_norm(x, g, b):
    xf = x.astype(jnp.float32)
    xc = xf - jnp.mean(xf, axis=-1, keepdims=True)
    var = jnp.mean(xc * xc, axis=-1, keepdims=True)
    return (xc * lax.rsqrt(var + EPS) * g.astype(jnp.float32) + b.astype(jnp.float32)).astype(x.dtype)


def modulate(x, g, shift, scale):
    return rms_norm(x, g) * (1 + scale) + shift


def depthwise_conv(x, w):
    pad = w.shape[0] // 2
    return lax.conv_general_dilated(x, w[:, None, :].astype(x.dtype), window_strides=(1,),
                                    padding=[(pad, pad)], dimension_numbers=('NWC', 'WIO', 'NWC'),
                                    feature_group_count=x.shape[-1])


def split_columns(z):
    sizes = [A_WIDTH, KV_WIDTH, KV_WIDTH, CONF_W, CONF_W, SC_W, SC_W, SC_W, HG_W, HG_W, HG_W, HG_W, HG_W]
    parts, start = [], 0
    for s in sizes:
        parts.append(z[..., start:start + s])
        start += s
    return parts


def axial_rope_tables(T):
    rows = T // GRID_W
    row = jnp.repeat(jnp.arange(rows, dtype=jnp.float32), GRID_W)
    col = jnp.tile(jnp.arange(GRID_W, dtype=jnp.float32), rows)
    inv = ROPE_BASE ** (-jnp.arange(0, ROT_AXIS, 2, dtype=jnp.float32) / ROT_AXIS)
    ang = jnp.stack([row[:, None] * inv, col[:, None] * inv], axis=1)
    return jnp.cos(ang), jnp.sin(ang)


def apply_rope(x, cos, sin):
    B, T, H, _ = x.shape
    xr = x.astype(jnp.float32).reshape(B, T, H, 2, 2, ROT_AXIS // 2)
    x1, x2 = xr[..., 0, :], xr[..., 1, :]
    c = cos[None, :, None]
    s = sin[None, :, None]
    out = jnp.stack([x1 * c - x2 * s, x2 * c + x1 * s], axis=-2)
    return out.reshape(B, T, H, HEAD_DIM).astype(x.dtype)


def context_attention(q, k, v, sink):
    B, S, H, D = q.shape
    G = H // N_KV_A
    nb = S // BLOCK
    qb = q.reshape(B, nb, BLOCK, N_KV_A, G, D).swapaxes(0, 1)
    sink_l = sink.astype(jnp.float32).reshape(N_KV_A, G)[None, :, :, None, None]

    def one_block(qj):
        s = jnp.einsum('bqkgd,bskd->bkgqs', qj, k, preferred_element_type=jnp.float32) * ATTN_SCALE
        logits = jnp.concatenate([s, jnp.broadcast_to(sink_l, s.shape[:-1] + (1,))], axis=-1)
        p = jax.nn.softmax(logits, axis=-1)[..., :-1].astype(v.dtype)
        return jnp.einsum('bkgqs,bskd->bqkgd', p, v)

    out = lax.map(one_block, qb)
    return out.swapaxes(0, 1).reshape(B, S, H * D)


def latent_window_attention(q, k, v, k_ctx, v_ctx, sink):
    B, T, H, D = q.shape
    G = H // N_KV_A
    nb = T // BLOCK
    pad = ((0, 0), (BLOCK, BLOCK), (0, 0), (0, 0))
    kp = jnp.pad(k, pad)
    vp = jnp.pad(v, pad)
    qb = q.reshape(B, nb, BLOCK, N_KV_A, G, D).swapaxes(0, 1)
    sink_l = sink.astype(jnp.float32).reshape(N_KV_A, G)[None, :, :, None, None]
    offs_q = jnp.arange(BLOCK)
    offs_k = jnp.arange(3 * BLOCK) - BLOCK
    band = jnp.abs(offs_k[None, :] - offs_q[:, None]) <= WINDOW

    def one_block(args):
        j, qj = args
        kw = lax.dynamic_slice_in_dim(kp, j * BLOCK, 3 * BLOCK, axis=1)
        vw = lax.dynamic_slice_in_dim(vp, j * BLOCK, 3 * BLOCK, axis=1)
        kpos = j * BLOCK + offs_k
        valid = band & ((kpos >= 0) & (kpos < T))[None, :]
        s_w = jnp.einsum('bqkgd,bskd->bkgqs', qj, kw, preferred_element_type=jnp.float32) * ATTN_SCALE
        s_w = jnp.where(valid, s_w, NEG_MASK)
        s_c = jnp.einsum('bqkgd,bskd->bkgqs', qj, k_ctx, preferred_element_type=jnp.float32) * ATTN_SCALE
        logits = jnp.concatenate([s_w, s_c, jnp.broadcast_to(sink_l, s_w.shape[:-1] + (1,))], axis=-1)
        p = jax.nn.softmax(logits, axis=-1)
        p_w = p[..., :3 * BLOCK].astype(v.dtype)
        p_c = p[..., 3 * BLOCK:-1].astype(v.dtype)
        return jnp.einsum('bkgqs,bskd->bqkgd', p_w, vw) + jnp.einsum('bkgqs,bskd->bqkgd', p_c, v_ctx)

    out = lax.map(one_block, (jnp.arange(nb), qb))
    return out.swapaxes(0, 1).reshape(B, T, H * D)


def hgrn2_scan(q, k, v, logf, s0):
    B, T, H, _ = q.shape
    n = T // HG_CHUNK

    def chunks(a):
        return a.reshape(B, n, HG_CHUNK, H, a.shape[-1]).transpose(1, 0, 3, 2, 4)

    causal = jnp.tril(jnp.ones((HG_CHUNK, HG_CHUNK), dtype=bool))[:, :, None]

    def step(S, inp):
        qc, kc, vc, gc = inp
        b = jnp.cumsum(gc, axis=2)
        o = jnp.einsum('bhtd,bhdv->bhtv', qc * jnp.exp(b), S)
        diff = b[:, :, :, None, :] - b[:, :, None, :, :]
        decay = jnp.where(causal, jnp.exp(jnp.where(causal, diff, 0.0)), 0.0)
        att = jnp.einsum('bhtd,bhsd,bhtsd->bhts', qc, kc, decay)
        o = o + jnp.einsum('bhts,bhsv->bhtv', att, vc)
        b_last = b[:, :, -1:, :]
        S = jnp.exp(b_last[:, :, 0, :])[..., None] * S + jnp.einsum('bhsd,bhsv->bhdv', kc * jnp.exp(b_last - b), vc)
        return S, o

    S, o = lax.scan(step, s0, (chunks(q), chunks(k), chunks(v), chunks(logf)))
    return o.transpose(1, 0, 3, 2, 4).reshape(B, T, H, -1), S


def hgrn2_mixer(hq, hf_fwd, hf_bwd, hi, hg, lb_pair, norm_g, s_f0, s_b0):
    B, T, _ = hq.shape

    def heads(a):
        return a.astype(jnp.float32).reshape(B, T, HG_HEADS, -1)

    q = jax.nn.silu(heads(hq)) * HG_DK ** -0.5
    v = heads(hi)

    def decay(f_logit, lb):
        f_logit = heads(f_logit)
        lb = lb.reshape(HG_HEADS, HG_DK)
        logf = jax.nn.log_sigmoid(f_logit) + jnp.log1p(lb * jnp.exp(-f_logit))
        k = (1 - lb) * jax.nn.sigmoid(-f_logit)
        return k, logf

    k_f, g_f = decay(hf_fwd, lb_pair[0])
    k_b, g_b = decay(hf_bwd, lb_pair[1])
    o_f, s_f = hgrn2_scan(q, k_f, v, g_f, s_f0)
    rev = lambda a: jnp.flip(a, axis=1)
    o_b, s_b = hgrn2_scan(rev(q), rev(k_b), rev(v), rev(g_b), s_b0)
    o = o_f + rev(o_b)
    o = rms_norm(o, norm_g) * jax.nn.silu(heads(hg))
    return o.reshape(B, T, HG_W).astype(hq.dtype), s_f, s_b


def trunk_layer(x, mod, p, l, lb_pair, rope, ctx):
    B, T, _ = x.shape
    shift1, scale1, gate1, shift2, scale2, gate2 = jnp.split(mod, 6, axis=-1)
    h = modulate(x, p['norm1_g'][l], shift1, scale1)
    (qa, ka, va, conf_a, conf_g, sc_b, sc_c, sc_h,
     hq, hf_fwd, hf_bwd, hi, hg) = split_columns(h @ p['w_in'][l])
    qa = rms_norm(qa.reshape(B, T, N_HEADS_A, HEAD_DIM), p['q_norm_g'][l])
    ka = rms_norm(ka.reshape(B, T, N_KV_A, HEAD_DIM), p['k_norm_g'][l])
    va = va.reshape(B, T, N_KV_A, HEAD_DIM)
    sink = p['attn_sink'][l]
    if ctx is None:
        attn = context_attention(qa, ka, va, sink)
        s_f0 = jnp.zeros((B, HG_HEADS, HG_DK, HG_DV), jnp.float32)
        s_b0 = s_f0
    else:
        k_ctx, v_ctx, s_f0, s_b0 = ctx
        cos, sin = rope
        attn = latent_window_attention(apply_rope(qa, cos, sin), apply_rope(ka, cos, sin), va, k_ctx, v_ctx, sink)
    y_a = attn @ p['w_attn_out'][l]
    u = depthwise_conv(conf_a * jax.nn.sigmoid(conf_g), p['conf_dw_w'][l]) + p['conf_dw_b'][l]
    u = layer_norm(u, p['conf_ln_g'][l], p['conf_ln_b'][l])
    y_b = jax.nn.silu(u) @ p['w_conf_out'][l]
    y_c = (sc_b * depthwise_conv(sc_c * sc_h, p['sc_conv_w'][l])) @ p['w_sc_out'][l]
    o_d, s_f, s_b = hgrn2_mixer(hq, hf_fwd, hf_bwd, hi, hg, lb_pair, p['hg_norm_g'][l],
                                s_f0.astype(jnp.float32), s_b0.astype(jnp.float32))
    y_d = o_d @ p['w_hg_out'][l]
    gates = jax.nn.sigmoid(h @ p['w_gate'][l] + p['b_gate'][l]).reshape(B, T, N_BRANCH, D_MODEL)
    merged = jnp.einsum('btnd,btnd->btd', gates, jnp.stack([y_a, y_b, y_c, y_d], axis=2))
    x = x + gate1 * (merged @ p['w_o'][l])
    h2 = modulate(x, p['norm2_g'][l], shift2, scale2)
    ff = (jax.nn.silu(h2 @ p['ffn_w1'][l]) * (h2 @ p['ffn_w3'][l])) @ p['ffn_w2'][l]
    x = x + gate2 * ff
    produced = (ka, va, s_f, s_b) if ctx is None else None
    return x, produced


def setup_inputs(seed: int = 0) -> dict:
    key = jax.random.key(seed)
    ks = jax.random.split(key, 40)
    D = D_MODEL

    def nrm(k, shape, scale=1.0):
        return jax.random.normal(k, shape, jnp.float32) * scale

    return {
        'x_prompt': nrm(ks[0], (BATCH, SEQ, D)),
        'x_sample': nrm(ks[1], (DEC_BATCH, DEC_SEQ, D)),
        'cache_k': nrm(ks[2], (DEC_BATCH, DEPTH, PAST_LEN, N_KV_A, HEAD_DIM)),
        'cache_v': nrm(ks[3], (DEC_BATCH, DEPTH, PAST_LEN, N_KV_A, HEAD_DIM)),
        'state_hgrn_fwd': nrm(ks[4], (DEC_BATCH, DEPTH, HG_HEADS, HG_DK, HG_DV), 0.3),
        'state_hgrn_bwd': nrm(ks[5], (DEC_BATCH, DEPTH, HG_HEADS, HG_DK, HG_DV), 0.3),
        'c': nrm(ks[6], (DEC_BATCH, D)),
        'c_ctx': nrm(ks[7], (D,)),
        'ada_w': nrm(ks[8], (DEPTH, D, 6 * D), 0.5 * D ** -0.5),
        'ada_b': nrm(ks[9], (DEPTH, 6 * D), 0.01),
        'norm1_g': 1.0 + nrm(ks[10], (DEPTH, D), 0.02),
        'norm2_g': 1.0 + nrm(ks[11], (DEPTH, D), 0.02),
        'w_in': nrm(ks[12], (DEPTH, D, IN_COLS), D ** -0.5),
        'q_norm_g': 1.0 + nrm(ks[13], (DEPTH, HEAD_DIM), 0.02),
        'k_norm_g': 1.0 + nrm(ks[14], (DEPTH, HEAD_DIM), 0.02),
        'attn_sink': nrm(ks[15], (DEPTH, N_HEADS_A), 0.5),
        'w_attn_out': nrm(ks[16], (DEPTH, A_WIDTH, D), A_WIDTH ** -0.5),
        'conf_dw_w': nrm(ks[17], (DEPTH, CONF_K, CONF_W), CONF_K ** -0.5),
        'conf_dw_b': nrm(ks[18], (DEPTH, CONF_W), 0.01),
        'conf_ln_g': 1.0 + nrm(ks[19], (DEPTH, CONF_W), 0.02),
        'conf_ln_b': nrm(ks[20], (DEPTH, CONF_W), 0.01),
        'w_conf_out': nrm(ks[21], (DEPTH, CONF_W, D), CONF_W ** -0.5),
        'sc_conv_w': nrm(ks[22], (DEPTH, SC_K, SC_W), SC_K ** -0.5),
        'w_sc_out': nrm(ks[23], (DEPTH, SC_W, D), SC_W ** -0.5),
        'hg_lb': nrm(ks[24], (2, DEPTH, HG_W), 0.1),
        'hg_norm_g': 1.0 + nrm(ks[25], (DEPTH, HG_DV), 0.02),
        'w_hg_out': nrm(ks[26], (DEPTH, HG_W, D), HG_W ** -0.5),
        'w_gate': nrm(ks[27], (DEPTH, D, N_BRANCH * D), D ** -0.5),
        'b_gate': nrm(ks[28], (DEPTH, N_BRANCH * D), 0.01),
        'w_o': nrm(ks[29], (DEPTH, D, D), D ** -0.5),
        'ffn_w1': nrm(ks[30], (DEPTH, D, D_FF), D ** -0.5),
        'ffn_w3': nrm(ks[31], (DEPTH, D, D_FF), D ** -0.5),
        'ffn_w2': nrm(ks[32], (DEPTH, D_FF, D), D_FF ** -0.5),
    }


def reference(x_prompt, x_sample, cache_k, cache_v, state_hgrn_fwd, state_hgrn_bwd, c, c_ctx,
              ada_w, ada_b, norm1_g, norm2_g, w_in, q_norm_g, k_norm_g, attn_sink, w_attn_out,
              conf_dw_w, conf_dw_b, conf_ln_g, conf_ln_b, w_conf_out, sc_conv_w, w_sc_out,
              hg_lb, hg_norm_g, w_hg_out, w_gate, b_gate, w_o, ffn_w1, ffn_w3, ffn_w2):
    p = {
        'norm1_g': norm1_g, 'norm2_g': norm2_g, 'w_in': w_in, 'q_norm_g': q_norm_g, 'k_norm_g': k_norm_g,
        'attn_sink': attn_sink, 'w_attn_out': w_attn_out, 'conf_dw_w': conf_dw_w, 'conf_dw_b': conf_dw_b,
        'conf_ln_g': conf_ln_g, 'conf_ln_b': conf_ln_b, 'w_conf_out': w_conf_out, 'sc_conv_w': sc_conv_w,
        'w_sc_out': w_sc_out, 'hg_norm_g': hg_norm_g, 'w_hg_out': w_hg_out, 'w_gate': w_gate,
        'b_gate': b_gate, 'w_o': w_o, 'ffn_w1': ffn_w1, 'ffn_w3': ffn_w3, 'ffn_w2': ffn_w2,
    }
    lb = jax.nn.softmax(hg_lb.astype(jnp.float32), axis=1)
    lb = jnp.maximum(jnp.cumsum(lb, axis=1) - lb[:, :1], 0.0)
    rope = axial_rope_tables(x_sample.shape[1])
    xp = x_prompt
    xs = x_sample
    ks_out, vs_out, sf_out, sb_out = [], [], [], []
    for l in range(DEPTH):
        mod_ctx = jax.nn.silu(c_ctx) @ ada_w[l] + ada_b[l]
        mod_lat = (jax.nn.silu(c) @ ada_w[l] + ada_b[l])[:, None, :]
        xp, (k_l, v_l, sf_l, sb_l) = trunk_layer(xp, mod_ctx, p, l, lb[:, l], None, None)
        xs, _ = trunk_layer(xs, mod_lat, p, l, lb[:, l], rope,
                            (cache_k[:, l], cache_v[:, l], state_hgrn_fwd[:, l], state_hgrn_bwd[:, l]))
        ks_out.append(k_l)
        vs_out.append(v_l)
        sf_out.append(sf_l.astype(x_prompt.dtype))
        sb_out.append(sb_l.astype(x_prompt.dtype))
    new_cache_k = jnp.stack(ks_out, axis=1)
    new_cache_v = jnp.stack(vs_out, axis=1)
    new_state_hgrn_fwd = jnp.stack(sf_out, axis=1)
    new_state_hgrn_bwd = jnp.stack(sb_out, axis=1)
    return (xp, xs, new_cache_k, new_cache_v, new_state_hgrn_fwd, new_state_hgrn_bwd)
```

```python
import functools

import jax
import jax.numpy as jnp
from jax import lax
from jax.experimental import pallas as pl
from jax.experimental.pallas import tpu as pltpu

F32 = jnp.float32
BF16 = jnp.bfloat16

V7X_VMEM_BYTES = 64 * 1024 * 1024
VMEM_LIMIT_BYTES = V7X_VMEM_BYTES - 8 * 1024 * 1024
LANES = 128

GRID_W = 64
HEAD_DIM = 128
N_HEADS_A = 8
N_KV_A = 2
GROUP_A = N_HEADS_A // N_KV_A
A_WIDTH = N_HEADS_A * HEAD_DIM
KV_WIDTH = N_KV_A * HEAD_DIM
WINDOW = 128
BLOCK = 128
ROPE_BASE = 10000.0
ROT_AXIS = HEAD_DIM // 2
ATTN_SCALE = HEAD_DIM ** -0.5
NEG_MASK = -1e9
CONF_W = 512
CONF_K = 31
SC_W = 512
SC_K = 3
HG_HEADS = 4
HG_DK = 128
HG_DV = 128
HG_W = HG_HEADS * HG_DK
HG_CHUNK = 32
N_BRANCH = 4
EPS = 1e-6

COL_W = 512
COL_CONF_A, COL_CONF_G, COL_SC_B, COL_SC_C, COL_SC_H = 3, 4, 5, 6, 7
COL_HQ, COL_HF_FWD, COL_HF_BWD, COL_HI, COL_HG = 8, 9, 10, 11, 12
KV_COL_K, KV_COL_V = A_WIDTH // KV_WIDTH, A_WIDTH // KV_WIDTH + 1

SEQ_TILE = 256
HALO = 16
HG_SAFE_LOG_DECAY = -60.0

MOD_SHIFT1, MOD_SCALE1, MOD_GATE1, MOD_SHIFT2, MOD_SCALE2, MOD_GATE2 = range(6)
MOD_ROWS = 8


def _cparams(*semantics):
    return pltpu.CompilerParams(dimension_semantics=semantics, vmem_limit_bytes=VMEM_LIMIT_BYTES)


def _sigmoid(x):
    return 1.0 / (1.0 + jnp.exp(-x))


def _silu(x):
    return x * _sigmoid(x)


def _chunk_of(idx, size):
    assert size & (size - 1) == 0
    return jnp.right_shift(idx, size.bit_length() - 1)


def _dot(a, b):
    return jnp.dot(a, b, preferred_element_type=F32)


def _dot_nt(a, b):
    return lax.dot_general(a, b, (((1,), (1,)), ((), ())), preferred_element_type=F32)


class _Layout:
    def __init__(self, batch, seq, dec_batch, dec_seq):
        self.batch, self.seq, self.dec_batch, self.dec_seq = batch, seq, dec_batch, dec_seq
        self.p_rows = batch * seq
        self.s_rows = dec_batch * dec_seq
        self.rows = self.p_rows + self.s_rows
        self.n_seq = batch + dec_batch

    def row_tile(self, preferred):
        t = preferred
        while self.p_rows % t or self.dec_seq % t:
            t //= 2
        return t

    def mod_row(self, i, tile):
        p_tiles = self.p_rows // tile
        per = self.dec_seq // tile
        return jnp.where(i < p_tiles, 0, 1 + (i - p_tiles) // per)

    def seq_pos(self, n, tile):
        p_tiles = self.p_rows // tile
        tpp = self.seq // tile
        tps = self.dec_seq // tile
        is_p = n < p_tiles
        seq_idx = jnp.where(is_p, n // tpp, self.batch + (n - p_tiles) // tps)
        pos = jnp.where(is_p, n % tpp, (n - p_tiles) % tps)
        count = jnp.where(is_p, tpp, tps)
        return seq_idx, pos, count


def _mod_kernel(c_ref, w_ref, b_ref, o_ref):
    s = _silu(c_ref[...]).astype(BF16)
    o_ref[...] = _dot(s, w_ref[...].astype(BF16)) + b_ref[...]


def _modulation(cvec, ada_w, ada_b):
    depth, d, n = ada_w.shape
    tn = 1024
    return pl.pallas_call(
        _mod_kernel,
        grid=(depth, n // tn),
        in_specs=[
            pl.BlockSpec((MOD_ROWS, d), lambda l, j: (0, 0)),
            pl.BlockSpec((None, d, tn), lambda l, j: (l, 0, j)),
            pl.BlockSpec((None, 1, tn), lambda l, j: (l, 0, j)),
        ],
        out_specs=pl.BlockSpec((None, MOD_ROWS, tn), lambda l, j: (l, 0, j)),
        out_shape=jax.ShapeDtypeStruct((depth, MOD_ROWS, n), F32),
        compiler_params=_cparams("parallel", "parallel"),
        name="modulation",
    )(cvec, ada_w, ada_b.reshape(depth, 1, n))


def _modulated_norm(x, g, shift, scale):
    y = x * lax.rsqrt(jnp.mean(x * x, axis=-1, keepdims=True) + EPS) * g
    return y * (1.0 + scale) + shift


def _inproj_kernel(x_ref, mod_ref, g_ref, w_ref, z_ref, h_ref):
    @pl.when(pl.program_id(1) == 0)
    def _():
        h = _modulated_norm(x_ref[...], g_ref[...], mod_ref[MOD_SHIFT1:MOD_SHIFT1 + 1, :],
                            mod_ref[MOD_SCALE1:MOD_SCALE1 + 1, :])
        h_ref[...] = h.astype(BF16)

    z_ref[...] = _dot(h_ref[...], w_ref[...])


def _in_projection(lay, l, x, mods, norm_g, w_in):
    m, d = x.shape
    n = w_in.shape[-1]
    tm = lay.row_tile(1024)
    tn = 512
    return pl.pallas_call(
        _inproj_kernel,
        grid=(m // tm, n // tn),
        in_specs=[
            pl.BlockSpec((tm, d), lambda i, j: (i, 0)),
            pl.BlockSpec((None, None, MOD_ROWS, d), lambda i, j: (l, lay.mod_row(i, tm), 0, 0)),
            pl.BlockSpec((None, 1, d), lambda i, j: (l, 0, 0)),
            pl.BlockSpec((None, d, tn), lambda i, j: (l, 0, j)),
        ],
        out_specs=[
            pl.BlockSpec((tm, tn), lambda i, j: (i, j)),
            pl.BlockSpec((tm, d), lambda i, j: (i, 0)),
        ],
        out_shape=[jax.ShapeDtypeStruct((m, n), F32), jax.ShapeDtypeStruct((m, d), BF16)],
        compiler_params=_cparams("parallel", "arbitrary"),
        name="in_projection",
    )(x, mods, norm_g.reshape(norm_g.shape[0], 1, d), w_in)


def _residual_matmul_kernel(a_ref, w_ref, x_ref, mod_ref, o_ref, *, gate_row):
    o_ref[...] = x_ref[...] + mod_ref[gate_row:gate_row + 1, :] * _dot(a_ref[...], w_ref[...])


def _residual_matmul(lay, l, a, w, x, mods, gate_row, name):
    m, k = a.shape
    d = w.shape[-1]
    tm = lay.row_tile(512)
    tn = 512
    return pl.pallas_call(
        functools.partial(_residual_matmul_kernel, gate_row=gate_row),
        grid=(m // tm, d // tn),
        in_specs=[
            pl.BlockSpec((tm, k), lambda i, j: (i, 0)),
            pl.BlockSpec((None, k, tn), lambda i, j: (l, 0, j)),
            pl.BlockSpec((tm, tn), lambda i, j: (i, j)),
            pl.BlockSpec((None, None, MOD_ROWS, tn), lambda i, j: (l, lay.mod_row(i, tm), 0, j)),
        ],
        out_specs=pl.BlockSpec((tm, tn), lambda i, j: (i, j)),
        out_shape=jax.ShapeDtypeStruct((m, d), F32),
        compiler_params=_cparams("parallel", "arbitrary"),
        name=name,
    )(a, w, x, mods)


def _ffn_up_kernel(x_ref, mod_ref, g_ref, w1_ref, w3_ref, o_ref, h_ref):
    @pl.when(pl.program_id(1) == 0)
    def _():
        h = _modulated_norm(x_ref[...], g_ref[...], mod_ref[MOD_SHIFT2:MOD_SHIFT2 + 1, :],
                            mod_ref[MOD_SCALE2:MOD_SCALE2 + 1, :])
        h_ref[...] = h.astype(BF16)

    h = h_ref[...]
    o_ref[...] = (_silu(_dot(h, w1_ref[...])) * _dot(h, w3_ref[...])).astype(BF16)


def _ffn_up(lay, l, x, mods, norm_g, w1, w3):
    m, d = x.shape
    n = w1.shape[-1]
    tm = lay.row_tile(1024)
    tn = 512
    return pl.pallas_call(
        _ffn_up_kernel,
        grid=(m // tm, n // tn),
        in_specs=[
            pl.BlockSpec((tm, d), lambda i, j: (i, 0)),
            pl.BlockSpec((None, None, MOD_ROWS, d), lambda i, j: (l, lay.mod_row(i, tm), 0, 0)),
            pl.BlockSpec((None, 1, d), lambda i, j: (l, 0, 0)),
            pl.BlockSpec((None, d, tn), lambda i, j: (l, 0, j)),
            pl.BlockSpec((None, d, tn), lambda i, j: (l, 0, j)),
        ],
        out_specs=pl.BlockSpec((tm, tn), lambda i, j: (i, j)),
        out_shape=jax.ShapeDtypeStruct((m, n), BF16),
        scratch_shapes=[pltpu.VMEM((tm, d), BF16)],
        compiler_params=_cparams("parallel", "arbitrary"),
        name="ffn_up",
    )(x, mods, norm_g.reshape(norm_g.shape[0], 1, d), w1, w3)


def _qk_prep_kernel(q_ref, k_ref, cos_ref, sa_ref, sb_ref, qg_ref, kg_ref, qo_ref, ko_ref, kf_ref):
    cos, sin_a, sin_b = cos_ref[...], sa_ref[...], sb_ref[...]

    def norm(x, g):
        return x * lax.rsqrt(jnp.mean(x * x, axis=-1, keepdims=True) + EPS) * g

    def rope(x):
        return (x * cos + pltpu.roll(x, HEAD_DIM - ROT_AXIS // 2, 1) * sin_a
                + pltpu.roll(x, ROT_AXIS // 2, 1) * sin_b)

    for h in range(N_HEADS_A):
        sl = slice(h * HEAD_DIM, (h + 1) * HEAD_DIM)
        qo_ref[:, sl] = (rope(norm(q_ref[:, sl], qg_ref[...])) * ATTN_SCALE).astype(BF16)
    for h in range(N_KV_A):
        sl = slice(h * HEAD_DIM, (h + 1) * HEAD_DIM)
        kn = norm(k_ref[:, sl], kg_ref[...])
        kf_ref[:, sl] = kn
        ko_ref[:, sl] = rope(kn).astype(BF16)


def _qk_prep(lay, l, z, rope_tabs, q_norm_g, k_norm_g):
    m = z.shape[0]
    tm = lay.row_tile(512)
    cos, sin_a, sin_b = rope_tabs
    tab_spec = pl.BlockSpec((tm, HEAD_DIM), lambda i: (i, 0))
    g_spec = pl.BlockSpec((None, 1, HEAD_DIM), lambda i: (l, 0, 0))
    return pl.pallas_call(
        _qk_prep_kernel,
        grid=(m // tm,),
        in_specs=[
            pl.BlockSpec((tm, A_WIDTH), lambda i: (i, 0)),
            pl.BlockSpec((tm, KV_WIDTH), lambda i: (i, KV_COL_K)),
            tab_spec, tab_spec, tab_spec, g_spec, g_spec,
        ],
        out_specs=[
            pl.BlockSpec((tm, A_WIDTH), lambda i: (i, 0)),
            pl.BlockSpec((tm, KV_WIDTH), lambda i: (i, 0)),
            pl.BlockSpec((tm, KV_WIDTH), lambda i: (i, 0)),
        ],
        out_shape=[
            jax.ShapeDtypeStruct((m, A_WIDTH), BF16),
            jax.ShapeDtypeStruct((m, KV_WIDTH), BF16),
            jax.ShapeDtypeStruct((m, KV_WIDTH), F32),
        ],
        compiler_params=_cparams("parallel"),
        name="qk_prep",
    )(z, z, cos, sin_a, sin_b, q_norm_g.reshape(-1, 1, HEAD_DIM), k_norm_g.reshape(-1, 1, HEAD_DIM))


def _stack_heads(q_ref, kv):
    return jnp.concatenate(
        [q_ref[:, (kv * GROUP_A + g) * HEAD_DIM:(kv * GROUP_A + g + 1) * HEAD_DIM] for g in range(GROUP_A)], axis=0)


def _sink_column(sink_ref, kv, rows):
    return jnp.concatenate(
        [jnp.full((rows, 1), sink_ref[kv * GROUP_A + g], F32) for g in range(GROUP_A)], axis=0)


def _context_attention_kernel(sink_ref, q_ref, k_ref, v_ref, o_ref):
    rows = q_ref.shape[0]
    for kv in range(N_KV_A):
        sl = slice(kv * HEAD_DIM, (kv + 1) * HEAD_DIM)
        q = _stack_heads(q_ref, kv)
        s = _dot_nt(q, k_ref[:, sl])
        sink = _sink_column(sink_ref, kv, rows)
        mx = jnp.maximum(jnp.max(s, axis=-1, keepdims=True), sink)
        p = jnp.exp(s - mx)
        den = jnp.sum(p, axis=-1, keepdims=True) + jnp.exp(sink - mx)
        o = _dot((p * (1.0 / den)).astype(BF16), v_ref[:, sl].astype(BF16))
        for g in range(GROUP_A):
            h = kv * GROUP_A + g
            o_ref[:, h * HEAD_DIM:(h + 1) * HEAD_DIM] = o[g * rows:(g + 1) * rows].astype(BF16)


def _context_attention(lay, l, qn, kn, z, sink):
    seq = lay.seq
    return pl.pallas_call(
        _context_attention_kernel,
        grid=(lay.batch,),
        in_specs=[
            pl.BlockSpec(memory_space=pltpu.SMEM),
            pl.BlockSpec((seq, A_WIDTH), lambda b: (b, 0)),
            pl.BlockSpec((seq, KV_WIDTH), lambda b: (b, 0)),
            pl.BlockSpec((seq, KV_WIDTH), lambda b: (b, KV_COL_V)),
        ],
        out_specs=pl.BlockSpec((seq, A_WIDTH), lambda b: (b, 0)),
        out_shape=jax.ShapeDtypeStruct((lay.p_rows, A_WIDTH), BF16),
        compiler_params=_cparams("parallel"),
        name="context_attention",
    )(sink, qn, kn, z)


def _latent_attention_kernel(sink_ref, q_ref, kp_ref, kc_ref, kn_ref, vp_ref, vc_ref, vn_ref, ck_ref, cv_ref,
                             o_ref, *, seq_len):
    j = pl.program_id(1)
    r = lax.broadcasted_iota(jnp.int32, (BLOCK, 3 * BLOCK), 0)
    c = lax.broadcasted_iota(jnp.int32, (BLOCK, 3 * BLOCK), 1)
    kpos = (j - 1) * BLOCK + c
    valid = (jnp.abs(c - BLOCK - r) <= WINDOW) & (kpos >= 0) & (kpos < seq_len)
    valid = jnp.concatenate([valid] * GROUP_A, axis=0)
    for kv in range(N_KV_A):
        sl = slice(kv * HEAD_DIM, (kv + 1) * HEAD_DIM)
        k_win = jnp.concatenate([kp_ref[:, sl], kc_ref[:, sl], kn_ref[:, sl]], axis=0)
        v_win = jnp.concatenate([vp_ref[:, sl], vc_ref[:, sl], vn_ref[:, sl]], axis=0).astype(BF16)
        q = _stack_heads(q_ref, kv)
        s_w = jnp.where(valid, _dot_nt(q, k_win), NEG_MASK)
        s_c = _dot_nt(q, ck_ref[:, sl].astype(BF16))
        sink = _sink_column(sink_ref, kv, BLOCK)
        mx = jnp.maximum(jnp.maximum(jnp.max(s_w, axis=-1, keepdims=True), jnp.max(s_c, axis=-1, keepdims=True)),
                         sink)
        p_w = jnp.exp(s_w - mx)
        p_c = jnp.exp(s_c - mx)
        den = jnp.sum(p_w, axis=-1, keepdims=True) + jnp.sum(p_c, axis=-1, keepdims=True) + jnp.exp(sink - mx)
        inv = 1.0 / den
        o = _dot((p_w * inv).astype(BF16), v_win) + _dot((p_c * inv).astype(BF16), cv_ref[:, sl].astype(BF16))
        for g in range(GROUP_A):
            h = kv * GROUP_A + g
            o_ref[:, h * HEAD_DIM:(h + 1) * HEAD_DIM] = o[g * BLOCK:(g + 1) * BLOCK].astype(BF16)


def _latent_attention(lay, l, qn, kn, z, ctx_k, ctx_v, sink):
    nb = lay.dec_seq // BLOCK
    base = lay.p_rows // BLOCK
    past = ctx_k.shape[2]

    def blk(shift, col):
        return lambda b, j: (base + b * nb + jnp.clip(j + shift, 0, nb - 1), col)

    k_spec = [pl.BlockSpec((BLOCK, KV_WIDTH), blk(s, 0)) for s in (-1, 0, 1)]
    v_spec = [pl.BlockSpec((BLOCK, KV_WIDTH), blk(s, KV_COL_V)) for s in (-1, 0, 1)]
    ctx_spec = pl.BlockSpec((None, None, past, KV_WIDTH), lambda b, j: (b, l, 0, 0))
    return pl.pallas_call(
        functools.partial(_latent_attention_kernel, seq_len=lay.dec_seq),
        grid=(lay.dec_batch, nb),
        in_specs=[pl.BlockSpec(memory_space=pltpu.SMEM), pl.BlockSpec((BLOCK, A_WIDTH), blk(0, 0))]
        + k_spec + v_spec + [ctx_spec, ctx_spec],
        out_specs=pl.BlockSpec((BLOCK, A_WIDTH), lambda b, j: (b * nb + j, 0)),
        out_shape=jax.ShapeDtypeStruct((lay.s_rows, A_WIDTH), BF16),
        compiler_params=_cparams("parallel", "parallel"),
        name="latent_attention",
    )(sink, qn, kn, kn, kn, z, z, z, ctx_k, ctx_v)


def _conv_kernel(a_c, a_p, a_n, g_c, g_p, g_n, b_c, c_c, c_p, c_n, h_c, h_p, h_n,
                 dw_ref, db_ref, lg_ref, lb_ref, sw_ref, ub_ref, uc_ref, xb_s, xc_s, *, lay):
    _, pos, count = lay.seq_pos(pl.program_id(0), SEQ_TILE)
    has_prev = pos > 0
    has_next = pos < count - 1
    xb_s[0:HALO, :] = jnp.where(has_prev, a_p[...] * _sigmoid(g_p[...]), 0.0)
    xb_s[HALO:HALO + SEQ_TILE, :] = a_c[...] * _sigmoid(g_c[...])
    xb_s[HALO + SEQ_TILE:, :] = jnp.where(has_next, a_n[...] * _sigmoid(g_n[...]), 0.0)
    xc_s[0:HALO, :] = jnp.where(has_prev, c_p[...] * h_p[...], 0.0)
    xc_s[HALO:HALO + SEQ_TILE, :] = c_c[...] * h_c[...]
    xc_s[HALO + SEQ_TILE:, :] = jnp.where(has_next, c_n[...] * h_n[...], 0.0)
    rb = 32
    for r0 in range(0, SEQ_TILE, rb):
        acc = jnp.zeros((rb, CONF_W), F32)
        for k in range(CONF_K):
            off = HALO - CONF_K // 2 + r0 + k
            acc = acc + dw_ref[k:k + 1, :] * xb_s[off:off + rb, :]
        u = acc + db_ref[...]
        uc = u - jnp.mean(u, axis=-1, keepdims=True)
        var = jnp.mean(uc * uc, axis=-1, keepdims=True)
        u = uc * lax.rsqrt(var + EPS) * lg_ref[...] + lb_ref[...]
        ub_ref[r0:r0 + rb, :] = _silu(u).astype(BF16)
        acc = jnp.zeros((rb, SC_W), F32)
        for k in range(SC_K):
            off = HALO - SC_K // 2 + r0 + k
            acc = acc + sw_ref[k:k + 1, :] * xc_s[off:off + rb, :]
        uc_ref[r0:r0 + rb, :] = (b_c[r0:r0 + rb, :] * acc).astype(BF16)


def _conv_mixers(lay, l, z, conf_dw_w, conf_dw_b, conf_ln_g, conf_ln_b, sc_conv_w):
    m = z.shape[0]
    n_tiles = m // SEQ_TILE
    per = SEQ_TILE // HALO
    n_halo = m // HALO

    def cur(col):
        return pl.BlockSpec((SEQ_TILE, COL_W), lambda i: (i, col))

    def prev(col):
        return pl.BlockSpec((HALO, COL_W), lambda i: (jnp.maximum(i * per - 1, 0), col))

    def nxt(col):
        return pl.BlockSpec((HALO, COL_W), lambda i: (jnp.minimum((i + 1) * per, n_halo - 1), col))

    def trio(col):
        return [cur(col), prev(col), nxt(col)]

    def par(rows):
        return pl.BlockSpec((None, rows, COL_W), lambda i: (l, 0, 0))

    depth = conf_dw_w.shape[0]
    dw = jnp.pad(conf_dw_w, ((0, 0), (0, 32 - CONF_K), (0, 0)))
    sw = jnp.pad(sc_conv_w, ((0, 0), (0, 8 - SC_K), (0, 0)))
    return pl.pallas_call(
        functools.partial(_conv_kernel, lay=lay),
        grid=(n_tiles,),
        in_specs=trio(COL_CONF_A) + trio(COL_CONF_G) + [cur(COL_SC_B)] + trio(COL_SC_C) + trio(COL_SC_H)
        + [par(32), par(1), par(1), par(1), par(8)],
        out_specs=[pl.BlockSpec((SEQ_TILE, CONF_W), lambda i: (i, 0)),
                   pl.BlockSpec((SEQ_TILE, SC_W), lambda i: (i, 0))],
        out_shape=[jax.ShapeDtypeStruct((m, CONF_W), BF16), jax.ShapeDtypeStruct((m, SC_W), BF16)],
        scratch_shapes=[pltpu.VMEM((SEQ_TILE + 2 * HALO, CONF_W), F32),
                        pltpu.VMEM((SEQ_TILE + 2 * HALO, SC_W), F32)],
        compiler_params=_cparams("parallel"),
        name="conv_mixers",
    )(*([z] * 13), dw, conf_dw_b.reshape(depth, 1, CONF_W), conf_ln_g.reshape(depth, 1, CONF_W),
      conf_ln_b.reshape(depth, 1, CONF_W), sw)


def _hgrn_kernel(hq_ref, hf_ref, hi_ref, lb_ref, s0_ref, o_ref, sf_ref, st_ref, a_ref, q_s, k_s, b_s,
                 *, lay, reverse):
    ts = SEQ_TILE
    n_chunks = ts // HG_CHUNK
    _, pos, count = lay.seq_pos(pl.program_id(0), ts)

    @pl.when(pos == 0)
    def _():
        st_ref[...] = s0_ref[...]

    zq = hq_ref[...]
    q = _silu(zq) * (HG_DK ** -0.5)
    zf = hf_ref[...]
    lb = lb_ref[...]
    e = jnp.exp(-jnp.abs(zf))
    r = 1.0 / (1.0 + e)
    nonneg = zf >= 0
    sig_pos = jnp.where(nonneg, r, e * r)
    sig_neg = jnp.where(nonneg, e * r, r)
    logf = jnp.log(lb + (1.0 - lb) * sig_pos)
    kk = (1.0 - lb) * sig_neg

    row = lax.broadcasted_iota(jnp.int32, (ts, ts), 0)
    col = lax.broadcasted_iota(jnp.int32, (ts, ts), 1)
    same = _chunk_of(row, HG_CHUNK) == _chunk_of(col, HG_CHUNK)
    tri = same & ((col >= row) if reverse else (col <= row))
    lhs = jnp.concatenate([tri, same], axis=0).astype(F32).astype(BF16)
    g1 = logf.astype(BF16)
    r1 = logf - g1.astype(F32)
    g2 = r1.astype(BF16)
    g3 = (r1 - g2.astype(F32)).astype(BF16)
    sums = _dot(lhs, g1) + _dot(lhs, g2) + _dot(lhs, g3)
    b = sums[:ts]
    btot = sums[ts:]

    qt = q * jnp.exp(b)
    kend = kk * jnp.exp(btot - b)
    decay = jnp.exp(btot)
    v = hi_ref[...]
    safe = jnp.min(b) > HG_SAFE_LOG_DECAY

    @pl.when(safe)
    def _():
        kinv = (kk * jnp.exp(-b)).astype(BF16)
        qb = qt.astype(BF16)
        for hd in range(HG_HEADS):
            sl = slice(hd * HG_DK, (hd + 1) * HG_DK)
            a_ref[hd] = jnp.where(tri, _dot_nt(qb[:, sl], kinv[:, sl]), 0.0)

    @pl.when(jnp.logical_not(safe))
    def _():
        a_ref[...] = jnp.zeros_like(a_ref)
        q_s[...] = q
        k_s[...] = kk
        b_s[...] = b

        def body(s, carry):
            c0 = pl.multiple_of((s // HG_CHUNK) * HG_CHUNK, HG_CHUNK)
            w = jnp.exp(jnp.minimum(b_s[pl.ds(c0, HG_CHUNK), :] - b_s[pl.ds(s, 1), :], 0.0))
            prod = q_s[pl.ds(c0, HG_CHUNK), :] * k_s[pl.ds(s, 1), :] * w
            trow = c0 + lax.broadcasted_iota(jnp.int32, (HG_CHUNK, 1), 0)
            ok = (trow <= s) if reverse else (trow >= s)
            onehot = (lax.broadcasted_iota(jnp.int32, (1, ts), 1) == s).astype(F32)
            for hd in range(HG_HEADS):
                colv = jnp.sum(prod[:, hd * HG_DK:(hd + 1) * HG_DK], axis=-1, keepdims=True)
                a_ref[hd, pl.ds(c0, HG_CHUNK), :] += jnp.where(ok, colv, 0.0) * onehot
            return carry

        lax.fori_loop(0, ts, body, 0)

    blk_row = _chunk_of(lax.broadcasted_iota(jnp.int32, (ts, n_chunks * HG_DK), 0), HG_CHUNK)
    blk_col = _chunk_of(lax.broadcasted_iota(jnp.int32, (ts, n_chunks * HG_DK), 1), HG_DK)
    blk = blk_row == blk_col
    order = range(n_chunks - 1, -1, -1) if reverse else range(n_chunks)
    for hd in range(HG_HEADS):
        sl = slice(hd * HG_DK, (hd + 1) * HG_DK)
        v_h = v[:, sl]
        o_intra = _dot(a_ref[hd].astype(BF16), v_h.astype(BF16))
        k_blocks = jnp.where(blk, jnp.concatenate([kend[:, sl]] * n_chunks, axis=1), 0.0).astype(BF16)
        incr = _dot(v_h.T.astype(BF16), k_blocks)
        state = st_ref[hd]
        starts = [None] * n_chunks
        for c in order:
            starts[c] = state
            state = state * decay[c * HG_CHUNK:c * HG_CHUNK + 1, sl] + incr[:, c * HG_DK:(c + 1) * HG_DK]
        st_ref[hd] = state
        q_blocks = jnp.where(blk, jnp.concatenate([qt[:, sl]] * n_chunks, axis=1), 0.0).astype(BF16)
        o_inter = _dot_nt(q_blocks, jnp.concatenate(starts, axis=1).astype(BF16))
        o_ref[:, sl] = o_intra + o_inter

    @pl.when(pos == count - 1)
    def _():
        sf_ref[...] = st_ref[...]


def _hgrn_direction(lay, z, lb, s0_t, reverse):
    m = z.shape[0]
    n_tiles = m // SEQ_TILE

    def tile_of(n):
        seq_idx, pos, count = lay.seq_pos(n, SEQ_TILE)
        return (n - pos + (count - 1 - pos)) if reverse else n

    def zcol(col):
        return pl.BlockSpec((SEQ_TILE, COL_W), lambda n: (tile_of(n), col))

    state_spec = pl.BlockSpec((None, HG_HEADS, HG_DV, HG_DK), lambda n: (lay.seq_pos(n, SEQ_TILE)[0], 0, 0, 0))
    return pl.pallas_call(
        functools.partial(_hgrn_kernel, lay=lay, reverse=reverse),
        grid=(n_tiles,),
        in_specs=[zcol(COL_HQ), zcol(COL_HF_BWD if reverse else COL_HF_FWD), zcol(COL_HI),
                  pl.BlockSpec((1, HG_W), lambda n: (0, 0)), state_spec],
        out_specs=[pl.BlockSpec((SEQ_TILE, HG_W), lambda n: (tile_of(n), 0)), state_spec],
        out_shape=[jax.ShapeDtypeStruct((m, HG_W), F32),
                   jax.ShapeDtypeStruct((lay.n_seq, HG_HEADS, HG_DV, HG_DK), F32)],
        scratch_shapes=[pltpu.VMEM((HG_HEADS, HG_DV, HG_DK), F32),
                        pltpu.VMEM((HG_HEADS, SEQ_TILE, SEQ_TILE), F32),
                        pltpu.VMEM((SEQ_TILE, HG_W), F32),
                        pltpu.VMEM((SEQ_TILE, HG_W), F32),
                        pltpu.VMEM((SEQ_TILE, HG_W), F32)],
        compiler_params=_cparams("arbitrary"),
        name="hgrn_bwd" if reverse else "hgrn_fwd",
    )(z, z, z, lb, s0_t)


def _hgrn_out_kernel(of_ref, ob_ref, hg_ref, g_ref, o_ref):
    for hd in range(HG_HEADS):
        sl = slice(hd * HG_DV, (hd + 1) * HG_DV)
        o = of_ref[:, sl] + ob_ref[:, sl]
        o = o * lax.rsqrt(jnp.mean(o * o, axis=-1, keepdims=True) + EPS) * g_ref[...]
        o_ref[:, sl] = (o * _silu(hg_ref[:, sl])).astype(BF16)


def _hgrn_output(lay, l, o_f, o_b, z, hg_norm_g):
    m = z.shape[0]
    tm = lay.row_tile(512)
    row_spec = pl.BlockSpec((tm, HG_W), lambda i: (i, 0))
    return pl.pallas_call(
        _hgrn_out_kernel,
        grid=(m // tm,),
        in_specs=[row_spec, row_spec, pl.BlockSpec((tm, COL_W), lambda i: (i, COL_HG)),
                  pl.BlockSpec((None, 1, HG_DV), lambda i: (l, 0, 0))],
        out_specs=row_spec,
        out_shape=jax.ShapeDtypeStruct((m, HG_W), BF16),
        compiler_params=_cparams("parallel"),
        name="hgrn_output",
    )(o_f, o_b, z, hg_norm_g.reshape(-1, 1, HG_DV))


def _merge_kernel(h_ref, a_ref, b_ref, c_ref, d_ref, wg0, wg1, wg2, wg3, bg0, bg1, bg2, bg3,
                  wa_ref, wb_ref, wc_ref, wd_ref, o_ref):
    h = h_ref[...]
    acc = None
    for br_ref, w_ref, wg_ref, bg_ref in ((a_ref, wa_ref, wg0, bg0), (b_ref, wb_ref, wg1, bg1),
                                          (c_ref, wc_ref, wg2, bg2), (d_ref, wd_ref, wg3, bg3)):
        gate = _sigmoid(_dot(h, wg_ref[...]) + bg_ref[...])
        term = gate * _dot(br_ref[...], w_ref[...])
        acc = term if acc is None else acc + term
    o_ref[...] = acc.astype(BF16)


def _merge(lay, l, h, branches, w_gate, b_gate, w_outs):
    m, d = h.shape
    tm = lay.row_tile(512)
    tn = 512
    nj = d // tn
    depth = w_gate.shape[0]

    def gate_w(n):
        return pl.BlockSpec((None, d, tn), lambda j, i: (l, 0, n * nj + j))

    def gate_b(n):
        return pl.BlockSpec((None, 1, tn), lambda j, i: (l, 0, n * nj + j))

    b_gate3 = b_gate.reshape(depth, 1, N_BRANCH * d)
    return pl.pallas_call(
        _merge_kernel,
        grid=(nj, m // tm),
        in_specs=[pl.BlockSpec((tm, d), lambda j, i: (i, 0))]
        + [pl.BlockSpec((tm, br.shape[1]), lambda j, i: (i, 0)) for br in branches]
        + [gate_w(n) for n in range(N_BRANCH)] + [gate_b(n) for n in range(N_BRANCH)]
        + [pl.BlockSpec((None, w.shape[1], tn), lambda j, i: (l, 0, j)) for w in w_outs],
        out_specs=pl.BlockSpec((tm, tn), lambda j, i: (i, j)),
        out_shape=jax.ShapeDtypeStruct((m, d), BF16),
        compiler_params=_cparams("parallel", "parallel"),
        name="merge",
    )(h, *branches, *([w_gate] * N_BRANCH), *([b_gate3] * N_BRANCH), *w_outs)


def _rope_tables(lay):
    t = lay.dec_seq
    half = ROT_AXIS // 2
    rows = t // GRID_W
    row = jnp.repeat(jnp.arange(rows, dtype=F32), GRID_W)
    col = jnp.tile(jnp.arange(GRID_W, dtype=F32), rows)
    inv = ROPE_BASE ** (-jnp.arange(0, ROT_AXIS, 2, dtype=F32) / ROT_AXIS)
    ang = jnp.concatenate([row[:, None] * inv, row[:, None] * inv, col[:, None] * inv, col[:, None] * inv], axis=1)
    lane = jnp.arange(HEAD_DIM)
    first_half = (lane % ROT_AXIS) < half
    cos = jnp.cos(ang)
    sin = jnp.sin(ang)
    sin_a = jnp.where(first_half, -sin, 0.0)
    sin_b = jnp.where(first_half, 0.0, sin)

    def full(tab, fill):
        return jnp.concatenate([jnp.full((lay.p_rows, HEAD_DIM), fill, F32)] + [tab] * lay.dec_batch, axis=0)

    return full(cos, 1.0), full(sin_a, 0.0), full(sin_b, 0.0)


def kernel(x_prompt, x_sample, cache_k, cache_v, state_hgrn_fwd, state_hgrn_bwd, c, c_ctx, ada_w, ada_b, norm1_g, norm2_g, w_in, q_norm_g, k_norm_g, attn_sink, w_attn_out, conf_dw_w, conf_dw_b, conf_ln_g, conf_ln_b, w_conf_out, sc_conv_w, w_sc_out, hg_lb, hg_norm_g, w_hg_out, w_gate, b_gate, w_o, ffn_w1, ffn_w3, ffn_w2):
    batch, seq, d = x_prompt.shape
    dec_batch, dec_seq, _ = x_sample.shape
    depth = ada_w.shape[0]
    past = cache_k.shape[2]
    lay = _Layout(batch, seq, dec_batch, dec_seq)
    assert seq % SEQ_TILE == 0 and dec_seq % SEQ_TILE == 0 and dec_seq % GRID_W == 0
    assert 1 + dec_batch <= MOD_ROWS

    wb = {name: w.astype(BF16) for name, w in dict(
        w_in=w_in, w_attn_out=w_attn_out, w_conf_out=w_conf_out, w_sc_out=w_sc_out, w_hg_out=w_hg_out,
        w_gate=w_gate, w_o=w_o, ffn_w1=ffn_w1, ffn_w3=ffn_w3, ffn_w2=ffn_w2).items()}

    cvec = jnp.concatenate([c_ctx[None], c, jnp.zeros((MOD_ROWS - 1 - dec_batch, d), F32)], axis=0)
    mods = _modulation(cvec, ada_w, ada_b)
    mods = mods.reshape(depth, MOD_ROWS, 6, d)[:, :1 + dec_batch]
    mods = jnp.pad(mods, ((0, 0), (0, 0), (0, MOD_ROWS - 6), (0, 0)))

    lb = jax.nn.softmax(hg_lb.astype(F32), axis=1)
    lb = jnp.maximum(jnp.cumsum(lb, axis=1) - lb[:, :1], 0.0)

    rope_tabs = _rope_tables(lay)
    ctx_k = cache_k.reshape(dec_batch, depth, past, KV_WIDTH)
    ctx_v = cache_v.reshape(dec_batch, depth, past, KV_WIDTH)
    zero_states = jnp.zeros((batch, HG_HEADS, HG_DV, HG_DK), F32)

    x = jnp.concatenate([x_prompt.reshape(lay.p_rows, d), x_sample.reshape(lay.s_rows, d)], axis=0)
    ks_out, vs_out, sf_out, sb_out = [], [], [], []
    for l in range(depth):
        z, h = _in_projection(lay, l, x, mods, norm1_g, wb["w_in"])
        qn, kn, kf = _qk_prep(lay, l, z, rope_tabs, q_norm_g, k_norm_g)
        attn = jnp.concatenate([
            _context_attention(lay, l, qn, kn, z, attn_sink[l]),
            _latent_attention(lay, l, qn, kn, z, ctx_k, ctx_v, attn_sink[l])], axis=0)
        u_b, u_c = _conv_mixers(lay, l, z, conf_dw_w, conf_dw_b, conf_ln_g, conf_ln_b, sc_conv_w)
        s0_f = jnp.concatenate([zero_states, jnp.swapaxes(state_hgrn_fwd[:, l], -1, -2)], axis=0)
        s0_b = jnp.concatenate([zero_states, jnp.swapaxes(state_hgrn_bwd[:, l], -1, -2)], axis=0)
        o_f, s_f = _hgrn_direction(lay, z, lb[0, l][None], s0_f, reverse=False)
        o_b, s_b = _hgrn_direction(lay, z, lb[1, l][None], s0_b, reverse=True)
        o_d = _hgrn_output(lay, l, o_f, o_b, z, hg_norm_g)
        merged = _merge(lay, l, h, (attn, u_b, u_c, o_d), wb["w_gate"], b_gate,
                        (wb["w_attn_out"], wb["w_conf_out"], wb["w_sc_out"], wb["w_hg_out"]))
        x = _residual_matmul(lay, l, merged, wb["w_o"], x, mods, MOD_GATE1, "out_projection")
        t = _ffn_up(lay, l, x, mods, norm2_g, wb["ffn_w1"], wb["ffn_w3"])
        x = _residual_matmul(lay, l, t, wb["ffn_w2"], x, mods, MOD_GATE2, "ffn_down")
        ks_out.append(kf[:lay.p_rows].reshape(batch, seq, N_KV_A, HEAD_DIM))
        vs_out.append(z[:lay.p_rows, A_WIDTH + KV_WIDTH:A_WIDTH + 2 * KV_WIDTH].reshape(batch, seq, N_KV_A, HEAD_DIM))
        sf_out.append(jnp.swapaxes(s_f[:batch], -1, -2))
        sb_out.append(jnp.swapaxes(s_b[:batch], -1, -2))
    return (x[:lay.p_rows].reshape(batch, seq, d), x[lay.p_rows:].reshape(dec_batch, dec_seq, d),
            jnp.stack(ks_out, axis=1), jnp.stack(vs_out, axis=1),
            jnp.stack(sf_out, axis=1), jnp.stack(sb_out, axis=1))
```

```python
import functools

import jax
import jax.numpy as jnp
from jax import lax
from jax.experimental import pallas as pl
from jax.experimental.pallas import tpu as pltpu

F32 = jnp.float32
BF16 = jnp.bfloat16

V7X_VMEM_BYTES = 64 * 1024 * 1024
VMEM_LIMIT_BYTES = V7X_VMEM_BYTES - 8 * 1024 * 1024
LANES = 128
SUBLANES = 8

GRID_W = 64
HEAD_DIM = 128
N_HEADS_A = 8
N_KV_A = 2
GROUP_A = N_HEADS_A // N_KV_A
A_WIDTH = N_HEADS_A * HEAD_DIM
KV_WIDTH = N_KV_A * HEAD_DIM
WINDOW = 128
BLOCK = 128
ROPE_BASE = 10000.0
ROT_AXIS = HEAD_DIM // 2
ATTN_SCALE = HEAD_DIM ** -0.5
NEG_MASK = -1e9
CONF_W = 512
CONF_K = 31
SC_W = 512
SC_K = 3
HG_HEADS = 4
HG_DK = 128
HG_DV = 128
HG_W = HG_HEADS * HG_DK
HG_CHUNK = 32
N_BRANCH = 4
EPS = 1e-6

COL_W = 512
COL_CONF_A, COL_CONF_G, COL_SC_B, COL_SC_C, COL_SC_H = 3, 4, 5, 6, 7
COL_HQ, COL_HF_FWD, COL_HF_BWD, COL_HI, COL_HG = 8, 9, 10, 11, 12
KV_COL_K, KV_COL_V = A_WIDTH // KV_WIDTH, A_WIDTH // KV_WIDTH + 1

SEQ_TILE = 256
HALO = 16
HG_SAFE_LOG_DECAY = -60.0

MOD_SHIFT1, MOD_SCALE1, MOD_GATE1, MOD_SHIFT2, MOD_SCALE2, MOD_GATE2 = range(6)
MOD_ROWS = 8


def _cparams(*semantics):
    return pltpu.CompilerParams(dimension_semantics=semantics, vmem_limit_bytes=VMEM_LIMIT_BYTES)


def _sigmoid(x):
    return 1.0 / (1.0 + jnp.exp(-x))


def _silu(x):
    return x * _sigmoid(x)


def _chunk_of(idx, size):
    assert size & (size - 1) == 0
    return jnp.right_shift(idx, size.bit_length() - 1)


def _dot(a, b):
    return jnp.dot(a, b, preferred_element_type=F32)


def _dot_nt(a, b):
    return lax.dot_general(a, b, (((1,), (1,)), ((), ())), preferred_element_type=F32)


class _Layout:
    def __init__(self, batch, seq, dec_batch, dec_seq):
        self.batch, self.seq, self.dec_batch, self.dec_seq = batch, seq, dec_batch, dec_seq
        self.p_rows = batch * seq
        self.s_rows = dec_batch * dec_seq
        self.rows = self.p_rows + self.s_rows
        self.n_seq = batch + dec_batch

    def row_tile(self, preferred):
        t = preferred
        while self.p_rows % t or self.dec_seq % t:
            t //= 2
        return t

    def mod_row(self, i, tile):
        p_tiles = self.p_rows // tile
        per = self.dec_seq // tile
        return jnp.where(i < p_tiles, 0, 1 + (i - p_tiles) // per)

    def seq_pos(self, n, tile):
        p_tiles = self.p_rows // tile
        tpp = self.seq // tile
        tps = self.dec_seq // tile
        is_p = n < p_tiles
        seq_idx = jnp.where(is_p, n // tpp, self.batch + (n - p_tiles) // tps)
        pos = jnp.where(is_p, n % tpp, (n - p_tiles) % tps)
        count = jnp.where(is_p, tpp, tps)
        return seq_idx, pos, count


def _mod_kernel(c_ref, w_ref, b_ref, o_ref):
    s = _silu(c_ref[...]).astype(BF16)
    o_ref[...] = _dot(s, w_ref[...].astype(BF16)) + b_ref[...]


def _modulation(cvec, ada_w, ada_b):
    depth, d, n = ada_w.shape
    tn = 1024
    return pl.pallas_call(
        _mod_kernel,
        grid=(depth, n // tn),
        in_specs=[
            pl.BlockSpec((MOD_ROWS, d), lambda l, j: (0, 0)),
            pl.BlockSpec((None, d, tn), lambda l, j: (l, 0, j)),
            pl.BlockSpec((None, 1, tn), lambda l, j: (l, 0, j)),
        ],
        out_specs=pl.BlockSpec((None, MOD_ROWS, tn), lambda l, j: (l, 0, j)),
        out_shape=jax.ShapeDtypeStruct((depth, MOD_ROWS, n), F32),
        compiler_params=_cparams("parallel", "parallel"),
        name="modulation",
    )(cvec, ada_w, ada_b.reshape(depth, 1, n))


def _modulated_norm(x_ref, g_ref, mod_ref, shift_row, scale_row, h_ref):
    rows = 16

    def body(c, carry):
        r = pl.multiple_of(c * rows, rows)
        x = x_ref[pl.ds(r, rows), :]
        y = x * lax.rsqrt(jnp.mean(x * x, axis=-1, keepdims=True) + EPS) * g_ref[...]
        h = y * (1.0 + mod_ref[scale_row:scale_row + 1, :]) + mod_ref[shift_row:shift_row + 1, :]
        h_ref[pl.ds(r, rows), :] = h.astype(BF16)
        return carry

    lax.fori_loop(0, x_ref.shape[0] // rows, body, 0, unroll=8)


def _inproj_kernel(x_ref, mod_ref, g_ref, w_ref, z_ref, h_ref):
    @pl.when(pl.program_id(1) == 0)
    def _():
        _modulated_norm(x_ref, g_ref, mod_ref, MOD_SHIFT1, MOD_SCALE1, h_ref)

    z_ref[...] = _dot(h_ref[...], w_ref[...])


def _in_projection(lay, l, x, mods, norm_g, w_in):
    m, d = x.shape
    n = w_in.shape[-1]
    tm = lay.row_tile(1024)
    tn = 512
    return pl.pallas_call(
        _inproj_kernel,
        grid=(m // tm, n // tn),
        in_specs=[
            pl.BlockSpec((tm, d), lambda i, j: (i, 0)),
            pl.BlockSpec((None, None, MOD_ROWS, d), lambda i, j: (l, lay.mod_row(i, tm), 0, 0)),
            pl.BlockSpec((None, 1, d), lambda i, j: (l, 0, 0)),
            pl.BlockSpec((None, d, tn), lambda i, j: (l, 0, j)),
        ],
        out_specs=[
            pl.BlockSpec((tm, tn), lambda i, j: (i, j)),
            pl.BlockSpec((tm, d), lambda i, j: (i, 0)),
        ],
        out_shape=[jax.ShapeDtypeStruct((m, n), F32), jax.ShapeDtypeStruct((m, d), BF16)],
        compiler_params=_cparams("parallel", "arbitrary"),
        name="in_projection",
    )(x, mods, norm_g.reshape(norm_g.shape[0], 1, d), w_in)


def _residual_matmul_kernel(a_ref, w_ref, x_ref, mod_ref, o_ref, *, gate_row):
    a = a_ref[...]
    chunk = 512
    for c0 in range(0, o_ref.shape[1], chunk):
        sl = slice(c0, c0 + chunk)
        o_ref[:, sl] = x_ref[:, sl] + mod_ref[gate_row:gate_row + 1, sl] * _dot(a, w_ref[:, sl])


def _residual_matmul(lay, l, a, w, x, mods, gate_row, name, tm, tn):
    m, k = a.shape
    d = w.shape[-1]
    tm = lay.row_tile(tm)
    return pl.pallas_call(
        functools.partial(_residual_matmul_kernel, gate_row=gate_row),
        grid=(m // tm, d // tn),
        in_specs=[
            pl.BlockSpec((tm, k), lambda i, j: (i, 0)),
            pl.BlockSpec((None, k, tn), lambda i, j: (l, 0, j)),
            pl.BlockSpec((tm, tn), lambda i, j: (i, j)),
            pl.BlockSpec((None, None, MOD_ROWS, tn), lambda i, j: (l, lay.mod_row(i, tm), 0, j)),
        ],
        out_specs=pl.BlockSpec((tm, tn), lambda i, j: (i, j)),
        out_shape=jax.ShapeDtypeStruct((m, d), F32),
        compiler_params=_cparams("parallel", "arbitrary"),
        name=name,
    )(a, w, x, mods)


def _ffn_up_kernel(x_ref, mod_ref, g_ref, w1_ref, w3_ref, o_ref, h_ref):
    @pl.when(pl.program_id(1) == 0)
    def _():
        _modulated_norm(x_ref, g_ref, mod_ref, MOD_SHIFT2, MOD_SCALE2, h_ref)

    h = h_ref[...]
    o_ref[...] = (_silu(_dot(h, w1_ref[...])) * _dot(h, w3_ref[...])).astype(BF16)


def _ffn_up(lay, l, x, mods, norm_g, w1, w3):
    m, d = x.shape
    n = w1.shape[-1]
    tm = lay.row_tile(1024)
    tn = 512
    return pl.pallas_call(
        _ffn_up_kernel,
        grid=(m // tm, n // tn),
        in_specs=[
            pl.BlockSpec((tm, d), lambda i, j: (i, 0)),
            pl.BlockSpec((None, None, MOD_ROWS, d), lambda i, j: (l, lay.mod_row(i, tm), 0, 0)),
            pl.BlockSpec((None, 1, d), lambda i, j: (l, 0, 0)),
            pl.BlockSpec((None, d, tn), lambda i, j: (l, 0, j)),
            pl.BlockSpec((None, d, tn), lambda i, j: (l, 0, j)),
        ],
        out_specs=pl.BlockSpec((tm, tn), lambda i, j: (i, j)),
        out_shape=jax.ShapeDtypeStruct((m, n), BF16),
        scratch_shapes=[pltpu.VMEM((tm, d), BF16)],
        compiler_params=_cparams("parallel", "arbitrary"),
        name="ffn_up",
    )(x, mods, norm_g.reshape(norm_g.shape[0], 1, d), w1, w3)


def _qk_prep_kernel(q_ref, k_ref, cos_ref, sa_ref, sb_ref, qg_ref, kg_ref, qo_ref, ko_ref, kf_ref):
    cos, sin_a, sin_b = cos_ref[...], sa_ref[...], sb_ref[...]

    def norm(x, g):
        return x * lax.rsqrt(jnp.mean(x * x, axis=-1, keepdims=True) + EPS) * g

    def rope(x):
        return (x * cos + pltpu.roll(x, HEAD_DIM - ROT_AXIS // 2, 1) * sin_a
                + pltpu.roll(x, ROT_AXIS // 2, 1) * sin_b)

    for h in range(N_HEADS_A):
        sl = slice(h * HEAD_DIM, (h + 1) * HEAD_DIM)
        qo_ref[:, sl] = (rope(norm(q_ref[:, sl], qg_ref[...])) * ATTN_SCALE).astype(BF16)
    for h in range(N_KV_A):
        sl = slice(h * HEAD_DIM, (h + 1) * HEAD_DIM)
        kn = norm(k_ref[:, sl], kg_ref[...])
        kf_ref[:, sl] = kn
        ko_ref[:, sl] = rope(kn).astype(BF16)


def _qk_prep(lay, l, z, rope_tabs, q_norm_g, k_norm_g):
    m = z.shape[0]
    tm = lay.row_tile(512)
    cos, sin_a, sin_b = rope_tabs
    tab_spec = pl.BlockSpec((tm, HEAD_DIM), lambda i: (i, 0))
    g_spec = pl.BlockSpec((None, 1, HEAD_DIM), lambda i: (l, 0, 0))
    return pl.pallas_call(
        _qk_prep_kernel,
        grid=(m // tm,),
        in_specs=[
            pl.BlockSpec((tm, A_WIDTH), lambda i: (i, 0)),
            pl.BlockSpec((tm, KV_WIDTH), lambda i: (i, KV_COL_K)),
            tab_spec, tab_spec, tab_spec, g_spec, g_spec,
        ],
        out_specs=[
            pl.BlockSpec((tm, A_WIDTH), lambda i: (i, 0)),
            pl.BlockSpec((tm, KV_WIDTH), lambda i: (i, 0)),
            pl.BlockSpec((tm, KV_WIDTH), lambda i: (i, 0)),
        ],
        out_shape=[
            jax.ShapeDtypeStruct((m, A_WIDTH), BF16),
            jax.ShapeDtypeStruct((m, KV_WIDTH), BF16),
            jax.ShapeDtypeStruct((m, KV_WIDTH), F32),
        ],
        compiler_params=_cparams("parallel"),
        name="qk_prep",
    )(z, z, cos, sin_a, sin_b, q_norm_g.reshape(-1, 1, HEAD_DIM), k_norm_g.reshape(-1, 1, HEAD_DIM))


def _stack_heads(q_ref, kv):
    return jnp.concatenate(
        [q_ref[:, (kv * GROUP_A + g) * HEAD_DIM:(kv * GROUP_A + g + 1) * HEAD_DIM] for g in range(GROUP_A)], axis=0)


def _fold_lane_tiles(x, op):
    out = x[:, :LANES]
    for c0 in range(LANES, x.shape[1], LANES):
        out = op(out, x[:, c0:c0 + LANES])
    return out


def _sink_column(sink_ref, kv, rows):
    return jnp.concatenate(
        [jnp.full((rows, 1), sink_ref[kv * GROUP_A + g], F32) for g in range(GROUP_A)], axis=0)


def _context_attention_kernel(sink_ref, q_ref, k_ref, v_ref, o_ref):
    rows = q_ref.shape[0]
    for kv in range(N_KV_A):
        sl = slice(kv * HEAD_DIM, (kv + 1) * HEAD_DIM)
        q = _stack_heads(q_ref, kv)
        s = _dot_nt(q, k_ref[:, sl])
        sink = _sink_column(sink_ref, kv, rows)
        mx = jnp.maximum(jnp.max(s, axis=-1, keepdims=True), sink)
        p = jnp.exp(s - mx)
        den = jnp.sum(p, axis=-1, keepdims=True) + jnp.exp(sink - mx)
        o = _dot(p.astype(BF16), v_ref[:, sl].astype(BF16)) * (1.0 / den)
        for g in range(GROUP_A):
            h = kv * GROUP_A + g
            o_ref[:, h * HEAD_DIM:(h + 1) * HEAD_DIM] = o[g * rows:(g + 1) * rows].astype(BF16)


def _context_attention(lay, l, qn, kn, z, sink):
    seq = lay.seq
    return pl.pallas_call(
        _context_attention_kernel,
        grid=(lay.batch,),
        in_specs=[
            pl.BlockSpec(memory_space=pltpu.SMEM),
            pl.BlockSpec((seq, A_WIDTH), lambda b: (b, 0)),
            pl.BlockSpec((seq, KV_WIDTH), lambda b: (b, 0)),
            pl.BlockSpec((seq, KV_WIDTH), lambda b: (b, KV_COL_V)),
        ],
        out_specs=pl.BlockSpec((seq, A_WIDTH), lambda b: (b, 0)),
        out_shape=jax.ShapeDtypeStruct((lay.p_rows, A_WIDTH), BF16),
        compiler_params=_cparams("parallel"),
        name="context_attention",
    )(sink, qn, kn, z)


def _latent_attention_kernel(sink_ref, q_ref, kp_ref, kc_ref, kn_ref, vp_ref, vc_ref, vn_ref, ck_ref, cv_ref,
                             o_ref, *, seq_len):
    assert WINDOW >= BLOCK - 1
    j = pl.program_id(1)
    n_blocks = seq_len // BLOCK
    r = lax.broadcasted_iota(jnp.int32, (BLOCK, BLOCK), 0)
    c = lax.broadcasted_iota(jnp.int32, (BLOCK, BLOCK), 1)
    valid_prev = jnp.concatenate([(r - (c - BLOCK) <= WINDOW) & (j >= 1)] * GROUP_A, axis=0)
    valid_next = jnp.concatenate([((c + BLOCK) - r <= WINDOW) & (j <= n_blocks - 2)] * GROUP_A, axis=0)
    for kv in range(N_KV_A):
        sl = slice(kv * HEAD_DIM, (kv + 1) * HEAD_DIM)
        q = _stack_heads(q_ref, kv)
        s_p = jnp.where(valid_prev, _dot_nt(q, kp_ref[:, sl]), NEG_MASK)
        s_o = _dot_nt(q, kc_ref[:, sl])
        s_n = jnp.where(valid_next, _dot_nt(q, kn_ref[:, sl]), NEG_MASK)
        s_c = _dot_nt(q, ck_ref[:, sl].astype(BF16))
        sink = _sink_column(sink_ref, kv, BLOCK)
        mx = jnp.maximum(jnp.maximum(s_p, s_o), jnp.maximum(s_n, _fold_lane_tiles(s_c, jnp.maximum)))
        mx = jnp.maximum(jnp.max(mx, axis=-1, keepdims=True), sink)
        psum = None
        o = None
        for s, v_ref in ((s_p, vp_ref), (s_o, vc_ref), (s_n, vn_ref), (s_c, cv_ref)):
            p = jnp.exp(s - mx)
            pf = _fold_lane_tiles(p, jnp.add)
            psum = pf if psum is None else psum + pf
            term = _dot(p.astype(BF16), v_ref[:, sl].astype(BF16))
            o = term if o is None else o + term
        den = jnp.sum(psum, axis=-1, keepdims=True) + jnp.exp(sink - mx)
        o = o * (1.0 / den)
        for g in range(GROUP_A):
            h = kv * GROUP_A + g
            o_ref[:, h * HEAD_DIM:(h + 1) * HEAD_DIM] = o[g * BLOCK:(g + 1) * BLOCK].astype(BF16)


def _latent_attention(lay, l, qn, kn, z, ctx_k, ctx_v, sink):
    nb = lay.dec_seq // BLOCK
    base = lay.p_rows // BLOCK
    past = ctx_k.shape[2]

    def blk(shift, col):
        return lambda b, j: (base + b * nb + jnp.clip(j + shift, 0, nb - 1), col)

    k_spec = [pl.BlockSpec((BLOCK, KV_WIDTH), blk(s, 0)) for s in (-1, 0, 1)]
    v_spec = [pl.BlockSpec((BLOCK, KV_WIDTH), blk(s, KV_COL_V)) for s in (-1, 0, 1)]
    ctx_spec = pl.BlockSpec((None, None, past, KV_WIDTH), lambda b, j: (b, l, 0, 0))
    return pl.pallas_call(
        functools.partial(_latent_attention_kernel, seq_len=lay.dec_seq),
        grid=(lay.dec_batch, nb),
        in_specs=[pl.BlockSpec(memory_space=pltpu.SMEM), pl.BlockSpec((BLOCK, A_WIDTH), blk(0, 0))]
        + k_spec + v_spec + [ctx_spec, ctx_spec],
        out_specs=pl.BlockSpec((BLOCK, A_WIDTH), lambda b, j: (b * nb + j, 0)),
        out_shape=jax.ShapeDtypeStruct((lay.s_rows, A_WIDTH), BF16),
        compiler_params=_cparams("parallel", "parallel"),
        name="latent_attention",
    )(sink, qn, kn, kn, kn, z, z, z, ctx_k, ctx_v)


def _conv_kernel(a_c, a_p, a_n, g_c, g_p, g_n, b_c, c_c, c_p, c_n, h_c, h_p, h_n,
                 dw_ref, db_ref, lg_ref, lb_ref, sw_ref, ub_ref, uc_ref, xb_s, xc_s, *, lay):
    _, pos, count = lay.seq_pos(pl.program_id(0), SEQ_TILE)
    has_prev = pos > 0
    has_next = pos < count - 1
    rb = 32
    xb_s[0, 0:HALO, :] = jnp.where(has_prev, a_p[...] * _sigmoid(g_p[...]), 0.0)
    xb_s[0, HALO + SEQ_TILE:, :] = jnp.where(has_next, a_n[...] * _sigmoid(g_n[...]), 0.0)
    xc_s[0:HALO, :] = jnp.where(has_prev, c_p[...] * h_p[...], 0.0)
    xc_s[HALO + SEQ_TILE:, :] = jnp.where(has_next, c_n[...] * h_n[...], 0.0)
    for r0 in range(0, SEQ_TILE, rb):
        rows = slice(r0, r0 + rb)
        xb_s[0, HALO + r0:HALO + r0 + rb, :] = a_c[rows, :] * _sigmoid(g_c[rows, :])
        xc_s[HALO + r0:HALO + r0 + rb, :] = c_c[rows, :] * h_c[rows, :]
    shifted_rows = SEQ_TILE + 2 * HALO - SUBLANES
    chunk = 40
    for s in range(1, SUBLANES):
        for r0 in range(0, shifted_rows, chunk):
            xb_s[s, r0:r0 + chunk, :] = xb_s[0, r0 + s:r0 + s + chunk, :]
    for r0 in range(0, SEQ_TILE, rb):
        acc = jnp.zeros((rb // SUBLANES, SUBLANES, CONF_W), F32)
        for k in range(CONF_K):
            off = HALO - CONF_K // 2 + k
            row = r0 + off - off % SUBLANES
            x = xb_s[off % SUBLANES, row:row + rb, :]
            acc = acc + dw_ref[k] * x.reshape(rb // SUBLANES, SUBLANES, CONF_W)
        u = acc.reshape(rb, CONF_W) + db_ref[...]
        uc = u - jnp.mean(u, axis=-1, keepdims=True)
        var = jnp.mean(uc * uc, axis=-1, keepdims=True)
        u = uc * lax.rsqrt(var + EPS) * lg_ref[...] + lb_ref[...]
        ub_ref[r0:r0 + rb, :] = _silu(u).astype(BF16)
        acc = jnp.zeros((rb // SUBLANES, SUBLANES, SC_W), F32)
        for k in range(SC_K):
            off = HALO - SC_K // 2 + r0 + k
            acc = acc + sw_ref[k] * xc_s[off:off + rb, :].reshape(rb // SUBLANES, SUBLANES, SC_W)
        uc_ref[r0:r0 + rb, :] = (b_c[r0:r0 + rb, :] * acc.reshape(rb, SC_W)).astype(BF16)


def _conv_mixers(lay, l, z, conf_dw_w, conf_dw_b, conf_ln_g, conf_ln_b, sc_conv_w):
    m = z.shape[0]
    n_tiles = m // SEQ_TILE
    per = SEQ_TILE // HALO
    n_halo = m // HALO

    def cur(col):
        return pl.BlockSpec((SEQ_TILE, COL_W), lambda i: (i, col))

    def prev(col):
        return pl.BlockSpec((HALO, COL_W), lambda i: (jnp.maximum(i * per - 1, 0), col))

    def nxt(col):
        return pl.BlockSpec((HALO, COL_W), lambda i: (jnp.minimum((i + 1) * per, n_halo - 1), col))

    def trio(col):
        return [cur(col), prev(col), nxt(col)]

    def par(rows):
        return pl.BlockSpec((None, rows, COL_W), lambda i: (l, 0, 0))

    def taps(n):
        return pl.BlockSpec((None, n, SUBLANES, COL_W), lambda i: (l, 0, 0, 0))

    depth = conf_dw_w.shape[0]
    dw = jnp.broadcast_to(conf_dw_w[:, :, None, :], (depth, CONF_K, SUBLANES, CONF_W))
    sw = jnp.broadcast_to(sc_conv_w[:, :, None, :], (depth, SC_K, SUBLANES, SC_W))
    return pl.pallas_call(
        functools.partial(_conv_kernel, lay=lay),
        grid=(n_tiles,),
        in_specs=trio(COL_CONF_A) + trio(COL_CONF_G) + [cur(COL_SC_B)] + trio(COL_SC_C) + trio(COL_SC_H)
        + [taps(CONF_K), par(1), par(1), par(1), taps(SC_K)],
        out_specs=[pl.BlockSpec((SEQ_TILE, CONF_W), lambda i: (i, 0)),
                   pl.BlockSpec((SEQ_TILE, SC_W), lambda i: (i, 0))],
        out_shape=[jax.ShapeDtypeStruct((m, CONF_W), BF16), jax.ShapeDtypeStruct((m, SC_W), BF16)],
        scratch_shapes=[pltpu.VMEM((SUBLANES, SEQ_TILE + 2 * HALO, CONF_W), F32),
                        pltpu.VMEM((SEQ_TILE + 2 * HALO, SC_W), F32)],
        compiler_params=_cparams("parallel"),
        name="conv_mixers",
    )(*([z] * 13), dw, conf_dw_b.reshape(depth, 1, CONF_W), conf_ln_g.reshape(depth, 1, CONF_W),
      conf_ln_b.reshape(depth, 1, CONF_W), sw)


def _hgrn_gates(hq_ref, hf_ref, lb_ref, sl, tri_b):
    zq = hq_ref[:, sl]
    q = _silu(zq) * (HG_DK ** -0.5)
    zf = hf_ref[:, sl]
    lb = lb_ref[:, sl]
    e = jnp.exp(-jnp.abs(zf))
    r = 1.0 / (1.0 + e)
    nonneg = zf >= 0
    sig_pos = jnp.where(nonneg, r, e * r)
    sig_neg = jnp.where(nonneg, e * r, r)
    logf = jnp.log(lb + (1.0 - lb) * sig_pos)
    kk = (1.0 - lb) * sig_neg
    g1 = logf.astype(BF16)
    r1 = logf - g1.astype(F32)
    g2 = r1.astype(BF16)
    g3 = (r1 - g2.astype(F32)).astype(BF16)
    b = _dot(tri_b, g1) + _dot(tri_b, g2) + _dot(tri_b, g3)
    return q, kk, b


def _hgrn_exact_scores(a_ref, q_s, k_s, b_s, reverse):
    ts = a_ref.shape[0]
    a_ref[...] = jnp.zeros_like(a_ref)

    def body(s, carry):
        c0 = pl.multiple_of((s // HG_CHUNK) * HG_CHUNK, HG_CHUNK)
        w = jnp.exp(jnp.minimum(b_s[pl.ds(c0, HG_CHUNK), :] - b_s[pl.ds(s, 1), :], 0.0))
        prod = q_s[pl.ds(c0, HG_CHUNK), :] * k_s[pl.ds(s, 1), :] * w
        trow = c0 + lax.broadcasted_iota(jnp.int32, (HG_CHUNK, 1), 0)
        ok = (trow <= s) if reverse else (trow >= s)
        onehot = (lax.broadcasted_iota(jnp.int32, (1, ts), 1) == s).astype(F32)
        colv = jnp.sum(prod, axis=-1, keepdims=True)
        a_ref[pl.ds(c0, HG_CHUNK), :] += jnp.where(ok, colv, 0.0) * onehot
        return carry

    lax.fori_loop(0, ts, body, 0)


def _hgrn_kernel(hq_ref, hf_ref, hi_ref, lb_ref, s0_ref, o_ref, sf_ref, st_ref, oi_s, a_ref, q_s, k_s, b_s,
                 *, lay, reverse):
    ts = SEQ_TILE
    n_chunks = ts // HG_CHUNK
    _, pos, count = lay.seq_pos(pl.program_id(0), ts)

    @pl.when(pos == 0)
    def _():
        st_ref[...] = s0_ref[...]

    row = lax.broadcasted_iota(jnp.int32, (ts, ts), 0)
    col = lax.broadcasted_iota(jnp.int32, (ts, ts), 1)
    same = _chunk_of(row, HG_CHUNK) == _chunk_of(col, HG_CHUNK)
    tri = same & ((col >= row) if reverse else (col <= row))
    tri_b = tri.astype(F32).astype(BF16)
    zero_b = jnp.zeros((), BF16)
    blk_row = _chunk_of(lax.broadcasted_iota(jnp.int32, (ts, n_chunks * HG_DK), 0), HG_CHUNK)
    blk_col = _chunk_of(lax.broadcasted_iota(jnp.int32, (ts, n_chunks * HG_DK), 1), HG_DK)
    blk = blk_row == blk_col
    order = range(n_chunks - 1, -1, -1) if reverse else range(n_chunks)
    last = 0 if reverse else HG_CHUNK - 1

    b_min = None
    for hd in range(HG_HEADS):
        sl = slice(hd * HG_DK, (hd + 1) * HG_DK)
        q, kk, b = _hgrn_gates(hq_ref, hf_ref, lb_ref, sl, tri_b)
        head_min = jnp.min(b)
        b_min = head_min if b_min is None else jnp.minimum(b_min, head_min)
        b3 = b.reshape(n_chunks, HG_CHUNK, HG_DK)
        total3 = b3[:, last:last + 1, :]
        qt = (q * jnp.exp(b)).astype(BF16)
        kend = (kk * jnp.exp(jnp.broadcast_to(total3, b3.shape).reshape(ts, HG_DK) - b)).astype(BF16)
        decay = jnp.exp(total3.reshape(n_chunks, HG_DK))
        v_h = hi_ref[:, sl]
        kinv = (kk * jnp.exp(-b)).astype(BF16)
        scores = jnp.where(tri, _dot_nt(qt, kinv), 0.0).astype(BF16)
        o_intra = _dot(scores, v_h.astype(BF16))
        k_blocks = jnp.where(blk, jnp.concatenate([kend] * n_chunks, axis=1), zero_b)
        incr = _dot(v_h.T.astype(BF16), k_blocks)
        state = st_ref[hd]
        starts = [None] * n_chunks
        for c in order:
            starts[c] = state
            state = state * decay[c:c + 1, :] + incr[:, c * HG_DK:(c + 1) * HG_DK]
        st_ref[hd] = state
        q_blocks = jnp.where(blk, jnp.concatenate([qt] * n_chunks, axis=1), zero_b)
        o_inter = _dot_nt(q_blocks, jnp.concatenate(starts, axis=1).astype(BF16))
        oi_s[:, sl] = o_inter
        o_ref[:, sl] = o_intra + o_inter

    @pl.when(b_min <= HG_SAFE_LOG_DECAY)
    def _():
        for hd in range(HG_HEADS):
            sl = slice(hd * HG_DK, (hd + 1) * HG_DK)
            q, kk, b = _hgrn_gates(hq_ref, hf_ref, lb_ref, sl, tri_b)
            q_s[...] = q
            k_s[...] = kk
            b_s[...] = b
            _hgrn_exact_scores(a_ref, q_s, k_s, b_s, reverse)
            o_ref[:, sl] = _dot(a_ref[...].astype(BF16), hi_ref[:, sl].astype(BF16)) + oi_s[:, sl]

    @pl.when(pos == count - 1)
    def _():
        sf_ref[...] = st_ref[...]


def _hgrn_direction(lay, z, lb, s0_t, reverse):
    m = z.shape[0]
    n_tiles = m // SEQ_TILE

    def tile_of(n):
        seq_idx, pos, count = lay.seq_pos(n, SEQ_TILE)
        return (n - pos + (count - 1 - pos)) if reverse else n

    def zcol(col):
        return pl.BlockSpec((SEQ_TILE, COL_W), lambda n: (tile_of(n), col))

    state_spec = pl.BlockSpec((None, HG_HEADS, HG_DV, HG_DK), lambda n: (lay.seq_pos(n, SEQ_TILE)[0], 0, 0, 0))
    return pl.pallas_call(
        functools.partial(_hgrn_kernel, lay=lay, reverse=reverse),
        grid=(n_tiles,),
        in_specs=[zcol(COL_HQ), zcol(COL_HF_BWD if reverse else COL_HF_FWD), zcol(COL_HI),
                  pl.BlockSpec((1, HG_W), lambda n: (0, 0)), state_spec],
        out_specs=[pl.BlockSpec((SEQ_TILE, HG_W), lambda n: (tile_of(n), 0)), state_spec],
        out_shape=[jax.ShapeDtypeStruct((m, HG_W), F32),
                   jax.ShapeDtypeStruct((lay.n_seq, HG_HEADS, HG_DV, HG_DK), F32)],
        scratch_shapes=[pltpu.VMEM((HG_HEADS, HG_DV, HG_DK), F32),
                        pltpu.VMEM((SEQ_TILE, HG_W), F32),
                        pltpu.VMEM((SEQ_TILE, SEQ_TILE), F32),
                        pltpu.VMEM((SEQ_TILE, HG_DK), F32),
                        pltpu.VMEM((SEQ_TILE, HG_DK), F32),
                        pltpu.VMEM((SEQ_TILE, HG_DK), F32)],
        compiler_params=_cparams("arbitrary"),
        name="hgrn_bwd" if reverse else "hgrn_fwd",
    )(z, z, z, lb, s0_t)


def _hgrn_out_kernel(of_ref, ob_ref, hg_ref, g_ref, o_ref):
    for hd in range(HG_HEADS):
        sl = slice(hd * HG_DV, (hd + 1) * HG_DV)
        o = of_ref[:, sl] + ob_ref[:, sl]
        o = o * lax.rsqrt(jnp.mean(o * o, axis=-1, keepdims=True) + EPS) * g_ref[...]
        o_ref[:, sl] = (o * _silu(hg_ref[:, sl])).astype(BF16)


def _hgrn_output(lay, l, o_f, o_b, z, hg_norm_g):
    m = z.shape[0]
    tm = lay.row_tile(512)
    row_spec = pl.BlockSpec((tm, HG_W), lambda i: (i, 0))
    return pl.pallas_call(
        _hgrn_out_kernel,
        grid=(m // tm,),
        in_specs=[row_spec, row_spec, pl.BlockSpec((tm, COL_W), lambda i: (i, COL_HG)),
                  pl.BlockSpec((None, 1, HG_DV), lambda i: (l, 0, 0))],
        out_specs=row_spec,
        out_shape=jax.ShapeDtypeStruct((m, HG_W), BF16),
        compiler_params=_cparams("parallel"),
        name="hgrn_output",
    )(o_f, o_b, z, hg_norm_g.reshape(-1, 1, HG_DV))


def _merge_kernel(h_ref, a_ref, b_ref, c_ref, d_ref, wg0, wg1, wg2, wg3, bg0, bg1, bg2, bg3,
                  wa_ref, wb_ref, wc_ref, wd_ref, o_ref):
    h = h_ref[...]
    acc = None
    for br_ref, w_ref, wg_ref, bg_ref in ((a_ref, wa_ref, wg0, bg0), (b_ref, wb_ref, wg1, bg1),
                                          (c_ref, wc_ref, wg2, bg2), (d_ref, wd_ref, wg3, bg3)):
        gate = _sigmoid(_dot(h, wg_ref[...]) + bg_ref[...])
        term = gate * _dot(br_ref[...], w_ref[...])
        acc = term if acc is None else acc + term
    o_ref[...] = acc.astype(BF16)


def _merge(lay, l, h, branches, w_gate, b_gate, w_outs):
    m, d = h.shape
    tm = lay.row_tile(512)
    tn = 512
    nj = d // tn
    depth = w_gate.shape[0]

    def gate_w(n):
        return pl.BlockSpec((None, d, tn), lambda j, i: (l, 0, n * nj + j))

    def gate_b(n):
        return pl.BlockSpec((None, 1, tn), lambda j, i: (l, 0, n * nj + j))

    b_gate3 = b_gate.reshape(depth, 1, N_BRANCH * d)
    return pl.pallas_call(
        _merge_kernel,
        grid=(nj, m // tm),
        in_specs=[pl.BlockSpec((tm, d), lambda j, i: (i, 0))]
        + [pl.BlockSpec((tm, br.shape[1]), lambda j, i: (i, 0)) for br in branches]
        + [gate_w(n) for n in range(N_BRANCH)] + [gate_b(n) for n in range(N_BRANCH)]
        + [pl.BlockSpec((None, w.shape[1], tn), lambda j, i: (l, 0, j)) for w in w_outs],
        out_specs=pl.BlockSpec((tm, tn), lambda j, i: (i, j)),
        out_shape=jax.ShapeDtypeStruct((m, d), BF16),
        compiler_params=_cparams("parallel", "parallel"),
        name="merge",
    )(h, *branches, *([w_gate] * N_BRANCH), *([b_gate3] * N_BRANCH), *w_outs)


def _rope_tables(lay):
    t = lay.dec_seq
    half = ROT_AXIS // 2
    rows = t // GRID_W
    row = jnp.repeat(jnp.arange(rows, dtype=F32), GRID_W)
    col = jnp.tile(jnp.arange(GRID_W, dtype=F32), rows)
    inv = ROPE_BASE ** (-jnp.arange(0, ROT_AXIS, 2, dtype=F32) / ROT_AXIS)
    ang = jnp.concatenate([row[:, None] * inv, row[:, None] * inv, col[:, None] * inv, col[:, None] * inv], axis=1)
    lane = jnp.arange(HEAD_DIM)
    first_half = (lane % ROT_AXIS) < half
    cos = jnp.cos(ang)
    sin = jnp.sin(ang)
    sin_a = jnp.where(first_half, -sin, 0.0)
    sin_b = jnp.where(first_half, 0.0, sin)

    def full(tab, fill):
        return jnp.concatenate([jnp.full((lay.p_rows, HEAD_DIM), fill, F32)] + [tab] * lay.dec_batch, axis=0)

    return full(cos, 1.0), full(sin_a, 0.0), full(sin_b, 0.0)


def kernel(x_prompt, x_sample, cache_k, cache_v, state_hgrn_fwd, state_hgrn_bwd, c, c_ctx, ada_w, ada_b, norm1_g, norm2_g, w_in, q_norm_g, k_norm_g, attn_sink, w_attn_out, conf_dw_w, conf_dw_b, conf_ln_g, conf_ln_b, w_conf_out, sc_conv_w, w_sc_out, hg_lb, hg_norm_g, w_hg_out, w_gate, b_gate, w_o, ffn_w1, ffn_w3, ffn_w2):
    batch, seq, d = x_prompt.shape
    dec_batch, dec_seq, _ = x_sample.shape
    depth = ada_w.shape[0]
    past = cache_k.shape[2]
    lay = _Layout(batch, seq, dec_batch, dec_seq)
    assert seq % SEQ_TILE == 0 and dec_seq % SEQ_TILE == 0 and dec_seq % GRID_W == 0
    assert 1 + dec_batch <= MOD_ROWS

    wb = {name: w.astype(BF16) for name, w in dict(
        w_in=w_in, w_attn_out=w_attn_out, w_conf_out=w_conf_out, w_sc_out=w_sc_out, w_hg_out=w_hg_out,
        w_gate=w_gate, w_o=w_o, ffn_w1=ffn_w1, ffn_w3=ffn_w3, ffn_w2=ffn_w2).items()}

    cvec = jnp.concatenate([c_ctx[None], c, jnp.zeros((MOD_ROWS - 1 - dec_batch, d), F32)], axis=0)
    mods = _modulation(cvec, ada_w, ada_b)
    mods = mods.reshape(depth, MOD_ROWS, 6, d)[:, :1 + dec_batch]
    mods = jnp.pad(mods, ((0, 0), (0, 0), (0, MOD_ROWS - 6), (0, 0)))

    lb = jax.nn.softmax(hg_lb.astype(F32), axis=1)
    lb = jnp.maximum(jnp.cumsum(lb, axis=1) - lb[:, :1], 0.0)

    rope_tabs = _rope_tables(lay)
    ctx_k = cache_k.reshape(dec_batch, depth, past, KV_WIDTH)
    ctx_v = cache_v.reshape(dec_batch, depth, past, KV_WIDTH)
    zero_states = jnp.zeros((batch, HG_HEADS, HG_DV, HG_DK), F32)

    x = jnp.concatenate([x_prompt.reshape(lay.p_rows, d), x_sample.reshape(lay.s_rows, d)], axis=0)
    ks_out, vs_out, sf_out, sb_out = [], [], [], []
    for l in range(depth):
        z, h = _in_projection(lay, l, x, mods, norm1_g, wb["w_in"])
        qn, kn, kf = _qk_prep(lay, l, z, rope_tabs, q_norm_g, k_norm_g)
        attn = jnp.concatenate([
            _context_attention(lay, l, qn, kn, z, attn_sink[l]),
            _latent_attention(lay, l, qn, kn, z, ctx_k, ctx_v, attn_sink[l])], axis=0)
        u_b, u_c = _conv_mixers(lay, l, z, conf_dw_w, conf_dw_b, conf_ln_g, conf_ln_b, sc_conv_w)
        s0_f = jnp.concatenate([zero_states, jnp.swapaxes(state_hgrn_fwd[:, l], -1, -2)], axis=0)
        s0_b = jnp.concatenate([zero_states, jnp.swapaxes(state_hgrn_bwd[:, l], -1, -2)], axis=0)
        o_f, s_f = _hgrn_direction(lay, z, lb[0, l][None], s0_f, reverse=False)
        o_b, s_b = _hgrn_direction(lay, z, lb[1, l][None], s0_b, reverse=True)
        o_d = _hgrn_output(lay, l, o_f, o_b, z, hg_norm_g)
        merged = _merge(lay, l, h, (attn, u_b, u_c, o_d), wb["w_gate"], b_gate,
                        (wb["w_attn_out"], wb["w_conf_out"], wb["w_sc_out"], wb["w_hg_out"]))
        x = _residual_matmul(lay, l, merged, wb["w_o"], x, mods, MOD_GATE1, "out_projection", tm=512, tn=d)
        t = _ffn_up(lay, l, x, mods, norm2_g, wb["ffn_w1"], wb["ffn_w3"])
        x = _residual_matmul(lay, l, t, wb["ffn_w2"], x, mods, MOD_GATE2, "ffn_down", tm=1024, tn=512)
        ks_out.append(kf[:lay.p_rows].reshape(batch, seq, N_KV_A, HEAD_DIM))
        vs_out.append(z[:lay.p_rows, A_WIDTH + KV_WIDTH:A_WIDTH + 2 * KV_WIDTH].reshape(batch, seq, N_KV_A, HEAD_DIM))
        sf_out.append(jnp.swapaxes(s_f[:batch], -1, -2))
        sb_out.append(jnp.swapaxes(s_b[:batch], -1, -2))
    return (x[:lay.p_rows].reshape(batch, seq, d), x[lay.p_rows:].reshape(dec_batch, dec_seq, d),
            jnp.stack(ks_out, axis=1), jnp.stack(vs_out, axis=1),
            jnp.stack(sf_out, axis=1), jnp.stack(sb_out, axis=1))
```

```python
import functools

import jax
import jax.numpy as jnp
from jax import lax
from jax.experimental import pallas as pl
from jax.experimental.pallas import tpu as pltpu

F32 = jnp.float32
BF16 = jnp.bfloat16

V7X_VMEM_BYTES = 64 * 1024 * 1024
VMEM_LIMIT_BYTES = V7X_VMEM_BYTES - 8 * 1024 * 1024
LANES = 128
SUBLANES = 8
MXU_WIDTH = 256

GRID_W = 64
HEAD_DIM = 128
N_HEADS_A = 8
N_KV_A = 2
GROUP_A = N_HEADS_A // N_KV_A
A_WIDTH = N_HEADS_A * HEAD_DIM
KV_WIDTH = N_KV_A * HEAD_DIM
WINDOW = 128
BLOCK = 128
ROPE_BASE = 10000.0
ROT_AXIS = HEAD_DIM // 2
ATTN_SCALE = HEAD_DIM ** -0.5
NEG_MASK = -1e9
CONF_W = 512
CONF_K = 31
SC_W = 512
SC_K = 3
HG_HEADS = 4
HG_DK = 128
HG_DV = 128
HG_W = HG_HEADS * HG_DK
HG_CHUNK = 32
N_BRANCH = 4
EPS = 1e-6

COL_W = 512
COL_CONF_A, COL_CONF_G, COL_SC_B, COL_SC_C, COL_SC_H = 3, 4, 5, 6, 7
COL_HQ, COL_HF_FWD, COL_HF_BWD, COL_HI, COL_HG = 8, 9, 10, 11, 12
KV_COL_K, KV_COL_V = A_WIDTH // KV_WIDTH, A_WIDTH // KV_WIDTH + 1

SEQ_TILE = 256
HALO = 16
HG_SAFE_LOG_DECAY = -60.0

MOD_SHIFT1, MOD_SCALE1, MOD_GATE1, MOD_SHIFT2, MOD_SCALE2, MOD_GATE2 = range(6)
MOD_ROWS = 8


def _cparams(*semantics):
    return pltpu.CompilerParams(dimension_semantics=semantics, vmem_limit_bytes=VMEM_LIMIT_BYTES)


def _sigmoid(x):
    return 1.0 / (1.0 + jnp.exp(-x))


def _silu(x):
    return x * _sigmoid(x)


def _chunk_of(idx, size):
    assert size & (size - 1) == 0
    return jnp.right_shift(idx, size.bit_length() - 1)


def _dot(a, b):
    return jnp.dot(a, b, preferred_element_type=F32)


def _dot_nt(a, b):
    return lax.dot_general(a, b, (((1,), (1,)), ((), ())), preferred_element_type=F32)


class _Layout:
    def __init__(self, batch, seq, dec_batch, dec_seq):
        self.batch, self.seq, self.dec_batch, self.dec_seq = batch, seq, dec_batch, dec_seq
        self.p_rows = batch * seq
        self.s_rows = dec_batch * dec_seq
        self.rows = self.p_rows + self.s_rows
        self.n_seq = batch + dec_batch

    def row_tile(self, preferred):
        t = preferred
        while self.p_rows % t or self.dec_seq % t:
            t //= 2
        return t

    def mod_row(self, i, tile):
        p_tiles = self.p_rows // tile
        per = self.dec_seq // tile
        return jnp.where(i < p_tiles, 0, 1 + (i - p_tiles) // per)

    def seq_pos(self, n, tile):
        p_tiles = self.p_rows // tile
        tpp = self.seq // tile
        tps = self.dec_seq // tile
        is_p = n < p_tiles
        seq_idx = jnp.where(is_p, n // tpp, self.batch + (n - p_tiles) // tps)
        pos = jnp.where(is_p, n % tpp, (n - p_tiles) % tps)
        count = jnp.where(is_p, tpp, tps)
        return seq_idx, pos, count


def _mod_kernel(c_ref, w_ref, b_ref, o_ref):
    s = _silu(c_ref[...]).astype(BF16)
    o_ref[...] = _dot(s, w_ref[...].astype(BF16)) + b_ref[...]


def _modulation(cvec, ada_w, ada_b):
    depth, d, n = ada_w.shape
    tn = 1024
    return pl.pallas_call(
        _mod_kernel,
        grid=(depth, n // tn),
        in_specs=[
            pl.BlockSpec((MOD_ROWS, d), lambda l, j: (0, 0)),
            pl.BlockSpec((None, d, tn), lambda l, j: (l, 0, j)),
            pl.BlockSpec((None, 1, tn), lambda l, j: (l, 0, j)),
        ],
        out_specs=pl.BlockSpec((None, MOD_ROWS, tn), lambda l, j: (l, 0, j)),
        out_shape=jax.ShapeDtypeStruct((depth, MOD_ROWS, n), F32),
        compiler_params=_cparams("parallel", "parallel"),
        name="modulation",
    )(cvec, ada_w, ada_b.reshape(depth, 1, n))


NORM_ROWS = 16


def _modulated_norm_rows(x_ref, g_ref, mod_ref, shift_row, scale_row, h_ref, start, n_rows):
    for r0 in range(0, n_rows, NORM_ROWS):
        rows = pl.ds(start + r0, NORM_ROWS)
        x = x_ref[rows, :]
        y = x * lax.rsqrt(jnp.mean(x * x, axis=-1, keepdims=True) + EPS) * g_ref[...]
        h = y * (1.0 + mod_ref[scale_row:scale_row + 1, :]) + mod_ref[shift_row:shift_row + 1, :]
        h_ref[rows, :] = h.astype(BF16)


def _modulated_norm(x_ref, g_ref, mod_ref, shift_row, scale_row, h_ref):
    block = 8 * NORM_ROWS

    def body(c, carry):
        start = pl.multiple_of(c * block, block)
        _modulated_norm_rows(x_ref, g_ref, mod_ref, shift_row, scale_row, h_ref, start, block)
        return carry

    lax.fori_loop(0, x_ref.shape[0] // block, body, 0)


def _inproj_kernel(x_ref, mod_ref, g_ref, w_ref, z_ref, h_ref):
    @pl.when(pl.program_id(1) == 0)
    def _():
        _modulated_norm(x_ref, g_ref, mod_ref, MOD_SHIFT1, MOD_SCALE1, h_ref)

    z_ref[...] = _dot(h_ref[...], w_ref[...])


def _in_projection(lay, l, x, mods, norm_g, w_in):
    m, d = x.shape
    n = w_in.shape[-1]
    tm = lay.row_tile(1024)
    tn = 512
    return pl.pallas_call(
        _inproj_kernel,
        grid=(m // tm, n // tn),
        in_specs=[
            pl.BlockSpec((tm, d), lambda i, j: (i, 0)),
            pl.BlockSpec((None, None, MOD_ROWS, d), lambda i, j: (l, lay.mod_row(i, tm), 0, 0)),
            pl.BlockSpec((None, 1, d), lambda i, j: (l, 0, 0)),
            pl.BlockSpec((None, d, tn), lambda i, j: (l, 0, j)),
        ],
        out_specs=[
            pl.BlockSpec((tm, tn), lambda i, j: (i, j)),
            pl.BlockSpec((tm, d), lambda i, j: (i, 0)),
        ],
        out_shape=[jax.ShapeDtypeStruct((m, n), F32), jax.ShapeDtypeStruct((m, d), BF16)],
        compiler_params=_cparams("parallel", "arbitrary"),
        name="in_projection",
    )(x, mods, norm_g.reshape(norm_g.shape[0], 1, d), w_in)


def _residual_matmul_kernel(a_ref, w_ref, x_ref, mod_ref, o_ref, *, gate_row):
    a = a_ref[...]
    chunk = 512
    for c0 in range(0, o_ref.shape[1], chunk):
        sl = slice(c0, c0 + chunk)
        o_ref[:, sl] = x_ref[:, sl] + mod_ref[gate_row:gate_row + 1, sl] * _dot(a, w_ref[:, sl])


def _residual_matmul(lay, l, a, w, x, mods, gate_row, name, tm, tn):
    m, k = a.shape
    d = w.shape[-1]
    tm = lay.row_tile(tm)
    return pl.pallas_call(
        functools.partial(_residual_matmul_kernel, gate_row=gate_row),
        grid=(m // tm, d // tn),
        in_specs=[
            pl.BlockSpec((tm, k), lambda i, j: (i, 0)),
            pl.BlockSpec((None, k, tn), lambda i, j: (l, 0, j)),
            pl.BlockSpec((tm, tn), lambda i, j: (i, j)),
            pl.BlockSpec((None, None, MOD_ROWS, tn), lambda i, j: (l, lay.mod_row(i, tm), 0, j)),
        ],
        out_specs=pl.BlockSpec((tm, tn), lambda i, j: (i, j)),
        out_shape=jax.ShapeDtypeStruct((m, d), F32),
        compiler_params=_cparams("parallel", "arbitrary"),
        name=name,
    )(a, w, x, mods)


def _out_proj_kernel(a_ref, w_ref, x_ref, mod_ref, g_ref, o_ref, h_ref):
    a = a_ref[...]
    chunk = 512
    for c0 in range(0, o_ref.shape[1], chunk):
        sl = slice(c0, c0 + chunk)
        o_ref[:, sl] = x_ref[:, sl] + mod_ref[MOD_GATE1:MOD_GATE1 + 1, sl] * _dot(a, w_ref[:, sl])
    _modulated_norm(o_ref, g_ref, mod_ref, MOD_SHIFT2, MOD_SCALE2, h_ref)


def _out_projection(lay, l, a, w_o, x, mods, norm2_g):
    m, k = a.shape
    d = w_o.shape[-1]
    tm = lay.row_tile(512)
    return pl.pallas_call(
        _out_proj_kernel,
        grid=(m // tm,),
        in_specs=[
            pl.BlockSpec((tm, k), lambda i: (i, 0)),
            pl.BlockSpec((None, k, d), lambda i: (l, 0, 0)),
            pl.BlockSpec((tm, d), lambda i: (i, 0)),
            pl.BlockSpec((None, None, MOD_ROWS, d), lambda i: (l, lay.mod_row(i, tm), 0, 0)),
            pl.BlockSpec((None, 1, d), lambda i: (l, 0, 0)),
        ],
        out_specs=[pl.BlockSpec((tm, d), lambda i: (i, 0)), pl.BlockSpec((tm, d), lambda i: (i, 0))],
        out_shape=[jax.ShapeDtypeStruct((m, d), F32), jax.ShapeDtypeStruct((m, d), BF16)],
        compiler_params=_cparams("parallel"),
        name="out_projection",
    )(a, w_o, x, mods, norm2_g.reshape(norm2_g.shape[0], 1, d))


def _ffn_up_kernel(h_ref, w1_ref, w3_ref, o_ref):
    h = h_ref[...]
    tn = o_ref.shape[1]
    tail = tn % MXU_WIDTH
    c0 = 0
    while c0 < tn - tail:
        c1 = min(c0 + 512, tn - tail)
        o_ref[:, c0:c1] = (_silu(_dot(h, w1_ref[:, c0:c1])) * _dot(h, w3_ref[:, c0:c1])).astype(BF16)
        c0 = c1
    if tail:
        both = _dot(h, jnp.concatenate([w1_ref[:, c0:], w3_ref[:, c0:]], axis=1))
        o_ref[:, c0:] = (_silu(both[:, :tail]) * both[:, tail:]).astype(BF16)


def _ffn_up(lay, l, h, w1, w3):
    m, d = h.shape
    n = w1.shape[-1]
    tm = lay.row_tile(1024)
    tn = n // 4
    assert n % 4 == 0 and tn % LANES == 0
    return pl.pallas_call(
        _ffn_up_kernel,
        grid=(m // tm, n // tn),
        in_specs=[
            pl.BlockSpec((tm, d), lambda i, j: (i, 0)),
            pl.BlockSpec((None, d, tn), lambda i, j: (l, 0, j)),
            pl.BlockSpec((None, d, tn), lambda i, j: (l, 0, j)),
        ],
        out_specs=pl.BlockSpec((tm, tn), lambda i, j: (i, j)),
        out_shape=jax.ShapeDtypeStruct((m, n), BF16),
        compiler_params=_cparams("parallel", "parallel"),
        name="ffn_up",
    )(h, w1, w3)


def _qk_prep_kernel(q_ref, k_ref, cos_ref, sa_ref, sb_ref, qg_ref, kg_ref, qo_ref, ko_ref, kf_ref):
    cos, sin_a, sin_b = cos_ref[...], sa_ref[...], sb_ref[...]

    def norm(x, g):
        return x * lax.rsqrt(jnp.mean(x * x, axis=-1, keepdims=True) + EPS) * g

    def rope(x):
        return (x * cos + pltpu.roll(x, HEAD_DIM - ROT_AXIS // 2, 1) * sin_a
                + pltpu.roll(x, ROT_AXIS // 2, 1) * sin_b)

    for h in range(N_HEADS_A):
        sl = slice(h * HEAD_DIM, (h + 1) * HEAD_DIM)
        qo_ref[:, sl] = (rope(norm(q_ref[:, sl], qg_ref[...])) * ATTN_SCALE).astype(BF16)
    for h in range(N_KV_A):
        sl = slice(h * HEAD_DIM, (h + 1) * HEAD_DIM)
        kn = norm(k_ref[:, sl], kg_ref[...])
        kf_ref[:, sl] = kn
        ko_ref[:, sl] = rope(kn).astype(BF16)


def _qk_prep(lay, l, z, rope_tabs, q_norm_g, k_norm_g):
    m = z.shape[0]
    tm = lay.row_tile(512)
    cos, sin_a, sin_b = rope_tabs
    tab_spec = pl.BlockSpec((tm, HEAD_DIM), lambda i: (i, 0))
    g_spec = pl.BlockSpec((None, 1, HEAD_DIM), lambda i: (l, 0, 0))
    return pl.pallas_call(
        _qk_prep_kernel,
        grid=(m // tm,),
        in_specs=[
            pl.BlockSpec((tm, A_WIDTH), lambda i: (i, 0)),
            pl.BlockSpec((tm, KV_WIDTH), lambda i: (i, KV_COL_K)),
            tab_spec, tab_spec, tab_spec, g_spec, g_spec,
        ],
        out_specs=[
            pl.BlockSpec((tm, A_WIDTH), lambda i: (i, 0)),
            pl.BlockSpec((tm, KV_WIDTH), lambda i: (i, 0)),
            pl.BlockSpec((tm, KV_WIDTH), lambda i: (i, 0)),
        ],
        out_shape=[
            jax.ShapeDtypeStruct((m, A_WIDTH), BF16),
            jax.ShapeDtypeStruct((m, KV_WIDTH), BF16),
            jax.ShapeDtypeStruct((m, KV_WIDTH), F32),
        ],
        compiler_params=_cparams("parallel"),
        name="qk_prep",
    )(z, z, cos, sin_a, sin_b, q_norm_g.reshape(-1, 1, HEAD_DIM), k_norm_g.reshape(-1, 1, HEAD_DIM))


def _stack_heads(q_ref, kv):
    return jnp.concatenate(
        [q_ref[:, (kv * GROUP_A + g) * HEAD_DIM:(kv * GROUP_A + g + 1) * HEAD_DIM] for g in range(GROUP_A)], axis=0)


def _fold_lane_tiles(x, op):
    out = x[:, :LANES]
    for c0 in range(LANES, x.shape[1], LANES):
        out = op(out, x[:, c0:c0 + LANES])
    return out


def _sink_column(sink_ref, kv, rows):
    return jnp.concatenate(
        [jnp.full((rows, 1), sink_ref[kv * GROUP_A + g], F32) for g in range(GROUP_A)], axis=0)


def _context_attention_kernel(sink_ref, q_ref, k_ref, v_ref, o_ref):
    rows = q_ref.shape[0]
    for kv in range(N_KV_A):
        sl = slice(kv * HEAD_DIM, (kv + 1) * HEAD_DIM)
        q = _stack_heads(q_ref, kv)
        s = _dot_nt(q, k_ref[:, sl])
        sink = _sink_column(sink_ref, kv, rows)
        mx = jnp.maximum(jnp.max(_fold_lane_tiles(s, jnp.maximum), axis=-1, keepdims=True), sink)
        p = jnp.exp(s - mx)
        den = jnp.sum(_fold_lane_tiles(p, jnp.add), axis=-1, keepdims=True) + jnp.exp(sink - mx)
        o = _dot(p.astype(BF16), v_ref[:, sl].astype(BF16)) * (1.0 / den)
        for g in range(GROUP_A):
            h = kv * GROUP_A + g
            o_ref[:, h * HEAD_DIM:(h + 1) * HEAD_DIM] = o[g * rows:(g + 1) * rows].astype(BF16)


def _context_attention(lay, l, qn, kn, z, sink):
    seq = lay.seq
    return pl.pallas_call(
        _context_attention_kernel,
        grid=(lay.batch,),
        in_specs=[
            pl.BlockSpec(memory_space=pltpu.SMEM),
            pl.BlockSpec((seq, A_WIDTH), lambda b: (b, 0)),
            pl.BlockSpec((seq, KV_WIDTH), lambda b: (b, 0)),
            pl.BlockSpec((seq, KV_WIDTH), lambda b: (b, KV_COL_V)),
        ],
        out_specs=pl.BlockSpec((seq, A_WIDTH), lambda b: (b, 0)),
        out_shape=jax.ShapeDtypeStruct((lay.p_rows, A_WIDTH), BF16),
        compiler_params=_cparams("parallel"),
        name="context_attention",
    )(sink, qn, kn, z)


def _latent_attention_kernel(sink_ref, q_ref, kp_ref, kc_ref, kn_ref, vp_ref, vc_ref, vn_ref, ck_ref, cv_ref,
                             o_ref, *, seq_len):
    assert WINDOW >= BLOCK - 1
    j = pl.program_id(1)
    n_blocks = seq_len // BLOCK
    r = lax.broadcasted_iota(jnp.int32, (BLOCK, BLOCK), 0)
    c = lax.broadcasted_iota(jnp.int32, (BLOCK, BLOCK), 1)
    valid_prev = jnp.concatenate([(r - (c - BLOCK) <= WINDOW) & (j >= 1)] * GROUP_A, axis=0)
    valid_next = jnp.concatenate([((c + BLOCK) - r <= WINDOW) & (j <= n_blocks - 2)] * GROUP_A, axis=0)
    for kv in range(N_KV_A):
        sl = slice(kv * HEAD_DIM, (kv + 1) * HEAD_DIM)
        q = _stack_heads(q_ref, kv)
        s_p = jnp.where(valid_prev, _dot_nt(q, kp_ref[:, sl]), NEG_MASK)
        s_o = _dot_nt(q, kc_ref[:, sl])
        s_n = jnp.where(valid_next, _dot_nt(q, kn_ref[:, sl]), NEG_MASK)
        s_c = _dot_nt(q, ck_ref[:, sl].astype(BF16))
        sink = _sink_column(sink_ref, kv, BLOCK)
        mx = jnp.maximum(jnp.maximum(s_p, s_o), jnp.maximum(s_n, _fold_lane_tiles(s_c, jnp.maximum)))
        mx = jnp.maximum(jnp.max(mx, axis=-1, keepdims=True), sink)
        psum = None
        o = None
        for s, v_ref in ((s_p, vp_ref), (s_o, vc_ref), (s_n, vn_ref), (s_c, cv_ref)):
            p = jnp.exp(s - mx)
            pf = _fold_lane_tiles(p, jnp.add)
            psum = pf if psum is None else psum + pf
            term = _dot(p.astype(BF16), v_ref[:, sl].astype(BF16))
            o = term if o is None else o + term
        den = jnp.sum(psum, axis=-1, keepdims=True) + jnp.exp(sink - mx)
        o = o * (1.0 / den)
        for g in range(GROUP_A):
            h = kv * GROUP_A + g
            o_ref[:, h * HEAD_DIM:(h + 1) * HEAD_DIM] = o[g * BLOCK:(g + 1) * BLOCK].astype(BF16)


def _latent_attention(lay, l, qn, kn, z, ctx_k, ctx_v, sink):
    nb = lay.dec_seq // BLOCK
    base = lay.p_rows // BLOCK
    past = ctx_k.shape[2]

    def blk(shift, col):
        return lambda b, j: (base + b * nb + jnp.clip(j + shift, 0, nb - 1), col)

    k_spec = [pl.BlockSpec((BLOCK, KV_WIDTH), blk(s, 0)) for s in (-1, 0, 1)]
    v_spec = [pl.BlockSpec((BLOCK, KV_WIDTH), blk(s, KV_COL_V)) for s in (-1, 0, 1)]
    ctx_spec = pl.BlockSpec((None, None, past, KV_WIDTH), lambda b, j: (b, l, 0, 0))
    return pl.pallas_call(
        functools.partial(_latent_attention_kernel, seq_len=lay.dec_seq),
        grid=(lay.dec_batch, nb),
        in_specs=[pl.BlockSpec(memory_space=pltpu.SMEM), pl.BlockSpec((BLOCK, A_WIDTH), blk(0, 0))]
        + k_spec + v_spec + [ctx_spec, ctx_spec],
        out_specs=pl.BlockSpec((BLOCK, A_WIDTH), lambda b, j: (b * nb + j, 0)),
        out_shape=jax.ShapeDtypeStruct((lay.s_rows, A_WIDTH), BF16),
        compiler_params=_cparams("parallel", "parallel"),
        name="latent_attention",
    )(sink, qn, kn, kn, kn, z, z, z, ctx_k, ctx_v)


def _conv_kernel(a_c, a_p, a_n, g_c, g_p, g_n, b_c, c_c, c_p, c_n, h_c, h_p, h_n,
                 dw_ref, db_ref, lg_ref, lb_ref, sw_ref, ub_ref, uc_ref, xb_s, xc_s, *, lay):
    _, pos, count = lay.seq_pos(pl.program_id(0), SEQ_TILE)
    has_prev = pos > 0
    has_next = pos < count - 1
    rb = 32
    xb_s[0, 0:HALO, :] = jnp.where(has_prev, a_p[...] * _sigmoid(g_p[...]), 0.0)
    xb_s[0, HALO + SEQ_TILE:, :] = jnp.where(has_next, a_n[...] * _sigmoid(g_n[...]), 0.0)
    xc_s[0:HALO, :] = jnp.where(has_prev, c_p[...] * h_p[...], 0.0)
    xc_s[HALO + SEQ_TILE:, :] = jnp.where(has_next, c_n[...] * h_n[...], 0.0)
    for r0 in range(0, SEQ_TILE, rb):
        rows = slice(r0, r0 + rb)
        xb_s[0, HALO + r0:HALO + r0 + rb, :] = a_c[rows, :] * _sigmoid(g_c[rows, :])
        xc_s[HALO + r0:HALO + r0 + rb, :] = c_c[rows, :] * h_c[rows, :]
    shifted_rows = SEQ_TILE + 2 * HALO - SUBLANES
    chunk = 40
    for s in range(1, SUBLANES):
        for r0 in range(0, shifted_rows, chunk):
            xb_s[s, r0:r0 + chunk, :] = xb_s[0, r0 + s:r0 + s + chunk, :]
    for r0 in range(0, SEQ_TILE, rb):
        acc = jnp.zeros((rb // SUBLANES, SUBLANES, CONF_W), F32)
        for k in range(CONF_K):
            off = HALO - CONF_K // 2 + k
            row = r0 + off - off % SUBLANES
            x = xb_s[off % SUBLANES, row:row + rb, :]
            acc = acc + dw_ref[k] * x.reshape(rb // SUBLANES, SUBLANES, CONF_W)
        u = acc.reshape(rb, CONF_W) + db_ref[...]
        uc = u - jnp.mean(u, axis=-1, keepdims=True)
        var = jnp.mean(uc * uc, axis=-1, keepdims=True)
        u = uc * lax.rsqrt(var + EPS) * lg_ref[...] + lb_ref[...]
        ub_ref[r0:r0 + rb, :] = _silu(u).astype(BF16)
        acc = jnp.zeros((rb // SUBLANES, SUBLANES, SC_W), F32)
        for k in range(SC_K):
            off = HALO - SC_K // 2 + r0 + k
            acc = acc + sw_ref[k] * xc_s[off:off + rb, :].reshape(rb // SUBLANES, SUBLANES, SC_W)
        uc_ref[r0:r0 + rb, :] = (b_c[r0:r0 + rb, :] * acc.reshape(rb, SC_W)).astype(BF16)


def _conv_mixers(lay, l, z, conf_dw_w, conf_dw_b, conf_ln_g, conf_ln_b, sc_conv_w):
    m = z.shape[0]
    n_tiles = m // SEQ_TILE
    per = SEQ_TILE // HALO
    n_halo = m // HALO

    def cur(col):
        return pl.BlockSpec((SEQ_TILE, COL_W), lambda i: (i, col))

    def prev(col):
        return pl.BlockSpec((HALO, COL_W), lambda i: (jnp.maximum(i * per - 1, 0), col))

    def nxt(col):
        return pl.BlockSpec((HALO, COL_W), lambda i: (jnp.minimum((i + 1) * per, n_halo - 1), col))

    def trio(col):
        return [cur(col), prev(col), nxt(col)]

    def par(rows):
        return pl.BlockSpec((None, rows, COL_W), lambda i: (l, 0, 0))

    def taps(n):
        return pl.BlockSpec((None, n, SUBLANES, COL_W), lambda i: (l, 0, 0, 0))

    depth = conf_dw_w.shape[0]
    dw = jnp.broadcast_to(conf_dw_w[:, :, None, :], (depth, CONF_K, SUBLANES, CONF_W))
    sw = jnp.broadcast_to(sc_conv_w[:, :, None, :], (depth, SC_K, SUBLANES, SC_W))
    return pl.pallas_call(
        functools.partial(_conv_kernel, lay=lay),
        grid=(n_tiles,),
        in_specs=trio(COL_CONF_A) + trio(COL_CONF_G) + [cur(COL_SC_B)] + trio(COL_SC_C) + trio(COL_SC_H)
        + [taps(CONF_K), par(1), par(1), par(1), taps(SC_K)],
        out_specs=[pl.BlockSpec((SEQ_TILE, CONF_W), lambda i: (i, 0)),
                   pl.BlockSpec((SEQ_TILE, SC_W), lambda i: (i, 0))],
        out_shape=[jax.ShapeDtypeStruct((m, CONF_W), BF16), jax.ShapeDtypeStruct((m, SC_W), BF16)],
        scratch_shapes=[pltpu.VMEM((SUBLANES, SEQ_TILE + 2 * HALO, CONF_W), F32),
                        pltpu.VMEM((SEQ_TILE + 2 * HALO, SC_W), F32)],
        compiler_params=_cparams("parallel"),
        name="conv_mixers",
    )(*([z] * 13), dw, conf_dw_b.reshape(depth, 1, CONF_W), conf_ln_g.reshape(depth, 1, CONF_W),
      conf_ln_b.reshape(depth, 1, CONF_W), sw)


def _hgrn_gates(hq_ref, hf_ref, lb_ref, sl, tri_b):
    zq = hq_ref[:, sl]
    q = _silu(zq) * (HG_DK ** -0.5)
    zf = hf_ref[:, sl]
    lb = lb_ref[:, sl]
    e = jnp.exp(-jnp.abs(zf))
    r = 1.0 / (1.0 + e)
    nonneg = zf >= 0
    sig_pos = jnp.where(nonneg, r, e * r)
    sig_neg = jnp.where(nonneg, e * r, r)
    logf = jnp.log(lb + (1.0 - lb) * sig_pos)
    kk = (1.0 - lb) * sig_neg
    g1 = logf.astype(BF16)
    r1 = logf - g1.astype(F32)
    g2 = r1.astype(BF16)
    g3 = (r1 - g2.astype(F32)).astype(BF16)
    b = _dot(tri_b, g1) + _dot(tri_b, g2) + _dot(tri_b, g3)
    return q, kk, b


def _hgrn_exact_scores(a_ref, q_s, k_s, b_s, reverse):
    ts = a_ref.shape[0]
    a_ref[...] = jnp.zeros_like(a_ref)

    def body(s, carry):
        c0 = pl.multiple_of((s // HG_CHUNK) * HG_CHUNK, HG_CHUNK)
        w = jnp.exp(jnp.minimum(b_s[pl.ds(c0, HG_CHUNK), :] - b_s[pl.ds(s, 1), :], 0.0))
        prod = q_s[pl.ds(c0, HG_CHUNK), :] * k_s[pl.ds(s, 1), :] * w
        trow = c0 + lax.broadcasted_iota(jnp.int32, (HG_CHUNK, 1), 0)
        ok = (trow <= s) if reverse else (trow >= s)
        onehot = (lax.broadcasted_iota(jnp.int32, (1, ts), 1) == s).astype(F32)
        colv = jnp.sum(prod, axis=-1, keepdims=True)
        a_ref[pl.ds(c0, HG_CHUNK), :] += jnp.where(ok, colv, 0.0) * onehot
        return carry

    lax.fori_loop(0, ts, body, 0)


def _hgrn_kernel(hq_ref, hf_ref, hi_ref, lb_ref, s0_ref, o_ref, sf_ref, st_ref, oi_s, a_ref, q_s, k_s, b_s,
                 *, lay, reverse):
    ts = SEQ_TILE
    n_chunks = ts // HG_CHUNK
    _, pos, count = lay.seq_pos(pl.program_id(0), ts)

    @pl.when(pos == 0)
    def _():
        st_ref[...] = s0_ref[...]

    row = lax.broadcasted_iota(jnp.int32, (ts, ts), 0)
    col = lax.broadcasted_iota(jnp.int32, (ts, ts), 1)
    same = _chunk_of(row, HG_CHUNK) == _chunk_of(col, HG_CHUNK)
    tri = same & ((col >= row) if reverse else (col <= row))
    tri_b = tri.astype(F32).astype(BF16)
    zero_b = jnp.zeros((), BF16)
    blk_row = _chunk_of(lax.broadcasted_iota(jnp.int32, (ts, n_chunks * HG_DK), 0), HG_CHUNK)
    blk_col = _chunk_of(lax.broadcasted_iota(jnp.int32, (ts, n_chunks * HG_DK), 1), HG_DK)
    blk = blk_row == blk_col
    order = range(n_chunks - 1, -1, -1) if reverse else range(n_chunks)
    last = 0 if reverse else HG_CHUNK - 1

    b_min = None
    for hd in range(HG_HEADS):
        sl = slice(hd * HG_DK, (hd + 1) * HG_DK)
        q, kk, b = _hgrn_gates(hq_ref, hf_ref, lb_ref, sl, tri_b)
        head_min = jnp.min(b)
        b_min = head_min if b_min is None else jnp.minimum(b_min, head_min)
        b3 = b.reshape(n_chunks, HG_CHUNK, HG_DK)
        total3 = b3[:, last:last + 1, :]
        qt = (q * jnp.exp(b)).astype(BF16)
        kend = (kk * jnp.exp(jnp.broadcast_to(total3, b3.shape).reshape(ts, HG_DK) - b)).astype(BF16)
        decay = jnp.exp(total3.reshape(n_chunks, HG_DK))
        v_h = hi_ref[:, sl]
        kinv = (kk * jnp.exp(-b)).astype(BF16)
        scores = jnp.where(tri, _dot_nt(qt, kinv), 0.0).astype(BF16)
        o_intra = _dot(scores, v_h.astype(BF16))
        k_blocks = jnp.where(blk, jnp.concatenate([kend] * n_chunks, axis=1), zero_b)
        incr = _dot(v_h.T.astype(BF16), k_blocks)
        state = st_ref[hd]
        starts = [None] * n_chunks
        for c in order:
            starts[c] = state
            state = state * decay[c:c + 1, :] + incr[:, c * HG_DK:(c + 1) * HG_DK]
        st_ref[hd] = state
        q_blocks = jnp.where(blk, jnp.concatenate([qt] * n_chunks, axis=1), zero_b)
        o_inter = _dot_nt(q_blocks, jnp.concatenate(starts, axis=1).astype(BF16))
        oi_s[:, sl] = o_inter
        o_ref[:, sl] = o_intra + o_inter

    @pl.when(b_min <= HG_SAFE_LOG_DECAY)
    def _():
        for hd in range(HG_HEADS):
            sl = slice(hd * HG_DK, (hd + 1) * HG_DK)
            q, kk, b = _hgrn_gates(hq_ref, hf_ref, lb_ref, sl, tri_b)
            q_s[...] = q
            k_s[...] = kk
            b_s[...] = b
            _hgrn_exact_scores(a_ref, q_s, k_s, b_s, reverse)
            o_ref[:, sl] = _dot(a_ref[...].astype(BF16), hi_ref[:, sl].astype(BF16)) + oi_s[:, sl]

    @pl.when(pos == count - 1)
    def _():
        sf_ref[...] = st_ref[...]


def _hgrn_direction(lay, z, lb, s0_t, reverse):
    m = z.shape[0]
    n_tiles = m // SEQ_TILE

    def tile_of(n):
        seq_idx, pos, count = lay.seq_pos(n, SEQ_TILE)
        return (n - pos + (count - 1 - pos)) if reverse else n

    def zcol(col):
        return pl.BlockSpec((SEQ_TILE, COL_W), lambda n: (tile_of(n), col))

    state_spec = pl.BlockSpec((None, HG_HEADS, HG_DV, HG_DK), lambda n: (lay.seq_pos(n, SEQ_TILE)[0], 0, 0, 0))
    return pl.pallas_call(
        functools.partial(_hgrn_kernel, lay=lay, reverse=reverse),
        grid=(n_tiles,),
        in_specs=[zcol(COL_HQ), zcol(COL_HF_BWD if reverse else COL_HF_FWD), zcol(COL_HI),
                  pl.BlockSpec((1, HG_W), lambda n: (0, 0)), state_spec],
        out_specs=[pl.BlockSpec((SEQ_TILE, HG_W), lambda n: (tile_of(n), 0)), state_spec],
        out_shape=[jax.ShapeDtypeStruct((m, HG_W), F32),
                   jax.ShapeDtypeStruct((lay.n_seq, HG_HEADS, HG_DV, HG_DK), F32)],
        scratch_shapes=[pltpu.VMEM((HG_HEADS, HG_DV, HG_DK), F32),
                        pltpu.VMEM((SEQ_TILE, HG_W), F32),
                        pltpu.VMEM((SEQ_TILE, SEQ_TILE), F32),
                        pltpu.VMEM((SEQ_TILE, HG_DK), F32),
                        pltpu.VMEM((SEQ_TILE, HG_DK), F32),
                        pltpu.VMEM((SEQ_TILE, HG_DK), F32)],
        compiler_params=_cparams("arbitrary"),
        name="hgrn_bwd" if reverse else "hgrn_fwd",
    )(z, z, z, lb, s0_t)


def _hgrn_out_kernel(of_ref, ob_ref, hg_ref, g_ref, o_ref):
    for hd in range(HG_HEADS):
        sl = slice(hd * HG_DV, (hd + 1) * HG_DV)
        o = of_ref[:, sl] + ob_ref[:, sl]
        o = o * lax.rsqrt(jnp.mean(o * o, axis=-1, keepdims=True) + EPS) * g_ref[...]
        o_ref[:, sl] = (o * _silu(hg_ref[:, sl])).astype(BF16)


def _hgrn_output(lay, l, o_f, o_b, z, hg_norm_g):
    m = z.shape[0]
    tm = lay.row_tile(512)
    row_spec = pl.BlockSpec((tm, HG_W), lambda i: (i, 0))
    return pl.pallas_call(
        _hgrn_out_kernel,
        grid=(m // tm,),
        in_specs=[row_spec, row_spec, pl.BlockSpec((tm, COL_W), lambda i: (i, COL_HG)),
                  pl.BlockSpec((None, 1, HG_DV), lambda i: (l, 0, 0))],
        out_specs=row_spec,
        out_shape=jax.ShapeDtypeStruct((m, HG_W), BF16),
        compiler_params=_cparams("parallel"),
        name="hgrn_output",
    )(o_f, o_b, z, hg_norm_g.reshape(-1, 1, HG_DV))


def _merge_kernel(h_ref, ap_ref, as_ref, b_ref, c_ref, d_ref, wg0, wg1, wg2, wg3, bg0, bg1, bg2, bg3,
                  wa_ref, wb_ref, wc_ref, wd_ref, o_ref, *, p_tiles):
    h = h_ref[...]
    attn = jnp.where(pl.program_id(1) < p_tiles, ap_ref[...], as_ref[...])
    acc = None
    for branch, w_ref, wg_ref, bg_ref in ((attn, wa_ref, wg0, bg0), (b_ref[...], wb_ref, wg1, bg1),
                                          (c_ref[...], wc_ref, wg2, bg2), (d_ref[...], wd_ref, wg3, bg3)):
        gate = _sigmoid(_dot(h, wg_ref[...]) + bg_ref[...])
        term = gate * _dot(branch, w_ref[...])
        acc = term if acc is None else acc + term
    o_ref[...] = acc.astype(BF16)


def _merge(lay, l, h, attn_p, attn_s, branches, w_gate, b_gate, w_outs):
    m, d = h.shape
    tm = lay.row_tile(512)
    tn = 512
    nj = d // tn
    depth = w_gate.shape[0]
    p_tiles = lay.p_rows // tm
    s_tiles = lay.s_rows // tm

    def gate_w(n):
        return pl.BlockSpec((None, d, tn), lambda j, i: (l, 0, n * nj + j))

    def gate_b(n):
        return pl.BlockSpec((None, 1, tn), lambda j, i: (l, 0, n * nj + j))

    b_gate3 = b_gate.reshape(depth, 1, N_BRANCH * d)
    return pl.pallas_call(
        functools.partial(_merge_kernel, p_tiles=p_tiles),
        grid=(nj, m // tm),
        in_specs=[pl.BlockSpec((tm, d), lambda j, i: (i, 0)),
                  pl.BlockSpec((tm, A_WIDTH), lambda j, i: (jnp.minimum(i, p_tiles - 1), 0)),
                  pl.BlockSpec((tm, A_WIDTH), lambda j, i: (jnp.clip(i - p_tiles, 0, s_tiles - 1), 0))]
        + [pl.BlockSpec((tm, br.shape[1]), lambda j, i: (i, 0)) for br in branches]
        + [gate_w(n) for n in range(N_BRANCH)] + [gate_b(n) for n in range(N_BRANCH)]
        + [pl.BlockSpec((None, w.shape[1], tn), lambda j, i: (l, 0, j)) for w in w_outs],
        out_specs=pl.BlockSpec((tm, tn), lambda j, i: (i, j)),
        out_shape=jax.ShapeDtypeStruct((m, d), BF16),
        compiler_params=_cparams("parallel", "parallel"),
        name="merge",
    )(h, attn_p, attn_s, *branches, *([w_gate] * N_BRANCH), *([b_gate3] * N_BRANCH), *w_outs)


def _rope_tables(lay):
    t = lay.dec_seq
    half = ROT_AXIS // 2
    rows = t // GRID_W
    row = jnp.repeat(jnp.arange(rows, dtype=F32), GRID_W)
    col = jnp.tile(jnp.arange(GRID_W, dtype=F32), rows)
    inv = ROPE_BASE ** (-jnp.arange(0, ROT_AXIS, 2, dtype=F32) / ROT_AXIS)
    ang = jnp.concatenate([row[:, None] * inv, row[:, None] * inv, col[:, None] * inv, col[:, None] * inv], axis=1)
    lane = jnp.arange(HEAD_DIM)
    first_half = (lane % ROT_AXIS) < half
    cos = jnp.cos(ang)
    sin = jnp.sin(ang)
    sin_a = jnp.where(first_half, -sin, 0.0)
    sin_b = jnp.where(first_half, 0.0, sin)

    def full(tab, fill):
        return jnp.concatenate([jnp.full((lay.p_rows, HEAD_DIM), fill, F32)] + [tab] * lay.dec_batch, axis=0)

    return full(cos, 1.0), full(sin_a, 0.0), full(sin_b, 0.0)


def kernel(x_prompt, x_sample, cache_k, cache_v, state_hgrn_fwd, state_hgrn_bwd, c, c_ctx, ada_w, ada_b, norm1_g, norm2_g, w_in, q_norm_g, k_norm_g, attn_sink, w_attn_out, conf_dw_w, conf_dw_b, conf_ln_g, conf_ln_b, w_conf_out, sc_conv_w, w_sc_out, hg_lb, hg_norm_g, w_hg_out, w_gate, b_gate, w_o, ffn_w1, ffn_w3, ffn_w2):
    batch, seq, d = x_prompt.shape
    dec_batch, dec_seq, _ = x_sample.shape
    depth = ada_w.shape[0]
    past = cache_k.shape[2]
    lay = _Layout(batch, seq, dec_batch, dec_seq)
    assert seq % SEQ_TILE == 0 and dec_seq % SEQ_TILE == 0 and dec_seq % GRID_W == 0
    assert 1 + dec_batch <= MOD_ROWS

    wb = {name: w.astype(BF16) for name, w in dict(
        w_in=w_in, w_attn_out=w_attn_out, w_conf_out=w_conf_out, w_sc_out=w_sc_out, w_hg_out=w_hg_out,
        w_gate=w_gate, w_o=w_o, ffn_w1=ffn_w1, ffn_w3=ffn_w3, ffn_w2=ffn_w2).items()}

    cvec = jnp.concatenate([c_ctx[None], c, jnp.zeros((MOD_ROWS - 1 - dec_batch, d), F32)], axis=0)
    mods = _modulation(cvec, ada_w, ada_b)
    mods = mods.reshape(depth, MOD_ROWS, 6, d)[:, :1 + dec_batch]
    mods = jnp.pad(mods, ((0, 0), (0, 0), (0, MOD_ROWS - 6), (0, 0)))

    lb = jax.nn.softmax(hg_lb.astype(F32), axis=1)
    lb = jnp.maximum(jnp.cumsum(lb, axis=1) - lb[:, :1], 0.0)

    rope_tabs = _rope_tables(lay)
    ctx_k = cache_k.reshape(dec_batch, depth, past, KV_WIDTH)
    ctx_v = cache_v.reshape(dec_batch, depth, past, KV_WIDTH)
    zero_states = jnp.zeros((batch, HG_HEADS, HG_DV, HG_DK), F32)

    x = jnp.concatenate([x_prompt.reshape(lay.p_rows, d), x_sample.reshape(lay.s_rows, d)], axis=0)
    ks_out, vs_out, sf_out, sb_out = [], [], [], []
    for l in range(depth):
        z, h = _in_projection(lay, l, x, mods, norm1_g, wb["w_in"])
        qn, kn, kf = _qk_prep(lay, l, z, rope_tabs, q_norm_g, k_norm_g)
        attn_p = _context_attention(lay, l, qn, kn, z, attn_sink[l])
        attn_s = _latent_attention(lay, l, qn, kn, z, ctx_k, ctx_v, attn_sink[l])
        u_b, u_c = _conv_mixers(lay, l, z, conf_dw_w, conf_dw_b, conf_ln_g, conf_ln_b, sc_conv_w)
        s0_f = jnp.concatenate([zero_states, jnp.swapaxes(state_hgrn_fwd[:, l], -1, -2)], axis=0)
        s0_b = jnp.concatenate([zero_states, jnp.swapaxes(state_hgrn_bwd[:, l], -1, -2)], axis=0)
        o_f, s_f = _hgrn_direction(lay, z, lb[0, l][None], s0_f, reverse=False)
        o_b, s_b = _hgrn_direction(lay, z, lb[1, l][None], s0_b, reverse=True)
        o_d = _hgrn_output(lay, l, o_f, o_b, z, hg_norm_g)
        merged = _merge(lay, l, h, attn_p, attn_s, (u_b, u_c, o_d), wb["w_gate"], b_gate,
                        (wb["w_attn_out"], wb["w_conf_out"], wb["w_sc_out"], wb["w_hg_out"]))
        x, h2 = _out_projection(lay, l, merged, wb["w_o"], x, mods, norm2_g)
        t = _ffn_up(lay, l, h2, wb["ffn_w1"], wb["ffn_w3"])
        x = _residual_matmul(lay, l, t, wb["ffn_w2"], x, mods, MOD_GATE2, "ffn_down", tm=1024, tn=512)
        ks_out.append(kf[:lay.p_rows].reshape(batch, seq, N_KV_A, HEAD_DIM))
        vs_out.append(z[:lay.p_rows, A_WIDTH + KV_WIDTH:A_WIDTH + 2 * KV_WIDTH].reshape(batch, seq, N_KV_A, HEAD_DIM))
        sf_out.append(jnp.swapaxes(s_f[:batch], -1, -2))
        sb_out.append(jnp.swapaxes(s_b[:batch], -1, -2))
    return (x[:lay.p_rows].reshape(batch, seq, d), x[lay.p_rows:].reshape(dec_batch, dec_seq, d),
            jnp.stack(ks_out, axis=1), jnp.stack(vs_out, axis=1),
            jnp.stack(sf_out, axis=1), jnp.stack(sb_out, axis=1))
```

```python
import functools

import jax
import jax.numpy as jnp
from jax import lax
from jax.experimental import pallas as pl
from jax.experimental.pallas import tpu as pltpu

F32 = jnp.float32
BF16 = jnp.bfloat16

V7X_VMEM_BYTES = 64 * 1024 * 1024
VMEM_LIMIT_BYTES = V7X_VMEM_BYTES - 8 * 1024 * 1024
LANES = 128
SUBLANES = 8
MXU_WIDTH = 256

GRID_W = 64
HEAD_DIM = 128
N_HEADS_A = 8
N_KV_A = 2
GROUP_A = N_HEADS_A // N_KV_A
A_WIDTH = N_HEADS_A * HEAD_DIM
KV_WIDTH = N_KV_A * HEAD_DIM
WINDOW = 128
BLOCK = 128
ROPE_BASE = 10000.0
ROT_AXIS = HEAD_DIM // 2
ATTN_SCALE = HEAD_DIM ** -0.5
NEG_MASK = -1e9
CONF_W = 512
CONF_K = 31
SC_W = 512
SC_K = 3
HG_HEADS = 4
HG_DK = 128
HG_DV = 128
HG_W = HG_HEADS * HG_DK
HG_CHUNK = 32
N_BRANCH = 4
EPS = 1e-6

COL_W = 512
COL_CONF_A, COL_CONF_G, COL_SC_B, COL_SC_C, COL_SC_H = 3, 4, 5, 6, 7
COL_HQ, COL_HF_FWD, COL_HF_BWD, COL_HI, COL_HG = 8, 9, 10, 11, 12
KV_COL_K, KV_COL_V = A_WIDTH // KV_WIDTH, A_WIDTH // KV_WIDTH + 1

SEQ_TILE = 256
HALO = 16
HG_SAFE_LOG_DECAY = -60.0

MOD_SHIFT1, MOD_SCALE1, MOD_GATE1, MOD_SHIFT2, MOD_SCALE2, MOD_GATE2 = range(6)
MOD_ROWS = 8


def _cparams(*semantics):
    return pltpu.CompilerParams(dimension_semantics=semantics, vmem_limit_bytes=VMEM_LIMIT_BYTES)


def _sigmoid(x):
    return 1.0 / (1.0 + jnp.exp(-x))


def _silu(x):
    return x * _sigmoid(x)


def _chunk_of(idx, size):
    assert size & (size - 1) == 0
    return jnp.right_shift(idx, size.bit_length() - 1)


def _dot(a, b):
    return jnp.dot(a, b, preferred_element_type=F32)


def _dot_nt(a, b):
    return lax.dot_general(a, b, (((1,), (1,)), ((), ())), preferred_element_type=F32)


class _Layout:
    def __init__(self, batch, seq, dec_batch, dec_seq):
        self.batch, self.seq, self.dec_batch, self.dec_seq = batch, seq, dec_batch, dec_seq
        self.p_rows = batch * seq
        self.s_rows = dec_batch * dec_seq
        self.rows = self.p_rows + self.s_rows
        self.n_seq = batch + dec_batch

    def row_tile(self, preferred):
        t = preferred
        while self.p_rows % t or self.dec_seq % t:
            t //= 2
        return t

    def mod_row(self, i, tile):
        p_tiles = self.p_rows // tile
        per = self.dec_seq // tile
        return jnp.where(i < p_tiles, 0, 1 + (i - p_tiles) // per)

    def seq_pos(self, n, tile):
        p_tiles = self.p_rows // tile
        tpp = self.seq // tile
        tps = self.dec_seq // tile
        is_p = n < p_tiles
        seq_idx = jnp.where(is_p, n // tpp, self.batch + (n - p_tiles) // tps)
        pos = jnp.where(is_p, n % tpp, (n - p_tiles) % tps)
        count = jnp.where(is_p, tpp, tps)
        return seq_idx, pos, count


def _mod_kernel(c_ref, w_ref, b_ref, o_ref):
    s = _silu(c_ref[...]).astype(BF16)
    o_ref[...] = _dot(s, w_ref[...].astype(BF16)) + b_ref[...]


def _modulation(cvec, ada_w, ada_b):
    depth, d, n = ada_w.shape
    tn = 1024
    return pl.pallas_call(
        _mod_kernel,
        grid=(depth, n // tn),
        in_specs=[
            pl.BlockSpec((MOD_ROWS, d), lambda l, j: (0, 0)),
            pl.BlockSpec((None, d, tn), lambda l, j: (l, 0, j)),
            pl.BlockSpec((None, 1, tn), lambda l, j: (l, 0, j)),
        ],
        out_specs=pl.BlockSpec((None, MOD_ROWS, tn), lambda l, j: (l, 0, j)),
        out_shape=jax.ShapeDtypeStruct((depth, MOD_ROWS, n), F32),
        compiler_params=_cparams("parallel", "parallel"),
        name="modulation",
    )(cvec, ada_w, ada_b.reshape(depth, 1, n))


NORM_ROWS = 16


def _modulated_norm_rows(x_ref, g_ref, mod_ref, shift_row, scale_row, h_ref, start, n_rows):
    for r0 in range(0, n_rows, NORM_ROWS):
        rows = pl.ds(start + r0, NORM_ROWS)
        x = x_ref[rows, :]
        y = x * lax.rsqrt(jnp.mean(x * x, axis=-1, keepdims=True) + EPS) * g_ref[...]
        h = y * (1.0 + mod_ref[scale_row:scale_row + 1, :]) + mod_ref[shift_row:shift_row + 1, :]
        h_ref[rows, :] = h.astype(BF16)


def _modulated_norm(x_ref, g_ref, mod_ref, shift_row, scale_row, h_ref):
    block = 8 * NORM_ROWS

    def body(c, carry):
        start = pl.multiple_of(c * block, block)
        _modulated_norm_rows(x_ref, g_ref, mod_ref, shift_row, scale_row, h_ref, start, block)
        return carry

    lax.fori_loop(0, x_ref.shape[0] // block, body, 0)


def _split_rows_specs(lay, tm, width, tile_of):
    p_tiles = lay.p_rows // tm
    s_tiles = lay.s_rows // tm
    return [pl.BlockSpec((tm, width), lambda *g: (jnp.minimum(tile_of(*g), p_tiles - 1), 0)),
            pl.BlockSpec((tm, width), lambda *g: (jnp.clip(tile_of(*g) - p_tiles, 0, s_tiles - 1), 0))]


def _inproj_kernel(*refs, p_tiles):
    x_refs, (mod_ref, g_ref, w_ref, z_ref, h_ref) = refs[:-5], refs[-5:]
    i = pl.program_id(0)
    first_step = pl.program_id(1) == 0
    for x_ref, mine in zip(x_refs, (True,) if p_tiles is None else (i < p_tiles, i >= p_tiles)):
        @pl.when(first_step & mine)
        def _():
            _modulated_norm(x_ref, g_ref, mod_ref, MOD_SHIFT1, MOD_SCALE1, h_ref)

    z_ref[...] = _dot(h_ref[...], w_ref[...])


def _in_projection(lay, l, x, mods, norm_g, w_in):
    pair = isinstance(x, tuple)
    m, d = lay.rows, w_in.shape[1]
    n = w_in.shape[-1]
    tm = lay.row_tile(1024)
    tn = 512
    x_specs = (_split_rows_specs(lay, tm, d, lambda i, j: i) if pair
               else [pl.BlockSpec((tm, d), lambda i, j: (i, 0))])
    return pl.pallas_call(
        functools.partial(_inproj_kernel, p_tiles=lay.p_rows // tm if pair else None),
        grid=(m // tm, n // tn),
        in_specs=x_specs + [
            pl.BlockSpec((None, None, MOD_ROWS, d), lambda i, j: (l, lay.mod_row(i, tm), 0, 0)),
            pl.BlockSpec((None, 1, d), lambda i, j: (l, 0, 0)),
            pl.BlockSpec((None, d, tn), lambda i, j: (l, 0, j)),
        ],
        out_specs=[
            pl.BlockSpec((tm, tn), lambda i, j: (i, j)),
            pl.BlockSpec((tm, d), lambda i, j: (i, 0)),
        ],
        out_shape=[jax.ShapeDtypeStruct((m, n), F32), jax.ShapeDtypeStruct((m, d), BF16)],
        compiler_params=_cparams("parallel", "arbitrary"),
        name="in_projection",
    )(*(x if pair else (x,)), mods, norm_g.reshape(norm_g.shape[0], 1, d), w_in)


def _residual_matmul_kernel(a_ref, w_ref, x_ref, mod_ref, o_ref, *, gate_row):
    a = a_ref[...]
    chunk = 512
    for c0 in range(0, o_ref.shape[1], chunk):
        sl = slice(c0, c0 + chunk)
        o_ref[:, sl] = x_ref[:, sl] + mod_ref[gate_row:gate_row + 1, sl] * _dot(a, w_ref[:, sl])


def _residual_matmul(lay, l, a, w, x, mods, gate_row, name, tm, tn, rows=None):
    m, k = a.shape
    d = w.shape[-1]
    tm = lay.row_tile(tm)
    first, stop = (0, m) if rows is None else rows
    assert first % tm == 0 and stop % tm == 0
    t0 = first // tm
    return pl.pallas_call(
        functools.partial(_residual_matmul_kernel, gate_row=gate_row),
        grid=((stop - first) // tm, d // tn),
        in_specs=[
            pl.BlockSpec((tm, k), lambda i, j: (t0 + i, 0)),
            pl.BlockSpec((None, k, tn), lambda i, j: (l, 0, j)),
            pl.BlockSpec((tm, tn), lambda i, j: (t0 + i, j)),
            pl.BlockSpec((None, None, MOD_ROWS, tn), lambda i, j: (l, lay.mod_row(t0 + i, tm), 0, j)),
        ],
        out_specs=pl.BlockSpec((tm, tn), lambda i, j: (i, j)),
        out_shape=jax.ShapeDtypeStruct((stop - first, d), F32),
        compiler_params=_cparams("parallel", "arbitrary"),
        name=name,
    )(a, w, x, mods)


def _out_proj_kernel(a_ref, w_ref, *refs, p_tiles):
    x_refs, (mod_ref, g_ref, o_ref, h_ref) = refs[:-4], refs[-4:]
    i = pl.program_id(0)
    a = a_ref[...]
    chunk = 512
    for c0 in range(0, o_ref.shape[1], chunk):
        sl = slice(c0, c0 + chunk)
        update = mod_ref[MOD_GATE1:MOD_GATE1 + 1, sl] * _dot(a, w_ref[:, sl])
        if p_tiles is None:
            o_ref[:, sl] = x_refs[0][:, sl] + update
        else:
            o_ref[:, sl] = jnp.where(i < p_tiles, x_refs[0][:, sl], x_refs[1][:, sl]) + update
    _modulated_norm(o_ref, g_ref, mod_ref, MOD_SHIFT2, MOD_SCALE2, h_ref)


def _out_projection(lay, l, a, w_o, x, mods, norm2_g):
    pair = isinstance(x, tuple)
    m, k = a.shape
    d = w_o.shape[-1]
    tm = lay.row_tile(512)
    x_specs = _split_rows_specs(lay, tm, d, lambda i: i) if pair else [pl.BlockSpec((tm, d), lambda i: (i, 0))]
    return pl.pallas_call(
        functools.partial(_out_proj_kernel, p_tiles=lay.p_rows // tm if pair else None),
        grid=(m // tm,),
        in_specs=[
            pl.BlockSpec((tm, k), lambda i: (i, 0)),
            pl.BlockSpec((None, k, d), lambda i: (l, 0, 0)),
        ] + x_specs + [
            pl.BlockSpec((None, None, MOD_ROWS, d), lambda i: (l, lay.mod_row(i, tm), 0, 0)),
            pl.BlockSpec((None, 1, d), lambda i: (l, 0, 0)),
        ],
        out_specs=[pl.BlockSpec((tm, d), lambda i: (i, 0)), pl.BlockSpec((tm, d), lambda i: (i, 0))],
        out_shape=[jax.ShapeDtypeStruct((m, d), F32), jax.ShapeDtypeStruct((m, d), BF16)],
        compiler_params=_cparams("parallel"),
        name="out_projection",
    )(a, w_o, *(x if pair else (x,)), mods, norm2_g.reshape(norm2_g.shape[0], 1, d))


def _ffn_up_kernel(h_ref, w1_ref, w3_ref, o_ref):
    h = h_ref[...]
    tn = o_ref.shape[1]
    tail = tn % MXU_WIDTH
    c0 = 0
    while c0 < tn - tail:
        c1 = min(c0 + 512, tn - tail)
        o_ref[:, c0:c1] = (_silu(_dot(h, w1_ref[:, c0:c1])) * _dot(h, w3_ref[:, c0:c1])).astype(BF16)
        c0 = c1
    if tail:
        both = _dot(h, jnp.concatenate([w1_ref[:, c0:], w3_ref[:, c0:]], axis=1))
        o_ref[:, c0:] = (_silu(both[:, :tail]) * both[:, tail:]).astype(BF16)


def _ffn_up(lay, l, h, w1, w3):
    m, d = h.shape
    n = w1.shape[-1]
    tm = lay.row_tile(1024)
    tn = n // 4
    assert n % 4 == 0 and tn % LANES == 0
    return pl.pallas_call(
        _ffn_up_kernel,
        grid=(m // tm, n // tn),
        in_specs=[
            pl.BlockSpec((tm, d), lambda i, j: (i, 0)),
            pl.BlockSpec((None, d, tn), lambda i, j: (l, 0, j)),
            pl.BlockSpec((None, d, tn), lambda i, j: (l, 0, j)),
        ],
        out_specs=pl.BlockSpec((tm, tn), lambda i, j: (i, j)),
        out_shape=jax.ShapeDtypeStruct((m, n), BF16),
        compiler_params=_cparams("parallel", "parallel"),
        name="ffn_up",
    )(h, w1, w3)


def _qk_prep_kernel(q_ref, k_ref, cos_ref, sa_ref, sb_ref, qg_ref, kg_ref, qo_ref, ko_ref, kf_ref):
    cos, sin_a, sin_b = cos_ref[...], sa_ref[...], sb_ref[...]

    ones = jnp.ones((HEAD_DIM, HEAD_DIM), BF16)

    def norm(x, g):
        mean_sq = _dot((x * x).astype(BF16), ones) * (1.0 / HEAD_DIM)
        return x * lax.rsqrt(mean_sq + EPS) * g

    def rope(x):
        return (x * cos + pltpu.roll(x, HEAD_DIM - ROT_AXIS // 2, 1) * sin_a
                + pltpu.roll(x, ROT_AXIS // 2, 1) * sin_b)

    for h in range(N_HEADS_A):
        sl = slice(h * HEAD_DIM, (h + 1) * HEAD_DIM)
        qo_ref[:, sl] = (rope(norm(q_ref[:, sl], qg_ref[...])) * ATTN_SCALE).astype(BF16)
    for h in range(N_KV_A):
        sl = slice(h * HEAD_DIM, (h + 1) * HEAD_DIM)
        kn = norm(k_ref[:, sl], kg_ref[...])
        kf_ref[:, sl] = kn
        ko_ref[:, sl] = rope(kn).astype(BF16)


def _qk_prep(lay, l, z, rope_tabs, q_norm_g, k_norm_g):
    m = z.shape[0]
    tm = lay.row_tile(512)
    cos, sin_a, sin_b = rope_tabs
    tab_spec = pl.BlockSpec((tm, HEAD_DIM), lambda i: (i, 0))
    g_spec = pl.BlockSpec((None, 1, HEAD_DIM), lambda i: (l, 0, 0))
    return pl.pallas_call(
        _qk_prep_kernel,
        grid=(m // tm,),
        in_specs=[
            pl.BlockSpec((tm, A_WIDTH), lambda i: (i, 0)),
            pl.BlockSpec((tm, KV_WIDTH), lambda i: (i, KV_COL_K)),
            tab_spec, tab_spec, tab_spec, g_spec, g_spec,
        ],
        out_specs=[
            pl.BlockSpec((tm, A_WIDTH), lambda i: (i, 0)),
            pl.BlockSpec((tm, KV_WIDTH), lambda i: (i, 0)),
            pl.BlockSpec((tm, KV_WIDTH), lambda i: (i, 0)),
        ],
        out_shape=[
            jax.ShapeDtypeStruct((m, A_WIDTH), BF16),
            jax.ShapeDtypeStruct((m, KV_WIDTH), BF16),
            jax.ShapeDtypeStruct((m, KV_WIDTH), F32),
        ],
        compiler_params=_cparams("parallel"),
        name="qk_prep",
    )(z, z, cos, sin_a, sin_b, q_norm_g.reshape(-1, 1, HEAD_DIM), k_norm_g.reshape(-1, 1, HEAD_DIM))


def _stack_heads(q_ref, kv):
    return jnp.concatenate(
        [q_ref[:, (kv * GROUP_A + g) * HEAD_DIM:(kv * GROUP_A + g + 1) * HEAD_DIM] for g in range(GROUP_A)], axis=0)


def _fold_lane_tiles(x, op):
    out = x[:, :LANES]
    for c0 in range(LANES, x.shape[1], LANES):
        out = op(out, x[:, c0:c0 + LANES])
    return out


def _sink_column(sink_ref, kv, rows):
    return jnp.concatenate(
        [jnp.full((rows, 1), sink_ref[kv * GROUP_A + g], F32) for g in range(GROUP_A)], axis=0)


def _context_attention_kernel(sink_ref, q_ref, k_ref, v_ref, o_ref):
    rows = q_ref.shape[0]
    for kv in range(N_KV_A):
        sl = slice(kv * HEAD_DIM, (kv + 1) * HEAD_DIM)
        q = _stack_heads(q_ref, kv)
        s = _dot_nt(q, k_ref[:, sl])
        sink = _sink_column(sink_ref, kv, rows)
        mx = jnp.maximum(jnp.max(_fold_lane_tiles(s, jnp.maximum), axis=-1, keepdims=True), sink)
        p = jnp.exp(s - mx)
        den = jnp.sum(_fold_lane_tiles(p, jnp.add), axis=-1, keepdims=True) + jnp.exp(sink - mx)
        o = _dot(p.astype(BF16), v_ref[:, sl].astype(BF16)) * (1.0 / den)
        for g in range(GROUP_A):
            h = kv * GROUP_A + g
            o_ref[:, h * HEAD_DIM:(h + 1) * HEAD_DIM] = o[g * rows:(g + 1) * rows].astype(BF16)


def _context_attention(lay, l, qn, kn, z, sink):
    seq = lay.seq
    return pl.pallas_call(
        _context_attention_kernel,
        grid=(lay.batch,),
        in_specs=[
            pl.BlockSpec(memory_space=pltpu.SMEM),
            pl.BlockSpec((seq, A_WIDTH), lambda b: (b, 0)),
            pl.BlockSpec((seq, KV_WIDTH), lambda b: (b, 0)),
            pl.BlockSpec((seq, KV_WIDTH), lambda b: (b, KV_COL_V)),
        ],
        out_specs=pl.BlockSpec((seq, A_WIDTH), lambda b: (b, 0)),
        out_shape=jax.ShapeDtypeStruct((lay.p_rows, A_WIDTH), BF16),
        compiler_params=_cparams("parallel"),
        name="context_attention",
    )(sink, qn, kn, z)


def _latent_attention_kernel(sink_ref, q_ref, kp_ref, kc_ref, kn_ref, vp_ref, vc_ref, vn_ref, ck_ref, cv_ref,
                             o_ref, *, seq_len):
    assert WINDOW >= BLOCK - 1
    j = pl.program_id(1)
    n_blocks = seq_len // BLOCK
    r = lax.broadcasted_iota(jnp.int32, (BLOCK, BLOCK), 0)
    c = lax.broadcasted_iota(jnp.int32, (BLOCK, BLOCK), 1)
    valid_prev = jnp.concatenate([(r - (c - BLOCK) <= WINDOW) & (j >= 1)] * GROUP_A, axis=0)
    valid_next = jnp.concatenate([((c + BLOCK) - r <= WINDOW) & (j <= n_blocks - 2)] * GROUP_A, axis=0)
    for kv in range(N_KV_A):
        sl = slice(kv * HEAD_DIM, (kv + 1) * HEAD_DIM)
        q = _stack_heads(q_ref, kv)
        s_p = jnp.where(valid_prev, _dot_nt(q, kp_ref[:, sl]), NEG_MASK)
        s_o = _dot_nt(q, kc_ref[:, sl])
        s_n = jnp.where(valid_next, _dot_nt(q, kn_ref[:, sl]), NEG_MASK)
        s_c = _dot_nt(q, ck_ref[:, sl].astype(BF16))
        sink = _sink_column(sink_ref, kv, BLOCK)
        mx = jnp.maximum(jnp.maximum(s_p, s_o), jnp.maximum(s_n, _fold_lane_tiles(s_c, jnp.maximum)))
        mx = jnp.maximum(jnp.max(mx, axis=-1, keepdims=True), sink)
        psum = None
        o = None
        for s, v_ref in ((s_p, vp_ref), (s_o, vc_ref), (s_n, vn_ref), (s_c, cv_ref)):
            p = jnp.exp(s - mx)
            pf = _fold_lane_tiles(p, jnp.add)
            psum = pf if psum is None else psum + pf
            term = _dot(p.astype(BF16), v_ref[:, sl].astype(BF16))
            o = term if o is None else o + term
        den = jnp.sum(psum, axis=-1, keepdims=True) + jnp.exp(sink - mx)
        o = o * (1.0 / den)
        for g in range(GROUP_A):
            h = kv * GROUP_A + g
            o_ref[:, h * HEAD_DIM:(h + 1) * HEAD_DIM] = o[g * BLOCK:(g + 1) * BLOCK].astype(BF16)


def _latent_attention(lay, l, qn, kn, z, ctx_k, ctx_v, sink):
    nb = lay.dec_seq // BLOCK
    base = lay.p_rows // BLOCK
    past = ctx_k.shape[2]

    def blk(shift, col):
        return lambda b, j: (base + b * nb + jnp.clip(j + shift, 0, nb - 1), col)

    k_spec = [pl.BlockSpec((BLOCK, KV_WIDTH), blk(s, 0)) for s in (-1, 0, 1)]
    v_spec = [pl.BlockSpec((BLOCK, KV_WIDTH), blk(s, KV_COL_V)) for s in (-1, 0, 1)]
    ctx_spec = pl.BlockSpec((None, None, past, KV_WIDTH), lambda b, j: (b, l, 0, 0))
    return pl.pallas_call(
        functools.partial(_latent_attention_kernel, seq_len=lay.dec_seq),
        grid=(lay.dec_batch, nb),
        in_specs=[pl.BlockSpec(memory_space=pltpu.SMEM), pl.BlockSpec((BLOCK, A_WIDTH), blk(0, 0))]
        + k_spec + v_spec + [ctx_spec, ctx_spec],
        out_specs=pl.BlockSpec((BLOCK, A_WIDTH), lambda b, j: (b * nb + j, 0)),
        out_shape=jax.ShapeDtypeStruct((lay.s_rows, A_WIDTH), BF16),
        compiler_params=_cparams("parallel", "parallel"),
        name="latent_attention",
    )(sink, qn, kn, kn, kn, z, z, z, ctx_k, ctx_v)


def _conv_kernel(a_c, a_p, a_n, g_c, g_p, g_n, b_c, c_c, c_p, c_n, h_c, h_p, h_n,
                 dw_ref, db_ref, lg_ref, lb_ref, sw_ref, ub_ref, uc_ref, xb_s, xc_s, *, lay):
    _, pos, count = lay.seq_pos(pl.program_id(0), SEQ_TILE)
    has_prev = pos > 0
    has_next = pos < count - 1
    rb = 32
    xb_s[0, 0:HALO, :] = jnp.where(has_prev, a_p[...] * _sigmoid(g_p[...]), 0.0)
    xb_s[0, HALO + SEQ_TILE:, :] = jnp.where(has_next, a_n[...] * _sigmoid(g_n[...]), 0.0)
    xc_s[0:HALO, :] = jnp.where(has_prev, c_p[...] * h_p[...], 0.0)
    xc_s[HALO + SEQ_TILE:, :] = jnp.where(has_next, c_n[...] * h_n[...], 0.0)
    for r0 in range(0, SEQ_TILE, rb):
        rows = slice(r0, r0 + rb)
        xb_s[0, HALO + r0:HALO + r0 + rb, :] = a_c[rows, :] * _sigmoid(g_c[rows, :])
        xc_s[HALO + r0:HALO + r0 + rb, :] = c_c[rows, :] * h_c[rows, :]
    shifted_rows = SEQ_TILE + 2 * HALO - SUBLANES
    chunk = 40
    for s in range(1, SUBLANES):
        for r0 in range(0, shifted_rows, chunk):
            xb_s[s, r0:r0 + chunk, :] = xb_s[0, r0 + s:r0 + s + chunk, :]
    for r0 in range(0, SEQ_TILE, rb):
        acc = jnp.zeros((rb // SUBLANES, SUBLANES, CONF_W), F32)
        for k in range(CONF_K):
            off = HALO - CONF_K // 2 + k
            row = r0 + off - off % SUBLANES
            x = xb_s[off % SUBLANES, row:row + rb, :]
            acc = acc + dw_ref[k] * x.reshape(rb // SUBLANES, SUBLANES, CONF_W)
        u = acc.reshape(rb, CONF_W) + db_ref[...]
        uc = u - jnp.mean(u, axis=-1, keepdims=True)
        var = jnp.mean(uc * uc, axis=-1, keepdims=True)
        u = uc * lax.rsqrt(var + EPS) * lg_ref[...] + lb_ref[...]
        ub_ref[r0:r0 + rb, :] = _silu(u).astype(BF16)
        acc = jnp.zeros((rb // SUBLANES, SUBLANES, SC_W), F32)
        for k in range(SC_K):
            off = HALO - SC_K // 2 + r0 + k
            acc = acc + sw_ref[k] * xc_s[off:off + rb, :].reshape(rb // SUBLANES, SUBLANES, SC_W)
        uc_ref[r0:r0 + rb, :] = (b_c[r0:r0 + rb, :] * acc.reshape(rb, SC_W)).astype(BF16)


def _conv_mixers(lay, l, z, conf_dw_w, conf_dw_b, conf_ln_g, conf_ln_b, sc_conv_w):
    m = z.shape[0]
    n_tiles = m // SEQ_TILE
    per = SEQ_TILE // HALO
    n_halo = m // HALO

    def cur(col):
        return pl.BlockSpec((SEQ_TILE, COL_W), lambda i: (i, col))

    def prev(col):
        return pl.BlockSpec((HALO, COL_W), lambda i: (jnp.maximum(i * per - 1, 0), col))

    def nxt(col):
        return pl.BlockSpec((HALO, COL_W), lambda i: (jnp.minimum((i + 1) * per, n_halo - 1), col))

    def trio(col):
        return [cur(col), prev(col), nxt(col)]

    def par(rows):
        return pl.BlockSpec((None, rows, COL_W), lambda i: (l, 0, 0))

    def taps(n):
        return pl.BlockSpec((None, n, SUBLANES, COL_W), lambda i: (l, 0, 0, 0))

    depth = conf_dw_w.shape[0]
    dw = jnp.broadcast_to(conf_dw_w[:, :, None, :], (depth, CONF_K, SUBLANES, CONF_W))
    sw = jnp.broadcast_to(sc_conv_w[:, :, None, :], (depth, SC_K, SUBLANES, SC_W))
    return pl.pallas_call(
        functools.partial(_conv_kernel, lay=lay),
        grid=(n_tiles,),
        in_specs=trio(COL_CONF_A) + trio(COL_CONF_G) + [cur(COL_SC_B)] + trio(COL_SC_C) + trio(COL_SC_H)
        + [taps(CONF_K), par(1), par(1), par(1), taps(SC_K)],
        out_specs=[pl.BlockSpec((SEQ_TILE, CONF_W), lambda i: (i, 0)),
                   pl.BlockSpec((SEQ_TILE, SC_W), lambda i: (i, 0))],
        out_shape=[jax.ShapeDtypeStruct((m, CONF_W), BF16), jax.ShapeDtypeStruct((m, SC_W), BF16)],
        scratch_shapes=[pltpu.VMEM((SUBLANES, SEQ_TILE + 2 * HALO, CONF_W), F32),
                        pltpu.VMEM((SEQ_TILE + 2 * HALO, SC_W), F32)],
        compiler_params=_cparams("parallel"),
        name="conv_mixers",
    )(*([z] * 13), dw, conf_dw_b.reshape(depth, 1, CONF_W), conf_ln_g.reshape(depth, 1, CONF_W),
      conf_ln_b.reshape(depth, 1, CONF_W), sw)


def _hgrn_gates(hq_ref, hf_ref, lb_ref, sl, tri_b3):
    zq = hq_ref[:, sl]
    q = _silu(zq) * (HG_DK ** -0.5)
    zf = hf_ref[:, sl]
    lb = lb_ref[:, sl]
    e = jnp.exp(-jnp.abs(zf))
    r = 1.0 / (1.0 + e)
    nonneg = zf >= 0
    sig_pos = jnp.where(nonneg, r, e * r)
    sig_neg = jnp.where(nonneg, e * r, r)
    logf = jnp.log(lb + (1.0 - lb) * sig_pos)
    kk = (1.0 - lb) * sig_neg
    g1 = logf.astype(BF16)
    r1 = logf - g1.astype(F32)
    g2 = r1.astype(BF16)
    g3 = (r1 - g2.astype(F32)).astype(BF16)
    b = _dot(tri_b3, jnp.concatenate([g1, g2, g3], axis=0))
    return q, kk, b


def _block_place(x, n_chunks):
    w = x.shape[1]
    zeros = jnp.zeros((HG_CHUNK, w), x.dtype)
    row_blocks = []
    for c in range(n_chunks):
        cols = [zeros] * n_chunks
        cols[c] = x[c * HG_CHUNK:(c + 1) * HG_CHUNK, :]
        row_blocks.append(jnp.concatenate(cols, axis=1))
    return jnp.concatenate(row_blocks, axis=0)


def _hgrn_exact_scores(a_ref, q_s, k_s, b_s, reverse):
    ts = a_ref.shape[0]
    a_ref[...] = jnp.zeros_like(a_ref)

    def body(s, carry):
        c0 = pl.multiple_of((s // HG_CHUNK) * HG_CHUNK, HG_CHUNK)
        w = jnp.exp(jnp.minimum(b_s[pl.ds(c0, HG_CHUNK), :] - b_s[pl.ds(s, 1), :], 0.0))
        prod = q_s[pl.ds(c0, HG_CHUNK), :] * k_s[pl.ds(s, 1), :] * w
        trow = c0 + lax.broadcasted_iota(jnp.int32, (HG_CHUNK, 1), 0)
        ok = (trow <= s) if reverse else (trow >= s)
        onehot = (lax.broadcasted_iota(jnp.int32, (1, ts), 1) == s).astype(F32)
        colv = jnp.sum(prod, axis=-1, keepdims=True)
        a_ref[pl.ds(c0, HG_CHUNK), :] += jnp.where(ok, colv, 0.0) * onehot
        return carry

    lax.fori_loop(0, ts, body, 0)


def _hgrn_kernel(hq_ref, hf_ref, hi_ref, lb_ref, s0_ref, o_ref, sf_ref, st_ref, oi_s, a_ref, q_s, k_s, b_s,
                 *, lay, reverse):
    ts = SEQ_TILE
    n_chunks = ts // HG_CHUNK
    _, pos, count = lay.seq_pos(pl.program_id(0), ts)

    @pl.when(pos == 0)
    def _():
        st_ref[...] = s0_ref[...]

    row = lax.broadcasted_iota(jnp.int32, (ts, ts), 0)
    col = lax.broadcasted_iota(jnp.int32, (ts, ts), 1)
    same = _chunk_of(row, HG_CHUNK) == _chunk_of(col, HG_CHUNK)
    tri = same & ((col >= row) if reverse else (col <= row))
    tri_b = tri.astype(F32).astype(BF16)
    tri_b3 = jnp.concatenate([tri_b] * 3, axis=1)
    order = range(n_chunks - 1, -1, -1) if reverse else range(n_chunks)
    last = 0 if reverse else HG_CHUNK - 1

    b_min = None
    for hd in range(HG_HEADS):
        sl = slice(hd * HG_DK, (hd + 1) * HG_DK)
        q, kk, b = _hgrn_gates(hq_ref, hf_ref, lb_ref, sl, tri_b3)
        head_min = jnp.min(b)
        b_min = head_min if b_min is None else jnp.minimum(b_min, head_min)
        b3 = b.reshape(n_chunks, HG_CHUNK, HG_DK)
        total3 = b3[:, last:last + 1, :]
        qt = (q * jnp.exp(b)).astype(BF16)
        kend = (kk * jnp.exp(jnp.broadcast_to(total3, b3.shape).reshape(ts, HG_DK) - b)).astype(BF16)
        decay = jnp.exp(total3.reshape(n_chunks, HG_DK))
        v_h = hi_ref[:, sl]
        kinv = (kk * jnp.exp(-b)).astype(BF16)
        scores = jnp.where(tri, _dot_nt(qt, kinv), 0.0).astype(BF16)
        o_intra = _dot(scores, v_h.astype(BF16))
        incr = _dot(v_h.T.astype(BF16), _block_place(kend, n_chunks))
        state = st_ref[hd]
        starts = [None] * n_chunks
        for c in order:
            starts[c] = state
            state = state * decay[c:c + 1, :] + incr[:, c * HG_DK:(c + 1) * HG_DK]
        st_ref[hd] = state
        o_inter = jnp.concatenate(
            [_dot_nt(qt[c * HG_CHUNK:(c + 1) * HG_CHUNK, :], starts[c].astype(BF16)) for c in range(n_chunks)],
            axis=0)
        oi_s[:, sl] = o_inter
        o_ref[:, sl] = o_intra + o_inter

    @pl.when(b_min <= HG_SAFE_LOG_DECAY)
    def _():
        for hd in range(HG_HEADS):
            sl = slice(hd * HG_DK, (hd + 1) * HG_DK)
            q, kk, b = _hgrn_gates(hq_ref, hf_ref, lb_ref, sl, tri_b3)
            q_s[...] = q
            k_s[...] = kk
            b_s[...] = b
            _hgrn_exact_scores(a_ref, q_s, k_s, b_s, reverse)
            o_ref[:, sl] = _dot(a_ref[...].astype(BF16), hi_ref[:, sl].astype(BF16)) + oi_s[:, sl]

    @pl.when(pos == count - 1)
    def _():
        sf_ref[...] = st_ref[...]


def _hgrn_direction(lay, z, lb, s0_t, reverse):
    m = z.shape[0]
    n_tiles = m // SEQ_TILE

    def tile_of(n):
        seq_idx, pos, count = lay.seq_pos(n, SEQ_TILE)
        return (n - pos + (count - 1 - pos)) if reverse else n

    def zcol(col):
        return pl.BlockSpec((SEQ_TILE, COL_W), lambda n: (tile_of(n), col))

    state_spec = pl.BlockSpec((None, HG_HEADS, HG_DV, HG_DK), lambda n: (lay.seq_pos(n, SEQ_TILE)[0], 0, 0, 0))
    return pl.pallas_call(
        functools.partial(_hgrn_kernel, lay=lay, reverse=reverse),
        grid=(n_tiles,),
        in_specs=[zcol(COL_HQ), zcol(COL_HF_BWD if reverse else COL_HF_FWD), zcol(COL_HI),
                  pl.BlockSpec((1, HG_W), lambda n: (0, 0)), state_spec],
        out_specs=[pl.BlockSpec((SEQ_TILE, HG_W), lambda n: (tile_of(n), 0)), state_spec],
        out_shape=[jax.ShapeDtypeStruct((m, HG_W), F32),
                   jax.ShapeDtypeStruct((lay.n_seq, HG_HEADS, HG_DV, HG_DK), F32)],
        scratch_shapes=[pltpu.VMEM((HG_HEADS, HG_DV, HG_DK), F32),
                        pltpu.VMEM((SEQ_TILE, HG_W), F32),
                        pltpu.VMEM((SEQ_TILE, SEQ_TILE), F32),
                        pltpu.VMEM((SEQ_TILE, HG_DK), F32),
                        pltpu.VMEM((SEQ_TILE, HG_DK), F32),
                        pltpu.VMEM((SEQ_TILE, HG_DK), F32)],
        compiler_params=_cparams("arbitrary"),
        name="hgrn_bwd" if reverse else "hgrn_fwd",
    )(z, z, z, lb, s0_t)


def _hgrn_out_kernel(of_ref, ob_ref, hg_ref, g_ref, o_ref):
    for hd in range(HG_HEADS):
        sl = slice(hd * HG_DV, (hd + 1) * HG_DV)
        o = of_ref[:, sl] + ob_ref[:, sl]
        o = o * lax.rsqrt(jnp.mean(o * o, axis=-1, keepdims=True) + EPS) * g_ref[...]
        o_ref[:, sl] = (o * _silu(hg_ref[:, sl])).astype(BF16)


def _hgrn_output(lay, l, o_f, o_b, z, hg_norm_g):
    m = z.shape[0]
    tm = lay.row_tile(512)
    row_spec = pl.BlockSpec((tm, HG_W), lambda i: (i, 0))
    return pl.pallas_call(
        _hgrn_out_kernel,
        grid=(m // tm,),
        in_specs=[row_spec, row_spec, pl.BlockSpec((tm, COL_W), lambda i: (i, COL_HG)),
                  pl.BlockSpec((None, 1, HG_DV), lambda i: (l, 0, 0))],
        out_specs=row_spec,
        out_shape=jax.ShapeDtypeStruct((m, HG_W), BF16),
        compiler_params=_cparams("parallel"),
        name="hgrn_output",
    )(o_f, o_b, z, hg_norm_g.reshape(-1, 1, HG_DV))


def _merge_kernel(h_ref, ap_ref, as_ref, b_ref, c_ref, d_ref, wg0, wg1, wg2, wg3, bg0, bg1, bg2, bg3,
                  wa_ref, wb_ref, wc_ref, wd_ref, o_ref, *, p_tiles):
    h = h_ref[...]
    attn = jnp.where(pl.program_id(1) < p_tiles, ap_ref[...], as_ref[...])
    acc = None
    for branch, w_ref, wg_ref, bg_ref in ((attn, wa_ref, wg0, bg0), (b_ref[...], wb_ref, wg1, bg1),
                                          (c_ref[...], wc_ref, wg2, bg2), (d_ref[...], wd_ref, wg3, bg3)):
        gate = _sigmoid(_dot(h, wg_ref[...]) + bg_ref[...])
        term = gate * _dot(branch, w_ref[...])
        acc = term if acc is None else acc + term
    o_ref[...] = acc.astype(BF16)


def _merge(lay, l, h, attn_p, attn_s, branches, w_gate, b_gate, w_outs):
    m, d = h.shape
    tm = lay.row_tile(512)
    tn = 512
    nj = d // tn
    depth = w_gate.shape[0]
    p_tiles = lay.p_rows // tm
    s_tiles = lay.s_rows // tm

    def gate_w(n):
        return pl.BlockSpec((None, d, tn), lambda j, i: (l, 0, n * nj + j))

    def gate_b(n):
        return pl.BlockSpec((None, 1, tn), lambda j, i: (l, 0, n * nj + j))

    b_gate3 = b_gate.reshape(depth, 1, N_BRANCH * d)
    return pl.pallas_call(
        functools.partial(_merge_kernel, p_tiles=p_tiles),
        grid=(nj, m // tm),
        in_specs=[pl.BlockSpec((tm, d), lambda j, i: (i, 0)),
                  pl.BlockSpec((tm, A_WIDTH), lambda j, i: (jnp.minimum(i, p_tiles - 1), 0)),
                  pl.BlockSpec((tm, A_WIDTH), lambda j, i: (jnp.clip(i - p_tiles, 0, s_tiles - 1), 0))]
        + [pl.BlockSpec((tm, br.shape[1]), lambda j, i: (i, 0)) for br in branches]
        + [gate_w(n) for n in range(N_BRANCH)] + [gate_b(n) for n in range(N_BRANCH)]
        + [pl.BlockSpec((None, w.shape[1], tn), lambda j, i: (l, 0, j)) for w in w_outs],
        out_specs=pl.BlockSpec((tm, tn), lambda j, i: (i, j)),
        out_shape=jax.ShapeDtypeStruct((m, d), BF16),
        compiler_params=_cparams("parallel", "parallel"),
        name="merge",
    )(h, attn_p, attn_s, *branches, *([w_gate] * N_BRANCH), *([b_gate3] * N_BRANCH), *w_outs)


def _rope_tables(lay):
    t = lay.dec_seq
    half = ROT_AXIS // 2
    rows = t // GRID_W
    row = jnp.repeat(jnp.arange(rows, dtype=F32), GRID_W)
    col = jnp.tile(jnp.arange(GRID_W, dtype=F32), rows)
    inv = ROPE_BASE ** (-jnp.arange(0, ROT_AXIS, 2, dtype=F32) / ROT_AXIS)
    ang = jnp.concatenate([row[:, None] * inv, row[:, None] * inv, col[:, None] * inv, col[:, None] * inv], axis=1)
    lane = jnp.arange(HEAD_DIM)
    first_half = (lane % ROT_AXIS) < half
    cos = jnp.cos(ang)
    sin = jnp.sin(ang)
    sin_a = jnp.where(first_half, -sin, 0.0)
    sin_b = jnp.where(first_half, 0.0, sin)

    def full(tab, fill):
        return jnp.concatenate([jnp.full((lay.p_rows, HEAD_DIM), fill, F32)] + [tab] * lay.dec_batch, axis=0)

    return full(cos, 1.0), full(sin_a, 0.0), full(sin_b, 0.0)


def kernel(x_prompt, x_sample, cache_k, cache_v, state_hgrn_fwd, state_hgrn_bwd, c, c_ctx, ada_w, ada_b, norm1_g, norm2_g, w_in, q_norm_g, k_norm_g, attn_sink, w_attn_out, conf_dw_w, conf_dw_b, conf_ln_g, conf_ln_b, w_conf_out, sc_conv_w, w_sc_out, hg_lb, hg_norm_g, w_hg_out, w_gate, b_gate, w_o, ffn_w1, ffn_w3, ffn_w2):
    batch, seq, d = x_prompt.shape
    dec_batch, dec_seq, _ = x_sample.shape
    depth = ada_w.shape[0]
    past = cache_k.shape[2]
    lay = _Layout(batch, seq, dec_batch, dec_seq)
    assert seq % SEQ_TILE == 0 and dec_seq % SEQ_TILE == 0 and dec_seq % GRID_W == 0
    assert 1 + dec_batch <= MOD_ROWS

    wb = {name: w.astype(BF16) for name, w in dict(
        w_in=w_in, w_attn_out=w_attn_out, w_conf_out=w_conf_out, w_sc_out=w_sc_out, w_hg_out=w_hg_out,
        w_gate=w_gate, w_o=w_o, ffn_w1=ffn_w1, ffn_w3=ffn_w3, ffn_w2=ffn_w2).items()}

    cvec = jnp.concatenate([c_ctx[None], c, jnp.zeros((MOD_ROWS - 1 - dec_batch, d), F32)], axis=0)
    mods = _modulation(cvec, ada_w, ada_b)
    mods = mods.reshape(depth, MOD_ROWS, 6, d)[:, :1 + dec_batch]
    mods = jnp.pad(mods, ((0, 0), (0, 0), (0, MOD_ROWS - 6), (0, 0)))

    lb = jax.nn.softmax(hg_lb.astype(F32), axis=1)
    lb = jnp.maximum(jnp.cumsum(lb, axis=1) - lb[:, :1], 0.0)

    rope_tabs = _rope_tables(lay)
    ctx_k = cache_k.reshape(dec_batch, depth, past, KV_WIDTH)
    ctx_v = cache_v.reshape(dec_batch, depth, past, KV_WIDTH)
    zero_states = jnp.zeros((batch, HG_HEADS, HG_DV, HG_DK), F32)

    x = (x_prompt.reshape(lay.p_rows, d), x_sample.reshape(lay.s_rows, d))
    ks_out, vs_out, sf_out, sb_out = [], [], [], []
    for l in range(depth):
        z, h = _in_projection(lay, l, x, mods, norm1_g, wb["w_in"])
        qn, kn, kf = _qk_prep(lay, l, z, rope_tabs, q_norm_g, k_norm_g)
        attn_p = _context_attention(lay, l, qn, kn, z, attn_sink[l])
        attn_s = _latent_attention(lay, l, qn, kn, z, ctx_k, ctx_v, attn_sink[l])
        u_b, u_c = _conv_mixers(lay, l, z, conf_dw_w, conf_dw_b, conf_ln_g, conf_ln_b, sc_conv_w)
        s0_f = jnp.concatenate([zero_states, jnp.swapaxes(state_hgrn_fwd[:, l], -1, -2)], axis=0)
        s0_b = jnp.concatenate([zero_states, jnp.swapaxes(state_hgrn_bwd[:, l], -1, -2)], axis=0)
        o_f, s_f = _hgrn_direction(lay, z, lb[0, l][None], s0_f, reverse=False)
        o_b, s_b = _hgrn_direction(lay, z, lb[1, l][None], s0_b, reverse=True)
        o_d = _hgrn_output(lay, l, o_f, o_b, z, hg_norm_g)
        merged = _merge(lay, l, h, attn_p, attn_s, (u_b, u_c, o_d), wb["w_gate"], b_gate,
                        (wb["w_attn_out"], wb["w_conf_out"], wb["w_sc_out"], wb["w_hg_out"]))
        x, h2 = _out_projection(lay, l, merged, wb["w_o"], x, mods, norm2_g)
        t = _ffn_up(lay, l, h2, wb["ffn_w1"], wb["ffn_w3"])
        if l + 1 < depth:
            x = _residual_matmul(lay, l, t, wb["ffn_w2"], x, mods, MOD_GATE2, "ffn_down", tm=1024, tn=512)
        else:
            y_p, y_s = (_residual_matmul(lay, l, t, wb["ffn_w2"], x, mods, MOD_GATE2, "ffn_down", tm=1024, tn=512,
                                         rows=r) for r in ((0, lay.p_rows), (lay.p_rows, lay.rows)))
        ks_out.append(kf[:lay.p_rows].reshape(batch, seq, N_KV_A, HEAD_DIM))
        vs_out.append(z[:lay.p_rows, A_WIDTH + KV_WIDTH:A_WIDTH + 2 * KV_WIDTH].reshape(batch, seq, N_KV_A, HEAD_DIM))
        sf_out.append(jnp.swapaxes(s_f[:batch], -1, -2))
        sb_out.append(jnp.swapaxes(s_b[:batch], -1, -2))
    return (y_p.reshape(batch, seq, d), y_s.reshape(dec_batch, dec_seq, d),
            jnp.stack(ks_out, axis=1), jnp.stack(vs_out, axis=1),
            jnp.stack(sf_out, axis=1), jnp.stack(sb_out, axis=1))
```

```python
import functools

import jax
import jax.numpy as jnp
from jax import lax
from jax.experimental import pallas as pl
from jax.experimental.pallas import tpu as pltpu

F32 = jnp.float32
BF16 = jnp.bfloat16

V7X_VMEM_BYTES = 64 * 1024 * 1024
VMEM_LIMIT_BYTES = V7X_VMEM_BYTES - 8 * 1024 * 1024
LANES = 128
SUBLANES = 8
MXU_WIDTH = 256

GRID_W = 64
HEAD_DIM = 128
N_HEADS_A = 8
N_KV_A = 2
GROUP_A = N_HEADS_A // N_KV_A
A_WIDTH = N_HEADS_A * HEAD_DIM
KV_WIDTH = N_KV_A * HEAD_DIM
WINDOW = 128
BLOCK = 128
ROPE_BASE = 10000.0
ROT_AXIS = HEAD_DIM // 2
ATTN_SCALE = HEAD_DIM ** -0.5
NEG_MASK = -1e9
CONF_W = 512
CONF_K = 31
SC_W = 512
SC_K = 3
HG_HEADS = 4
HG_DK = 128
HG_DV = 128
HG_W = HG_HEADS * HG_DK
HG_CHUNK = 32
N_BRANCH = 4
EPS = 1e-6

COL_W = 512
COL_CONF_A, COL_CONF_G, COL_SC_B, COL_SC_C, COL_SC_H = 3, 4, 5, 6, 7
COL_HQ, COL_HF_FWD, COL_HF_BWD, COL_HI, COL_HG = 8, 9, 10, 11, 12
KV_COL_K, KV_COL_V = A_WIDTH // KV_WIDTH, A_WIDTH // KV_WIDTH + 1

ATTN_Q_ROWS = 256
SEQ_TILE = 256
HALO = 16
HG_SAFE_LOG_DECAY = -60.0

MOD_SHIFT1, MOD_SCALE1, MOD_GATE1, MOD_SHIFT2, MOD_SCALE2, MOD_GATE2 = range(6)
MOD_ROWS = 8


def _cparams(*semantics):
    return pltpu.CompilerParams(dimension_semantics=semantics, vmem_limit_bytes=VMEM_LIMIT_BYTES)


def _sigmoid(x):
    return 1.0 / (1.0 + jnp.exp(-x))


def _silu(x):
    return x * _sigmoid(x)


def _chunk_of(idx, size):
    assert size & (size - 1) == 0
    return jnp.right_shift(idx, size.bit_length() - 1)


def _dot(a, b):
    return jnp.dot(a, b, preferred_element_type=F32)


def _dot_nt(a, b):
    return lax.dot_general(a, b, (((1,), (1,)), ((), ())), preferred_element_type=F32)


class _Layout:
    def __init__(self, batch, seq, dec_batch, dec_seq):
        self.batch, self.seq, self.dec_batch, self.dec_seq = batch, seq, dec_batch, dec_seq
        self.p_rows = batch * seq
        self.s_rows = dec_batch * dec_seq
        self.rows = self.p_rows + self.s_rows
        self.n_seq = batch + dec_batch

    def row_tile(self, preferred):
        t = preferred
        while self.p_rows % t or self.dec_seq % t:
            t //= 2
        return t

    def mod_row(self, i, tile):
        p_tiles = self.p_rows // tile
        per = self.dec_seq // tile
        return jnp.where(i < p_tiles, 0, 1 + (i - p_tiles) // per)

    def seq_pos(self, n, tile):
        p_tiles = self.p_rows // tile
        tpp = self.seq // tile
        tps = self.dec_seq // tile
        is_p = n < p_tiles
        seq_idx = jnp.where(is_p, n // tpp, self.batch + (n - p_tiles) // tps)
        pos = jnp.where(is_p, n % tpp, (n - p_tiles) % tps)
        count = jnp.where(is_p, tpp, tps)
        return seq_idx, pos, count


def _mod_kernel(c_ref, w_ref, b_ref, o_ref):
    s = _silu(c_ref[...]).astype(BF16)
    o_ref[...] = _dot(s, w_ref[...].astype(BF16)) + b_ref[...]


def _modulation(cvec, ada_w, ada_b):
    depth, d, n = ada_w.shape
    tn = 1024
    return pl.pallas_call(
        _mod_kernel,
        grid=(depth, n // tn),
        in_specs=[
            pl.BlockSpec((MOD_ROWS, d), lambda l, j: (0, 0)),
            pl.BlockSpec((None, d, tn), lambda l, j: (l, 0, j)),
            pl.BlockSpec((None, 1, tn), lambda l, j: (l, 0, j)),
        ],
        out_specs=pl.BlockSpec((None, MOD_ROWS, tn), lambda l, j: (l, 0, j)),
        out_shape=jax.ShapeDtypeStruct((depth, MOD_ROWS, n), F32),
        compiler_params=_cparams("parallel", "parallel"),
        name="modulation",
    )(cvec, ada_w, ada_b.reshape(depth, 1, n))


NORM_ROWS = 16


def _modulated_norm(x_ref, g_ref, mod_ref, shift_row, scale_row, h_ref):
    gain = g_ref[...] * (1.0 + mod_ref[scale_row:scale_row + 1, :])
    shift = mod_ref[shift_row:shift_row + 1, :]
    block = 8 * NORM_ROWS

    def body(c, carry):
        start = pl.multiple_of(c * block, block)
        for r0 in range(0, block, NORM_ROWS):
            rows = pl.ds(start + r0, NORM_ROWS)
            x = x_ref[rows, :]
            y = x * lax.rsqrt(jnp.mean(x * x, axis=-1, keepdims=True) + EPS)
            h_ref[rows, :] = (y * gain + shift).astype(BF16)
        return carry

    lax.fori_loop(0, x_ref.shape[0] // block, body, 0)


def _split_rows_specs(lay, tm, width, tile_of):
    p_tiles = lay.p_rows // tm
    s_tiles = lay.s_rows // tm
    return [pl.BlockSpec((tm, width), lambda *g: (jnp.minimum(tile_of(*g), p_tiles - 1), 0)),
            pl.BlockSpec((tm, width), lambda *g: (jnp.clip(tile_of(*g) - p_tiles, 0, s_tiles - 1), 0))]


def _inproj_kernel(*refs, p_tiles):
    x_refs, (mod_ref, g_ref, w_ref, z_ref, h_ref) = refs[:-5], refs[-5:]
    i = pl.program_id(0)
    first_step = pl.program_id(1) == 0
    for x_ref, mine in zip(x_refs, (True,) if p_tiles is None else (i < p_tiles, i >= p_tiles)):
        @pl.when(first_step & mine)
        def _():
            _modulated_norm(x_ref, g_ref, mod_ref, MOD_SHIFT1, MOD_SCALE1, h_ref)

    z_ref[...] = _dot(h_ref[...], w_ref[...])


def _in_projection(lay, l, x, mods, norm_g, w_in):
    pair = isinstance(x, tuple)
    m, d = lay.rows, w_in.shape[1]
    n = w_in.shape[-1]
    tm = lay.row_tile(1024)
    tn = 512
    x_specs = (_split_rows_specs(lay, tm, d, lambda i, j: i) if pair
               else [pl.BlockSpec((tm, d), lambda i, j: (i, 0))])
    return pl.pallas_call(
        functools.partial(_inproj_kernel, p_tiles=lay.p_rows // tm if pair else None),
        grid=(m // tm, n // tn),
        in_specs=x_specs + [
            pl.BlockSpec((None, None, MOD_ROWS, d), lambda i, j: (l, lay.mod_row(i, tm), 0, 0)),
            pl.BlockSpec((None, 1, d), lambda i, j: (l, 0, 0)),
            pl.BlockSpec((None, d, tn), lambda i, j: (l, 0, j)),
        ],
        out_specs=[
            pl.BlockSpec((tm, tn), lambda i, j: (i, j)),
            pl.BlockSpec((tm, d), lambda i, j: (i, 0)),
        ],
        out_shape=[jax.ShapeDtypeStruct((m, n), F32), jax.ShapeDtypeStruct((m, d), BF16)],
        compiler_params=_cparams("parallel", "arbitrary"),
        name="in_projection",
    )(*(x if pair else (x,)), mods, norm_g.reshape(norm_g.shape[0], 1, d), w_in)


def _residual_matmul_kernel(a_ref, w_ref, x_ref, mod_ref, o_ref, *, gate_row):
    a = a_ref[...]
    chunk = 512
    for c0 in range(0, o_ref.shape[1], chunk):
        sl = slice(c0, c0 + chunk)
        o_ref[:, sl] = x_ref[:, sl] + mod_ref[gate_row:gate_row + 1, sl] * _dot(a, w_ref[:, sl])


def _residual_matmul(lay, l, a, w, x, mods, gate_row, name, tm, tn, rows=None):
    m, k = a.shape
    d = w.shape[-1]
    tm = lay.row_tile(tm)
    first, stop = (0, m) if rows is None else rows
    assert first % tm == 0 and stop % tm == 0
    t0 = first // tm
    return pl.pallas_call(
        functools.partial(_residual_matmul_kernel, gate_row=gate_row),
        grid=((stop - first) // tm, d // tn),
        in_specs=[
            pl.BlockSpec((tm, k), lambda i, j: (t0 + i, 0)),
            pl.BlockSpec((None, k, tn), lambda i, j: (l, 0, j)),
            pl.BlockSpec((tm, tn), lambda i, j: (t0 + i, j)),
            pl.BlockSpec((None, None, MOD_ROWS, tn), lambda i, j: (l, lay.mod_row(t0 + i, tm), 0, j)),
        ],
        out_specs=pl.BlockSpec((tm, tn), lambda i, j: (i, j)),
        out_shape=jax.ShapeDtypeStruct((stop - first, d), F32),
        compiler_params=_cparams("parallel", "arbitrary"),
        name=name,
    )(a, w, x, mods)


def _out_proj_kernel(a_ref, w_ref, *refs, p_tiles):
    x_refs, (mod_ref, g_ref, o_ref, h_ref) = refs[:-4], refs[-4:]
    i = pl.program_id(0)
    a = a_ref[...]
    chunk = 512
    for c0 in range(0, o_ref.shape[1], chunk):
        sl = slice(c0, c0 + chunk)
        update = mod_ref[MOD_GATE1:MOD_GATE1 + 1, sl] * _dot(a, w_ref[:, sl])
        if p_tiles is None:
            o_ref[:, sl] = x_refs[0][:, sl] + update
        else:
            o_ref[:, sl] = jnp.where(i < p_tiles, x_refs[0][:, sl], x_refs[1][:, sl]) + update
    _modulated_norm(o_ref, g_ref, mod_ref, MOD_SHIFT2, MOD_SCALE2, h_ref)


def _out_projection(lay, l, a, w_o, x, mods, norm2_g):
    pair = isinstance(x, tuple)
    m, k = a.shape
    d = w_o.shape[-1]
    tm = lay.row_tile(512)
    x_specs = _split_rows_specs(lay, tm, d, lambda i: i) if pair else [pl.BlockSpec((tm, d), lambda i: (i, 0))]
    return pl.pallas_call(
        functools.partial(_out_proj_kernel, p_tiles=lay.p_rows // tm if pair else None),
        grid=(m // tm,),
        in_specs=[
            pl.BlockSpec((tm, k), lambda i: (i, 0)),
            pl.BlockSpec((None, k, d), lambda i: (l, 0, 0)),
        ] + x_specs + [
            pl.BlockSpec((None, None, MOD_ROWS, d), lambda i: (l, lay.mod_row(i, tm), 0, 0)),
            pl.BlockSpec((None, 1, d), lambda i: (l, 0, 0)),
        ],
        out_specs=[pl.BlockSpec((tm, d), lambda i: (i, 0)), pl.BlockSpec((tm, d), lambda i: (i, 0))],
        out_shape=[jax.ShapeDtypeStruct((m, d), F32), jax.ShapeDtypeStruct((m, d), BF16)],
        compiler_params=_cparams("parallel"),
        name="out_projection",
    )(a, w_o, *(x if pair else (x,)), mods, norm2_g.reshape(norm2_g.shape[0], 1, d))


def _ffn_up_kernel(h_ref, w1_ref, w3_ref, o_ref):
    h = h_ref[...]
    tn = o_ref.shape[1]
    tail = tn % MXU_WIDTH
    c0 = 0
    while c0 < tn - tail:
        c1 = min(c0 + 512, tn - tail)
        o_ref[:, c0:c1] = (_silu(_dot(h, w1_ref[:, c0:c1])) * _dot(h, w3_ref[:, c0:c1])).astype(BF16)
        c0 = c1
    if tail:
        both = _dot(h, jnp.concatenate([w1_ref[:, c0:], w3_ref[:, c0:]], axis=1))
        o_ref[:, c0:] = (_silu(both[:, :tail]) * both[:, tail:]).astype(BF16)


def _ffn_up(lay, l, h, w1, w3):
    m, d = h.shape
    n = w1.shape[-1]
    tm = lay.row_tile(1024)
    tn = n // 4
    assert n % 4 == 0 and tn % LANES == 0
    return pl.pallas_call(
        _ffn_up_kernel,
        grid=(m // tm, n // tn),
        in_specs=[
            pl.BlockSpec((tm, d), lambda i, j: (i, 0)),
            pl.BlockSpec((None, d, tn), lambda i, j: (l, 0, j)),
            pl.BlockSpec((None, d, tn), lambda i, j: (l, 0, j)),
        ],
        out_specs=pl.BlockSpec((tm, tn), lambda i, j: (i, j)),
        out_shape=jax.ShapeDtypeStruct((m, n), BF16),
        compiler_params=_cparams("parallel", "parallel"),
        name="ffn_up",
    )(h, w1, w3)


def _qk_prep_kernel(q_ref, k_ref, cos_ref, sa_ref, sb_ref, qg_ref, kg_ref, qo_ref, ko_ref, kf_ref):
    cos, sin_a, sin_b = cos_ref[...], sa_ref[...], sb_ref[...]

    ones = jnp.ones((HEAD_DIM, HEAD_DIM), BF16)

    def norm(x, g):
        mean_sq = _dot((x * x).astype(BF16), ones) * (1.0 / HEAD_DIM)
        return x * lax.rsqrt(mean_sq + EPS) * g

    def rope(x):
        return (x * cos + pltpu.roll(x, HEAD_DIM - ROT_AXIS // 2, 1) * sin_a
                + pltpu.roll(x, ROT_AXIS // 2, 1) * sin_b)

    for h in range(N_HEADS_A):
        sl = slice(h * HEAD_DIM, (h + 1) * HEAD_DIM)
        qo_ref[:, sl] = (rope(norm(q_ref[:, sl], qg_ref[...])) * ATTN_SCALE).astype(BF16)
    for h in range(N_KV_A):
        sl = slice(h * HEAD_DIM, (h + 1) * HEAD_DIM)
        kn = norm(k_ref[:, sl], kg_ref[...])
        kf_ref[:, sl] = kn
        ko_ref[:, sl] = rope(kn).astype(BF16)


def _qk_prep(lay, l, z, rope_tabs, q_norm_g, k_norm_g):
    m = z.shape[0]
    tm = lay.row_tile(512)
    cos, sin_a, sin_b = rope_tabs
    tab_spec = pl.BlockSpec((tm, HEAD_DIM), lambda i: (i, 0))
    g_spec = pl.BlockSpec((None, 1, HEAD_DIM), lambda i: (l, 0, 0))
    return pl.pallas_call(
        _qk_prep_kernel,
        grid=(m // tm,),
        in_specs=[
            pl.BlockSpec((tm, A_WIDTH), lambda i: (i, 0)),
            pl.BlockSpec((tm, KV_WIDTH), lambda i: (i, KV_COL_K)),
            tab_spec, tab_spec, tab_spec, g_spec, g_spec,
        ],
        out_specs=[
            pl.BlockSpec((tm, A_WIDTH), lambda i: (i, 0)),
            pl.BlockSpec((tm, KV_WIDTH), lambda i: (i, 0)),
            pl.BlockSpec((tm, KV_WIDTH), lambda i: (i, 0)),
        ],
        out_shape=[
            jax.ShapeDtypeStruct((m, A_WIDTH), BF16),
            jax.ShapeDtypeStruct((m, KV_WIDTH), BF16),
            jax.ShapeDtypeStruct((m, KV_WIDTH), F32),
        ],
        compiler_params=_cparams("parallel"),
        name="qk_prep",
    )(z, z, cos, sin_a, sin_b, q_norm_g.reshape(-1, 1, HEAD_DIM), k_norm_g.reshape(-1, 1, HEAD_DIM))


def _stack_heads(q_ref, kv):
    return jnp.concatenate(
        [q_ref[:, (kv * GROUP_A + g) * HEAD_DIM:(kv * GROUP_A + g + 1) * HEAD_DIM] for g in range(GROUP_A)], axis=0)


def _fold_lane_tiles(x, op):
    out = x[:, :LANES]
    for c0 in range(LANES, x.shape[1], LANES):
        out = op(out, x[:, c0:c0 + LANES])
    return out


def _sink_column(sink_ref, kv, rows):
    return jnp.concatenate(
        [jnp.full((rows, 1), sink_ref[kv * GROUP_A + g], F32) for g in range(GROUP_A)], axis=0)


def _context_attention_kernel(sink_ref, q_ref, k_ref, v_ref, o_ref):
    rows = q_ref.shape[0]
    for kv in range(N_KV_A):
        sl = slice(kv * HEAD_DIM, (kv + 1) * HEAD_DIM)
        q = _stack_heads(q_ref, kv)
        s = _dot_nt(q, k_ref[:, sl])
        sink = _sink_column(sink_ref, kv, rows)
        mx = jnp.maximum(jnp.max(_fold_lane_tiles(s, jnp.maximum), axis=-1, keepdims=True), sink)
        p = jnp.exp(s - mx)
        den = jnp.sum(_fold_lane_tiles(p, jnp.add), axis=-1, keepdims=True) + jnp.exp(sink - mx)
        o = _dot(p.astype(BF16), v_ref[:, sl].astype(BF16)) * (1.0 / den)
        for g in range(GROUP_A):
            h = kv * GROUP_A + g
            o_ref[:, h * HEAD_DIM:(h + 1) * HEAD_DIM] = o[g * rows:(g + 1) * rows].astype(BF16)


def _context_attention(lay, l, qn, kn, z, sink):
    seq = lay.seq
    return pl.pallas_call(
        _context_attention_kernel,
        grid=(lay.batch,),
        in_specs=[
            pl.BlockSpec(memory_space=pltpu.SMEM),
            pl.BlockSpec((seq, A_WIDTH), lambda b: (b, 0)),
            pl.BlockSpec((seq, KV_WIDTH), lambda b: (b, 0)),
            pl.BlockSpec((seq, KV_WIDTH), lambda b: (b, KV_COL_V)),
        ],
        out_specs=pl.BlockSpec((seq, A_WIDTH), lambda b: (b, 0)),
        out_shape=jax.ShapeDtypeStruct((lay.p_rows, A_WIDTH), BF16),
        compiler_params=_cparams("parallel"),
        name="context_attention",
    )(sink, qn, kn, z)


def _latent_attention_kernel(sink_ref, q_ref, kp_ref, kc_ref, kn_ref, vp_ref, vc_ref, vn_ref, ck_ref, cv_ref,
                             o_ref, *, seq_len):
    assert WINDOW <= BLOCK
    q_rows = q_ref.shape[0]
    j = pl.program_id(1)
    n_tiles = seq_len // q_rows
    r = lax.broadcasted_iota(jnp.int32, (q_rows, BLOCK), 0)
    c = lax.broadcasted_iota(jnp.int32, (q_rows, BLOCK), 1)
    valid_prev = jnp.concatenate([(r - (c - BLOCK) <= WINDOW) & (j >= 1)] * GROUP_A, axis=0)
    valid_next = jnp.concatenate([((c + q_rows) - r <= WINDOW) & (j <= n_tiles - 2)] * GROUP_A, axis=0)
    valid_own = None
    if q_rows - 1 > WINDOW:
        ro = lax.broadcasted_iota(jnp.int32, (q_rows, q_rows), 0)
        co = lax.broadcasted_iota(jnp.int32, (q_rows, q_rows), 1)
        valid_own = jnp.concatenate([jnp.abs(co - ro) <= WINDOW] * GROUP_A, axis=0)
    for kv in range(N_KV_A):
        sl = slice(kv * HEAD_DIM, (kv + 1) * HEAD_DIM)
        q = _stack_heads(q_ref, kv)
        s_p = jnp.where(valid_prev, _dot_nt(q, kp_ref[:, sl]), NEG_MASK)
        s_o = _dot_nt(q, kc_ref[:, sl])
        if valid_own is not None:
            s_o = jnp.where(valid_own, s_o, NEG_MASK)
        s_n = jnp.where(valid_next, _dot_nt(q, kn_ref[:, sl]), NEG_MASK)
        s_c = _dot_nt(q, ck_ref[:, sl].astype(BF16))
        sink = _sink_column(sink_ref, kv, q_rows)
        mx = jnp.maximum(jnp.maximum(s_p, _fold_lane_tiles(s_o, jnp.maximum)),
                         jnp.maximum(s_n, _fold_lane_tiles(s_c, jnp.maximum)))
        mx = jnp.maximum(jnp.max(mx, axis=-1, keepdims=True), sink)
        psum = None
        o = None
        for s, v_ref in ((s_p, vp_ref), (s_o, vc_ref), (s_n, vn_ref), (s_c, cv_ref)):
            p = jnp.exp(s - mx)
            pf = _fold_lane_tiles(p, jnp.add)
            psum = pf if psum is None else psum + pf
            term = _dot(p.astype(BF16), v_ref[:, sl].astype(BF16))
            o = term if o is None else o + term
        den = jnp.sum(psum, axis=-1, keepdims=True) + jnp.exp(sink - mx)
        o = o * (1.0 / den)
        for g in range(GROUP_A):
            h = kv * GROUP_A + g
            o_ref[:, h * HEAD_DIM:(h + 1) * HEAD_DIM] = o[g * q_rows:(g + 1) * q_rows].astype(BF16)


def _latent_attention(lay, l, qn, kn, z, ctx_k, ctx_v, sink):
    q_rows = ATTN_Q_ROWS
    per_tile = q_rows // BLOCK
    assert lay.p_rows % q_rows == 0 and lay.dec_seq % q_rows == 0
    nb = lay.dec_seq // BLOCK
    nt = lay.dec_seq // q_rows
    base = lay.p_rows // BLOCK
    past = ctx_k.shape[2]

    def own(col):
        return lambda b, j: (lay.p_rows // q_rows + b * nt + j, col)

    def prev(col):
        return lambda b, j: (base + b * nb + jnp.maximum(j * per_tile - 1, 0), col)

    def nxt(col):
        return lambda b, j: (base + b * nb + jnp.minimum((j + 1) * per_tile, nb - 1), col)

    def trio(col):
        return [pl.BlockSpec((BLOCK, KV_WIDTH), prev(col)), pl.BlockSpec((q_rows, KV_WIDTH), own(col)),
                pl.BlockSpec((BLOCK, KV_WIDTH), nxt(col))]

    k_spec = trio(0)
    v_spec = trio(KV_COL_V)
    ctx_spec = pl.BlockSpec((None, None, past, KV_WIDTH), lambda b, j: (b, l, 0, 0))
    return pl.pallas_call(
        functools.partial(_latent_attention_kernel, seq_len=lay.dec_seq),
        grid=(lay.dec_batch, nt),
        in_specs=[pl.BlockSpec(memory_space=pltpu.SMEM), pl.BlockSpec((q_rows, A_WIDTH), own(0))]
        + k_spec + v_spec + [ctx_spec, ctx_spec],
        out_specs=pl.BlockSpec((q_rows, A_WIDTH), lambda b, j: (b * nt + j, 0)),
        out_shape=jax.ShapeDtypeStruct((lay.s_rows, A_WIDTH), BF16),
        compiler_params=_cparams("parallel", "parallel"),
        name="latent_attention",
    )(sink, qn, kn, kn, kn, z, z, z, ctx_k, ctx_v)


def _conv_kernel(a_c, a_p, a_n, g_c, g_p, g_n, b_c, c_c, c_p, c_n, h_c, h_p, h_n,
                 dw_ref, db_ref, lg_ref, lb_ref, sw_ref, ub_ref, uc_ref, xb_s, xc_s, *, lay):
    _, pos, count = lay.seq_pos(pl.program_id(0), SEQ_TILE)
    has_prev = pos > 0
    has_next = pos < count - 1
    rb = 32
    xb_s[0, 0:HALO, :] = jnp.where(has_prev, a_p[...] * _sigmoid(g_p[...]), 0.0)
    xb_s[0, HALO + SEQ_TILE:, :] = jnp.where(has_next, a_n[...] * _sigmoid(g_n[...]), 0.0)
    xc_s[0:HALO, :] = jnp.where(has_prev, c_p[...] * h_p[...], 0.0)
    xc_s[HALO + SEQ_TILE:, :] = jnp.where(has_next, c_n[...] * h_n[...], 0.0)
    for r0 in range(0, SEQ_TILE, rb):
        rows = slice(r0, r0 + rb)
        xb_s[0, HALO + r0:HALO + r0 + rb, :] = a_c[rows, :] * _sigmoid(g_c[rows, :])
        xc_s[HALO + r0:HALO + r0 + rb, :] = c_c[rows, :] * h_c[rows, :]
    shifted_rows = SEQ_TILE + 2 * HALO - SUBLANES
    chunk = 40
    for s in range(1, SUBLANES):
        for r0 in range(0, shifted_rows, chunk):
            xb_s[s, r0:r0 + chunk, :] = xb_s[0, r0 + s:r0 + s + chunk, :]
    for r0 in range(0, SEQ_TILE, rb):
        acc = jnp.zeros((rb // SUBLANES, SUBLANES, CONF_W), F32)
        for k in range(CONF_K):
            off = HALO - CONF_K // 2 + k
            row = r0 + off - off % SUBLANES
            x = xb_s[off % SUBLANES, row:row + rb, :]
            acc = acc + dw_ref[k] * x.reshape(rb // SUBLANES, SUBLANES, CONF_W)
        u = acc.reshape(rb, CONF_W) + db_ref[...]
        uc = u - jnp.mean(u, axis=-1, keepdims=True)
        var = jnp.mean(uc * uc, axis=-1, keepdims=True)
        u = uc * lax.rsqrt(var + EPS) * lg_ref[...] + lb_ref[...]
        ub_ref[r0:r0 + rb, :] = _silu(u).astype(BF16)
        acc = jnp.zeros((rb // SUBLANES, SUBLANES, SC_W), F32)
        for k in range(SC_K):
            off = HALO - SC_K // 2 + r0 + k
            acc = acc + sw_ref[k] * xc_s[off:off + rb, :].reshape(rb // SUBLANES, SUBLANES, SC_W)
        uc_ref[r0:r0 + rb, :] = (b_c[r0:r0 + rb, :] * acc.reshape(rb, SC_W)).astype(BF16)


def _conv_mixers(lay, l, z, conf_dw_w, conf_dw_b, conf_ln_g, conf_ln_b, sc_conv_w):
    m = z.shape[0]
    n_tiles = m // SEQ_TILE
    per = SEQ_TILE // HALO
    n_halo = m // HALO

    def cur(col):
        return pl.BlockSpec((SEQ_TILE, COL_W), lambda i: (i, col))

    def prev(col):
        return pl.BlockSpec((HALO, COL_W), lambda i: (jnp.maximum(i * per - 1, 0), col))

    def nxt(col):
        return pl.BlockSpec((HALO, COL_W), lambda i: (jnp.minimum((i + 1) * per, n_halo - 1), col))

    def trio(col):
        return [cur(col), prev(col), nxt(col)]

    def par(rows):
        return pl.BlockSpec((None, rows, COL_W), lambda i: (l, 0, 0))

    def taps(n):
        return pl.BlockSpec((None, n, SUBLANES, COL_W), lambda i: (l, 0, 0, 0))

    depth = conf_dw_w.shape[0]
    dw = jnp.broadcast_to(conf_dw_w[:, :, None, :], (depth, CONF_K, SUBLANES, CONF_W))
    sw = jnp.broadcast_to(sc_conv_w[:, :, None, :], (depth, SC_K, SUBLANES, SC_W))
    return pl.pallas_call(
        functools.partial(_conv_kernel, lay=lay),
        grid=(n_tiles,),
        in_specs=trio(COL_CONF_A) + trio(COL_CONF_G) + [cur(COL_SC_B)] + trio(COL_SC_C) + trio(COL_SC_H)
        + [taps(CONF_K), par(1), par(1), par(1), taps(SC_K)],
        out_specs=[pl.BlockSpec((SEQ_TILE, CONF_W), lambda i: (i, 0)),
                   pl.BlockSpec((SEQ_TILE, SC_W), lambda i: (i, 0))],
        out_shape=[jax.ShapeDtypeStruct((m, CONF_W), BF16), jax.ShapeDtypeStruct((m, SC_W), BF16)],
        scratch_shapes=[pltpu.VMEM((SUBLANES, SEQ_TILE + 2 * HALO, CONF_W), F32),
                        pltpu.VMEM((SEQ_TILE + 2 * HALO, SC_W), F32)],
        compiler_params=_cparams("parallel"),
        name="conv_mixers",
    )(*([z] * 13), dw, conf_dw_b.reshape(depth, 1, CONF_W), conf_ln_g.reshape(depth, 1, CONF_W),
      conf_ln_b.reshape(depth, 1, CONF_W), sw)


def _hgrn_gates(hq_ref, hf_ref, lb_ref, sl, tri_b3):
    zq = hq_ref[:, sl]
    q = _silu(zq) * (HG_DK ** -0.5)
    zf = hf_ref[:, sl]
    lb = lb_ref[:, sl]
    e = jnp.exp(-jnp.abs(zf))
    r = 1.0 / (1.0 + e)
    nonneg = zf >= 0
    sig_pos = jnp.where(nonneg, r, e * r)
    sig_neg = jnp.where(nonneg, e * r, r)
    logf = jnp.log(lb + (1.0 - lb) * sig_pos)
    kk = (1.0 - lb) * sig_neg
    g1 = logf.astype(BF16)
    r1 = logf - g1.astype(F32)
    g2 = r1.astype(BF16)
    g3 = (r1 - g2.astype(F32)).astype(BF16)
    b = _dot(tri_b3, jnp.concatenate([g1, g2, g3], axis=0))
    return q, kk, b


def _block_place(x, n_chunks):
    w = x.shape[1]
    zeros = jnp.zeros((HG_CHUNK, w), x.dtype)
    row_blocks = []
    for c in range(n_chunks):
        cols = [zeros] * n_chunks
        cols[c] = x[c * HG_CHUNK:(c + 1) * HG_CHUNK, :]
        row_blocks.append(jnp.concatenate(cols, axis=1))
    return jnp.concatenate(row_blocks, axis=0)


def _hgrn_exact_scores(a_ref, q_s, k_s, b_s, reverse):
    ts = a_ref.shape[0]
    a_ref[...] = jnp.zeros_like(a_ref)

    def body(s, carry):
        c0 = pl.multiple_of((s // HG_CHUNK) * HG_CHUNK, HG_CHUNK)
        w = jnp.exp(jnp.minimum(b_s[pl.ds(c0, HG_CHUNK), :] - b_s[pl.ds(s, 1), :], 0.0))
        prod = q_s[pl.ds(c0, HG_CHUNK), :] * k_s[pl.ds(s, 1), :] * w
        trow = c0 + lax.broadcasted_iota(jnp.int32, (HG_CHUNK, 1), 0)
        ok = (trow <= s) if reverse else (trow >= s)
        onehot = (lax.broadcasted_iota(jnp.int32, (1, ts), 1) == s).astype(F32)
        colv = jnp.sum(prod, axis=-1, keepdims=True)
        a_ref[pl.ds(c0, HG_CHUNK), :] += jnp.where(ok, colv, 0.0) * onehot
        return carry

    lax.fori_loop(0, ts, body, 0)


def _hgrn_kernel(hq_ref, hf_ref, hi_ref, lb_ref, s0_ref, *refs, lay, reverse, finish):
    ts = SEQ_TILE
    n_chunks = ts // HG_CHUNK
    _, pos, count = lay.seq_pos(pl.program_id(0), ts)
    other_ref, gate_ref, gain_ref = refs[:3] if finish else (None, None, None)
    o_ref, sf_ref, st_ref, oi_s, a_ref, q_s, k_s, b_s = refs[3:] if finish else refs

    def emit(o, sl):
        if finish:
            o = o + other_ref[:, sl]
            o = o * lax.rsqrt(jnp.mean(o * o, axis=-1, keepdims=True) + EPS) * gain_ref[...]
            o_ref[:, sl] = (o * _silu(gate_ref[:, sl])).astype(BF16)
        else:
            o_ref[:, sl] = o

    @pl.when(pos == 0)
    def _():
        st_ref[...] = s0_ref[...]

    row = lax.broadcasted_iota(jnp.int32, (ts, ts), 0)
    col = lax.broadcasted_iota(jnp.int32, (ts, ts), 1)
    same = _chunk_of(row, HG_CHUNK) == _chunk_of(col, HG_CHUNK)
    tri = same & ((col >= row) if reverse else (col <= row))
    tri_b = tri.astype(F32).astype(BF16)
    tri_b3 = jnp.concatenate([tri_b] * 3, axis=1)
    order = range(n_chunks - 1, -1, -1) if reverse else range(n_chunks)
    last = 0 if reverse else HG_CHUNK - 1

    b_min = None
    for hd in range(HG_HEADS):
        sl = slice(hd * HG_DK, (hd + 1) * HG_DK)
        q, kk, b = _hgrn_gates(hq_ref, hf_ref, lb_ref, sl, tri_b3)
        head_min = jnp.min(b)
        b_min = head_min if b_min is None else jnp.minimum(b_min, head_min)
        b3 = b.reshape(n_chunks, HG_CHUNK, HG_DK)
        total3 = b3[:, last:last + 1, :]
        qt = (q * jnp.exp(b)).astype(BF16)
        kend = (kk * jnp.exp(jnp.broadcast_to(total3, b3.shape).reshape(ts, HG_DK) - b)).astype(BF16)
        decay = jnp.exp(total3.reshape(n_chunks, HG_DK))
        v_h = hi_ref[:, sl]
        kinv = (kk * jnp.exp(-b)).astype(BF16)
        scores = jnp.where(tri, _dot_nt(qt, kinv), 0.0).astype(BF16)
        o_intra = _dot(scores, v_h.astype(BF16))
        incr = _dot(v_h.T.astype(BF16), _block_place(kend, n_chunks))
        state = st_ref[hd]
        starts = [None] * n_chunks
        for c in order:
            starts[c] = state
            state = state * decay[c:c + 1, :] + incr[:, c * HG_DK:(c + 1) * HG_DK]
        st_ref[hd] = state
        o_inter = jnp.concatenate(
            [_dot_nt(qt[c * HG_CHUNK:(c + 1) * HG_CHUNK, :], starts[c].astype(BF16)) for c in range(n_chunks)],
            axis=0)
        oi_s[:, sl] = o_inter
        emit(o_intra + o_inter, sl)

    @pl.when(b_min <= HG_SAFE_LOG_DECAY)
    def _():
        for hd in range(HG_HEADS):
            sl = slice(hd * HG_DK, (hd + 1) * HG_DK)
            q, kk, b = _hgrn_gates(hq_ref, hf_ref, lb_ref, sl, tri_b3)
            q_s[...] = q
            k_s[...] = kk
            b_s[...] = b
            _hgrn_exact_scores(a_ref, q_s, k_s, b_s, reverse)
            emit(_dot(a_ref[...].astype(BF16), hi_ref[:, sl].astype(BF16)) + oi_s[:, sl], sl)

    @pl.when(pos == count - 1)
    def _():
        sf_ref[...] = st_ref[...]


def _hgrn_direction(lay, l, z, lb, s0_t, reverse, other=None, norm_g=None):
    finish = other is not None
    m = z.shape[0]
    n_tiles = m // SEQ_TILE

    def tile_of(n):
        seq_idx, pos, count = lay.seq_pos(n, SEQ_TILE)
        return (n - pos + (count - 1 - pos)) if reverse else n

    def zcol(col):
        return pl.BlockSpec((SEQ_TILE, COL_W), lambda n: (tile_of(n), col))

    state_spec = pl.BlockSpec((None, HG_HEADS, HG_DV, HG_DK), lambda n: (lay.seq_pos(n, SEQ_TILE)[0], 0, 0, 0))
    o_spec = pl.BlockSpec((SEQ_TILE, HG_W), lambda n: (tile_of(n), 0))
    finish_specs, finish_args = [], []
    if finish:
        finish_specs = [o_spec, zcol(COL_HG), pl.BlockSpec((None, 1, HG_DV), lambda n: (l, 0, 0))]
        finish_args = [other, z, norm_g.reshape(-1, 1, HG_DV)]
    return pl.pallas_call(
        functools.partial(_hgrn_kernel, lay=lay, reverse=reverse, finish=finish),
        grid=(n_tiles,),
        in_specs=[zcol(COL_HQ), zcol(COL_HF_BWD if reverse else COL_HF_FWD), zcol(COL_HI),
                  pl.BlockSpec((1, HG_W), lambda n: (0, 0)), state_spec] + finish_specs,
        out_specs=[o_spec, state_spec],
        out_shape=[jax.ShapeDtypeStruct((m, HG_W), BF16 if finish else F32),
                   jax.ShapeDtypeStruct((lay.n_seq, HG_HEADS, HG_DV, HG_DK), F32)],
        scratch_shapes=[pltpu.VMEM((HG_HEADS, HG_DV, HG_DK), F32),
                        pltpu.VMEM((SEQ_TILE, HG_W), F32),
                        pltpu.VMEM((SEQ_TILE, SEQ_TILE), F32),
                        pltpu.VMEM((SEQ_TILE, HG_DK), F32),
                        pltpu.VMEM((SEQ_TILE, HG_DK), F32),
                        pltpu.VMEM((SEQ_TILE, HG_DK), F32)],
        compiler_params=_cparams("arbitrary"),
        name="hgrn_bwd" if reverse else "hgrn_fwd",
    )(z, z, z, lb, s0_t, *finish_args)


def _merge_kernel(h_ref, ap_ref, as_ref, b_ref, c_ref, d_ref, wg0, wg1, wg2, wg3, bg0, bg1, bg2, bg3,
                  wa_ref, wb_ref, wc_ref, wd_ref, o_ref, *, p_tiles):
    h = h_ref[...]
    attn = jnp.where(pl.program_id(1) < p_tiles, ap_ref[...], as_ref[...])
    acc = None
    for branch, w_ref, wg_ref, bg_ref in ((attn, wa_ref, wg0, bg0), (b_ref[...], wb_ref, wg1, bg1),
                                          (c_ref[...], wc_ref, wg2, bg2), (d_ref[...], wd_ref, wg3, bg3)):
        gate = _sigmoid(_dot(h, wg_ref[...]) + bg_ref[...])
        term = gate * _dot(branch, w_ref[...])
        acc = term if acc is None else acc + term
    o_ref[...] = acc.astype(BF16)


def _merge(lay, l, h, attn_p, attn_s, branches, w_gate, b_gate, w_outs):
    m, d = h.shape
    tm = lay.row_tile(512)
    tn = 512
    nj = d // tn
    depth = w_gate.shape[0]
    p_tiles = lay.p_rows // tm
    s_tiles = lay.s_rows // tm

    def gate_w(n):
        return pl.BlockSpec((None, d, tn), lambda j, i: (l, 0, n * nj + j))

    def gate_b(n):
        return pl.BlockSpec((None, 1, tn), lambda j, i: (l, 0, n * nj + j))

    b_gate3 = b_gate.reshape(depth, 1, N_BRANCH * d)
    return pl.pallas_call(
        functools.partial(_merge_kernel, p_tiles=p_tiles),
        grid=(nj, m // tm),
        in_specs=[pl.BlockSpec((tm, d), lambda j, i: (i, 0)),
                  pl.BlockSpec((tm, A_WIDTH), lambda j, i: (jnp.minimum(i, p_tiles - 1), 0)),
                  pl.BlockSpec((tm, A_WIDTH), lambda j, i: (jnp.clip(i - p_tiles, 0, s_tiles - 1), 0))]
        + [pl.BlockSpec((tm, br.shape[1]), lambda j, i: (i, 0)) for br in branches]
        + [gate_w(n) for n in range(N_BRANCH)] + [gate_b(n) for n in range(N_BRANCH)]
        + [pl.BlockSpec((None, w.shape[1], tn), lambda j, i: (l, 0, j)) for w in w_outs],
        out_specs=pl.BlockSpec((tm, tn), lambda j, i: (i, j)),
        out_shape=jax.ShapeDtypeStruct((m, d), BF16),
        compiler_params=_cparams("parallel", "parallel"),
        name="merge",
    )(h, attn_p, attn_s, *branches, *([w_gate] * N_BRANCH), *([b_gate3] * N_BRANCH), *w_outs)


def _rope_tables(lay):
    t = lay.dec_seq
    half = ROT_AXIS // 2
    rows = t // GRID_W
    row = jnp.repeat(jnp.arange(rows, dtype=F32), GRID_W)
    col = jnp.tile(jnp.arange(GRID_W, dtype=F32), rows)
    inv = ROPE_BASE ** (-jnp.arange(0, ROT_AXIS, 2, dtype=F32) / ROT_AXIS)
    ang = jnp.concatenate([row[:, None] * inv, row[:, None] * inv, col[:, None] * inv, col[:, None] * inv], axis=1)
    lane = jnp.arange(HEAD_DIM)
    first_half = (lane % ROT_AXIS) < half
    cos = jnp.cos(ang)
    sin = jnp.sin(ang)
    sin_a = jnp.where(first_half, -sin, 0.0)
    sin_b = jnp.where(first_half, 0.0, sin)

    def full(tab, fill):
        return jnp.concatenate([jnp.full((lay.p_rows, HEAD_DIM), fill, F32)] + [tab] * lay.dec_batch, axis=0)

    return full(cos, 1.0), full(sin_a, 0.0), full(sin_b, 0.0)


def kernel(x_prompt, x_sample, cache_k, cache_v, state_hgrn_fwd, state_hgrn_bwd, c, c_ctx, ada_w, ada_b, norm1_g, norm2_g, w_in, q_norm_g, k_norm_g, attn_sink, w_attn_out, conf_dw_w, conf_dw_b, conf_ln_g, conf_ln_b, w_conf_out, sc_conv_w, w_sc_out, hg_lb, hg_norm_g, w_hg_out, w_gate, b_gate, w_o, ffn_w1, ffn_w3, ffn_w2):
    batch, seq, d = x_prompt.shape
    dec_batch, dec_seq, _ = x_sample.shape
    depth = ada_w.shape[0]
    past = cache_k.shape[2]
    lay = _Layout(batch, seq, dec_batch, dec_seq)
    assert seq % SEQ_TILE == 0 and dec_seq % SEQ_TILE == 0 and dec_seq % GRID_W == 0
    assert 1 + dec_batch <= MOD_ROWS

    wb = {name: w.astype(BF16) for name, w in dict(
        w_in=w_in, w_attn_out=w_attn_out, w_conf_out=w_conf_out, w_sc_out=w_sc_out, w_hg_out=w_hg_out,
        w_gate=w_gate, w_o=w_o, ffn_w1=ffn_w1, ffn_w3=ffn_w3, ffn_w2=ffn_w2).items()}

    cvec = jnp.concatenate([c_ctx[None], c, jnp.zeros((MOD_ROWS - 1 - dec_batch, d), F32)], axis=0)
    mods = _modulation(cvec, ada_w, ada_b)
    mods = mods.reshape(depth, MOD_ROWS, 6, d)[:, :1 + dec_batch]
    mods = jnp.pad(mods, ((0, 0), (0, 0), (0, MOD_ROWS - 6), (0, 0)))

    lb = jax.nn.softmax(hg_lb.astype(F32), axis=1)
    lb = jnp.maximum(jnp.cumsum(lb, axis=1) - lb[:, :1], 0.0)

    rope_tabs = _rope_tables(lay)
    ctx_k = cache_k.reshape(dec_batch, depth, past, KV_WIDTH)
    ctx_v = cache_v.reshape(dec_batch, depth, past, KV_WIDTH)
    zero_states = jnp.zeros((batch, HG_HEADS, HG_DV, HG_DK), F32)

    x = (x_prompt.reshape(lay.p_rows, d), x_sample.reshape(lay.s_rows, d))
    ks_out, vs_out, sf_out, sb_out = [], [], [], []
    for l in range(depth):
        z, h = _in_projection(lay, l, x, mods, norm1_g, wb["w_in"])
        qn, kn, kf = _qk_prep(lay, l, z, rope_tabs, q_norm_g, k_norm_g)
        attn_p = _context_attention(lay, l, qn, kn, z, attn_sink[l])
        attn_s = _latent_attention(lay, l, qn, kn, z, ctx_k, ctx_v, attn_sink[l])
        u_b, u_c = _conv_mixers(lay, l, z, conf_dw_w, conf_dw_b, conf_ln_g, conf_ln_b, sc_conv_w)
        s0_f = jnp.concatenate([zero_states, jnp.swapaxes(state_hgrn_fwd[:, l], -1, -2)], axis=0)
        s0_b = jnp.concatenate([zero_states, jnp.swapaxes(state_hgrn_bwd[:, l], -1, -2)], axis=0)
        o_f, s_f = _hgrn_direction(lay, l, z, lb[0, l][None], s0_f, reverse=False)
        o_d, s_b = _hgrn_direction(lay, l, z, lb[1, l][None], s0_b, reverse=True, other=o_f, norm_g=hg_norm_g)
        merged = _merge(lay, l, h, attn_p, attn_s, (u_b, u_c, o_d), wb["w_gate"], b_gate,
                        (wb["w_attn_out"], wb["w_conf_out"], wb["w_sc_out"], wb["w_hg_out"]))
        x, h2 = _out_projection(lay, l, merged, wb["w_o"], x, mods, norm2_g)
        t = _ffn_up(lay, l, h2, wb["ffn_w1"], wb["ffn_w3"])
        if l + 1 < depth:
            x = _residual_matmul(lay, l, t, wb["ffn_w2"], x, mods, MOD_GATE2, "ffn_down", tm=1024, tn=512)
        else:
            y_p, y_s = (_residual_matmul(lay, l, t, wb["ffn_w2"], x, mods, MOD_GATE2, "ffn_down", tm=1024, tn=512,
                                         rows=r) for r in ((0, lay.p_rows), (lay.p_rows, lay.rows)))
        ks_out.append(kf[:lay.p_rows].reshape(batch, seq, N_KV_A, HEAD_DIM))
        vs_out.append(z[:lay.p_rows, A_WIDTH + KV_WIDTH:A_WIDTH + 2 * KV_WIDTH].reshape(batch, seq, N_KV_A, HEAD_DIM))
        sf_out.append(jnp.swapaxes(s_f[:batch], -1, -2))
        sb_out.append(jnp.swapaxes(s_b[:batch], -1, -2))
    return (y_p.reshape(batch, seq, d), y_s.reshape(dec_batch, dec_seq, d),
            jnp.stack(ks_out, axis=1), jnp.stack(vs_out, axis=1),
            jnp.stack(sf_out, axis=1), jnp.stack(sb_out, axis=1))
```

```python
import functools

import jax
import jax.numpy as jnp
from jax import lax
from jax.experimental import pallas as pl
from jax.experimental.pallas import tpu as pltpu

F32 = jnp.float32
BF16 = jnp.bfloat16

V7X_VMEM_BYTES = 64 * 1024 * 1024
VMEM_LIMIT_BYTES = V7X_VMEM_BYTES - 8 * 1024 * 1024
LANES = 128
SUBLANES = 8
MXU_WIDTH = 256

GRID_W = 64
HEAD_DIM = 128
N_HEADS_A = 8
N_KV_A = 2
GROUP_A = N_HEADS_A // N_KV_A
A_WIDTH = N_HEADS_A * HEAD_DIM
KV_WIDTH = N_KV_A * HEAD_DIM
WINDOW = 128
BLOCK = 128
ROPE_BASE = 10000.0
ROT_AXIS = HEAD_DIM // 2
ATTN_SCALE = HEAD_DIM ** -0.5
NEG_MASK = -1e9
CONF_W = 512
CONF_K = 31
SC_W = 512
SC_K = 3
HG_HEADS = 4
HG_DK = 128
HG_DV = 128
HG_W = HG_HEADS * HG_DK
HG_CHUNK = 32
N_BRANCH = 4
EPS = 1e-6

COL_W = 512
COL_CONF_A, COL_CONF_G, COL_SC_B, COL_SC_C, COL_SC_H = 3, 4, 5, 6, 7
COL_HQ, COL_HF_FWD, COL_HF_BWD, COL_HI, COL_HG = 8, 9, 10, 11, 12
KV_COL_K, KV_COL_V = A_WIDTH // KV_WIDTH, A_WIDTH // KV_WIDTH + 1

ATTN_Q_ROWS = 512
SEQ_TILE = 256
HALO = 16
HG_SAFE_LOG_DECAY = -60.0

MOD_SHIFT1, MOD_SCALE1, MOD_GATE1, MOD_SHIFT2, MOD_SCALE2, MOD_GATE2 = range(6)
MOD_ROWS = 8


def _cparams(*semantics):
    return pltpu.CompilerParams(dimension_semantics=semantics, vmem_limit_bytes=VMEM_LIMIT_BYTES)


def _sigmoid(x):
    return 1.0 / (1.0 + jnp.exp(-x))


def _silu(x):
    return x * _sigmoid(x)


def _chunk_of(idx, size):
    assert size & (size - 1) == 0
    return jnp.right_shift(idx, size.bit_length() - 1)


def _dot(a, b):
    return jnp.dot(a, b, preferred_element_type=F32)


def _dot_nt(a, b):
    return lax.dot_general(a, b, (((1,), (1,)), ((), ())), preferred_element_type=F32)


class _Layout:
    def __init__(self, batch, seq, dec_batch, dec_seq):
        self.batch, self.seq, self.dec_batch, self.dec_seq = batch, seq, dec_batch, dec_seq
        self.p_rows = batch * seq
        self.s_rows = dec_batch * dec_seq
        self.rows = self.p_rows + self.s_rows
        self.n_seq = batch + dec_batch

    def row_tile(self, preferred):
        t = preferred
        while self.p_rows % t or self.dec_seq % t:
            t //= 2
        return t

    def mod_row(self, i, tile):
        p_tiles = self.p_rows // tile
        per = self.dec_seq // tile
        return jnp.where(i < p_tiles, 0, 1 + (i - p_tiles) // per)

    def seq_pos(self, n, tile):
        p_tiles = self.p_rows // tile
        tpp = self.seq // tile
        tps = self.dec_seq // tile
        is_p = n < p_tiles
        seq_idx = jnp.where(is_p, n // tpp, self.batch + (n - p_tiles) // tps)
        pos = jnp.where(is_p, n % tpp, (n - p_tiles) % tps)
        count = jnp.where(is_p, tpp, tps)
        return seq_idx, pos, count


def _mod_kernel(c_ref, w_ref, b_ref, o_ref):
    s = _silu(c_ref[...]).astype(BF16)
    o_ref[...] = _dot(s, w_ref[...].astype(BF16)) + b_ref[...]


def _modulation(cvec, ada_w, ada_b):
    depth, d, n = ada_w.shape
    tn = 1024
    return pl.pallas_call(
        _mod_kernel,
        grid=(depth, n // tn),
        in_specs=[
            pl.BlockSpec((MOD_ROWS, d), lambda l, j: (0, 0)),
            pl.BlockSpec((None, d, tn), lambda l, j: (l, 0, j)),
            pl.BlockSpec((None, 1, tn), lambda l, j: (l, 0, j)),
        ],
        out_specs=pl.BlockSpec((None, MOD_ROWS, tn), lambda l, j: (l, 0, j)),
        out_shape=jax.ShapeDtypeStruct((depth, MOD_ROWS, n), F32),
        compiler_params=_cparams("parallel", "parallel"),
        name="modulation",
    )(cvec, ada_w, ada_b.reshape(depth, 1, n))


NORM_ROWS = 16


def _modulated_norm(x_ref, g_ref, mod_ref, shift_row, scale_row, h_ref):
    gain = g_ref[...] * (1.0 + mod_ref[scale_row:scale_row + 1, :])
    shift = mod_ref[shift_row:shift_row + 1, :]
    block = 8 * NORM_ROWS

    def body(c, carry):
        start = pl.multiple_of(c * block, block)
        for r0 in range(0, block, NORM_ROWS):
            rows = pl.ds(start + r0, NORM_ROWS)
            x = x_ref[rows, :]
            y = x * lax.rsqrt(jnp.mean(x * x, axis=-1, keepdims=True) + EPS)
            h_ref[rows, :] = (y * gain + shift).astype(BF16)
        return carry

    lax.fori_loop(0, x_ref.shape[0] // block, body, 0)


def _split_rows_specs(lay, tm, width, tile_of):
    p_tiles = lay.p_rows // tm
    s_tiles = lay.s_rows // tm
    return [pl.BlockSpec((tm, width), lambda *g: (jnp.minimum(tile_of(*g), p_tiles - 1), 0)),
            pl.BlockSpec((tm, width), lambda *g: (jnp.clip(tile_of(*g) - p_tiles, 0, s_tiles - 1), 0))]


def _inproj_kernel(*refs, p_tiles):
    x_refs, (mod_ref, g_ref, w_ref, z_ref, h_ref) = refs[:-5], refs[-5:]
    i = pl.program_id(0)
    first_step = pl.program_id(1) == 0
    for x_ref, mine in zip(x_refs, (True,) if p_tiles is None else (i < p_tiles, i >= p_tiles)):
        @pl.when(first_step & mine)
        def _():
            _modulated_norm(x_ref, g_ref, mod_ref, MOD_SHIFT1, MOD_SCALE1, h_ref)

    z_ref[...] = _dot(h_ref[...], w_ref[...])


def _layer_of(w, l):
    return w if isinstance(w, tuple) else (w, l)


def _in_projection(lay, l, x, mods, norm_g, w_in):
    w_in, lw = _layer_of(w_in, l)
    pair = isinstance(x, tuple)
    m, d = lay.rows, w_in.shape[1]
    n = w_in.shape[-1]
    tm = lay.row_tile(1024)
    tn = 512
    x_specs = (_split_rows_specs(lay, tm, d, lambda i, j: i) if pair
               else [pl.BlockSpec((tm, d), lambda i, j: (i, 0))])
    return pl.pallas_call(
        functools.partial(_inproj_kernel, p_tiles=lay.p_rows // tm if pair else None),
        grid=(m // tm, n // tn),
        in_specs=x_specs + [
            pl.BlockSpec((None, None, MOD_ROWS, d), lambda i, j: (l, lay.mod_row(i, tm), 0, 0)),
            pl.BlockSpec((None, 1, d), lambda i, j: (l, 0, 0)),
            pl.BlockSpec((None, d, tn), lambda i, j: (lw, 0, j)),
        ],
        out_specs=[
            pl.BlockSpec((tm, tn), lambda i, j: (i, j)),
            pl.BlockSpec((tm, d), lambda i, j: (i, 0)),
        ],
        out_shape=[jax.ShapeDtypeStruct((m, n), F32), jax.ShapeDtypeStruct((m, d), BF16)],
        compiler_params=_cparams("parallel", "arbitrary"),
        name="in_projection",
    )(*(x if pair else (x,)), mods, norm_g.reshape(norm_g.shape[0], 1, d), w_in)


def _residual_matmul_kernel(a_ref, w_ref, x_ref, mod_ref, o_ref, *, gate_row):
    a = a_ref[...]
    chunk = 512
    for c0 in range(0, o_ref.shape[1], chunk):
        sl = slice(c0, c0 + chunk)
        o_ref[:, sl] = x_ref[:, sl] + mod_ref[gate_row:gate_row + 1, sl] * _dot(a, w_ref[:, sl])


def _residual_matmul(lay, l, a, w, x, mods, gate_row, name, tm, tn, rows=None):
    w, lw = _layer_of(w, l)
    m, k = a.shape
    d = w.shape[-1]
    tm = lay.row_tile(tm)
    first, stop = (0, m) if rows is None else rows
    assert first % tm == 0 and stop % tm == 0
    t0 = first // tm
    return pl.pallas_call(
        functools.partial(_residual_matmul_kernel, gate_row=gate_row),
        grid=((stop - first) // tm, d // tn),
        in_specs=[
            pl.BlockSpec((tm, k), lambda i, j: (t0 + i, 0)),
            pl.BlockSpec((None, k, tn), lambda i, j: (lw, 0, j)),
            pl.BlockSpec((tm, tn), lambda i, j: (t0 + i, j)),
            pl.BlockSpec((None, None, MOD_ROWS, tn), lambda i, j: (l, lay.mod_row(t0 + i, tm), 0, j)),
        ],
        out_specs=pl.BlockSpec((tm, tn), lambda i, j: (i, j)),
        out_shape=jax.ShapeDtypeStruct((stop - first, d), F32),
        compiler_params=_cparams("parallel", "arbitrary"),
        name=name,
    )(a, w, x, mods)


def _out_proj_kernel(a_ref, w_ref, *refs, p_tiles):
    x_refs, (mod_ref, g_ref, o_ref, h_ref) = refs[:-4], refs[-4:]
    i = pl.program_id(0)
    a = a_ref[...]
    chunk = 512
    for c0 in range(0, o_ref.shape[1], chunk):
        sl = slice(c0, c0 + chunk)
        update = mod_ref[MOD_GATE1:MOD_GATE1 + 1, sl] * _dot(a, w_ref[:, sl])
        if p_tiles is None:
            o_ref[:, sl] = x_refs[0][:, sl] + update
        else:
            o_ref[:, sl] = jnp.where(i < p_tiles, x_refs[0][:, sl], x_refs[1][:, sl]) + update
    _modulated_norm(o_ref, g_ref, mod_ref, MOD_SHIFT2, MOD_SCALE2, h_ref)


def _out_projection(lay, l, a, w_o, x, mods, norm2_g):
    pair = isinstance(x, tuple)
    m, k = a.shape
    d = w_o.shape[-1]
    tm = lay.row_tile(512)
    x_specs = _split_rows_specs(lay, tm, d, lambda i: i) if pair else [pl.BlockSpec((tm, d), lambda i: (i, 0))]
    return pl.pallas_call(
        functools.partial(_out_proj_kernel, p_tiles=lay.p_rows // tm if pair else None),
        grid=(m // tm,),
        in_specs=[
            pl.BlockSpec((tm, k), lambda i: (i, 0)),
            pl.BlockSpec((None, k, d), lambda i: (l, 0, 0)),
        ] + x_specs + [
            pl.BlockSpec((None, None, MOD_ROWS, d), lambda i: (l, lay.mod_row(i, tm), 0, 0)),
            pl.BlockSpec((None, 1, d), lambda i: (l, 0, 0)),
        ],
        out_specs=[pl.BlockSpec((tm, d), lambda i: (i, 0)), pl.BlockSpec((tm, d), lambda i: (i, 0))],
        out_shape=[jax.ShapeDtypeStruct((m, d), F32), jax.ShapeDtypeStruct((m, d), BF16)],
        compiler_params=_cparams("parallel"),
        name="out_projection",
    )(a, w_o, *(x if pair else (x,)), mods, norm2_g.reshape(norm2_g.shape[0], 1, d))


def _ffn_up_kernel(h_ref, w1_ref, w3_ref, o_ref):
    h = h_ref[...]
    tn = o_ref.shape[1]
    tail = tn % MXU_WIDTH
    c0 = 0
    while c0 < tn - tail:
        c1 = min(c0 + 512, tn - tail)
        o_ref[:, c0:c1] = (_silu(_dot(h, w1_ref[:, c0:c1])) * _dot(h, w3_ref[:, c0:c1])).astype(BF16)
        c0 = c1
    if tail:
        both = _dot(h, jnp.concatenate([w1_ref[:, c0:], w3_ref[:, c0:]], axis=1))
        o_ref[:, c0:] = (_silu(both[:, :tail]) * both[:, tail:]).astype(BF16)


def _ffn_up(lay, l, h, w1, w3):
    (w1, l1), (w3, l3) = _layer_of(w1, l), _layer_of(w3, l)
    m, d = h.shape
    n = w1.shape[-1]
    tm = lay.row_tile(1024)
    tn = n // 4
    assert n % 4 == 0 and tn % LANES == 0
    return pl.pallas_call(
        _ffn_up_kernel,
        grid=(m // tm, n // tn),
        in_specs=[
            pl.BlockSpec((tm, d), lambda i, j: (i, 0)),
            pl.BlockSpec((None, d, tn), lambda i, j: (l1, 0, j)),
            pl.BlockSpec((None, d, tn), lambda i, j: (l3, 0, j)),
        ],
        out_specs=pl.BlockSpec((tm, tn), lambda i, j: (i, j)),
        out_shape=jax.ShapeDtypeStruct((m, n), BF16),
        compiler_params=_cparams("parallel", "parallel"),
        name="ffn_up",
    )(h, w1, w3)


def _qk_prep_kernel(q_ref, k_ref, cos_ref, sa_ref, sb_ref, qg_ref, kg_ref, qo_ref, ko_ref, kf_ref):
    cos, sin_a, sin_b = cos_ref[...], sa_ref[...], sb_ref[...]

    ones = jnp.ones((HEAD_DIM, HEAD_DIM), BF16)

    def norm(x, g):
        mean_sq = _dot((x * x).astype(BF16), ones) * (1.0 / HEAD_DIM)
        return x * lax.rsqrt(mean_sq + EPS) * g

    def rope(x):
        return (x * cos + pltpu.roll(x, HEAD_DIM - ROT_AXIS // 2, 1) * sin_a
                + pltpu.roll(x, ROT_AXIS // 2, 1) * sin_b)

    for h in range(N_HEADS_A):
        sl = slice(h * HEAD_DIM, (h + 1) * HEAD_DIM)
        qo_ref[:, sl] = (rope(norm(q_ref[:, sl], qg_ref[...])) * ATTN_SCALE).astype(BF16)
    for h in range(N_KV_A):
        sl = slice(h * HEAD_DIM, (h + 1) * HEAD_DIM)
        kn = norm(k_ref[:, sl], kg_ref[...])
        kf_ref[:, sl] = kn
        ko_ref[:, sl] = rope(kn).astype(BF16)


def _qk_prep(lay, l, z, rope_tabs, q_norm_g, k_norm_g):
    m = z.shape[0]
    tm = lay.row_tile(512)
    cos, sin_a, sin_b = rope_tabs
    tab_spec = pl.BlockSpec((tm, HEAD_DIM), lambda i: (i, 0))
    g_spec = pl.BlockSpec((None, 1, HEAD_DIM), lambda i: (l, 0, 0))
    return pl.pallas_call(
        _qk_prep_kernel,
        grid=(m // tm,),
        in_specs=[
            pl.BlockSpec((tm, A_WIDTH), lambda i: (i, 0)),
            pl.BlockSpec((tm, KV_WIDTH), lambda i: (i, KV_COL_K)),
            tab_spec, tab_spec, tab_spec, g_spec, g_spec,
        ],
        out_specs=[
            pl.BlockSpec((tm, A_WIDTH), lambda i: (i, 0)),
            pl.BlockSpec((tm, KV_WIDTH), lambda i: (i, 0)),
            pl.BlockSpec((tm, KV_WIDTH), lambda i: (i, 0)),
        ],
        out_shape=[
            jax.ShapeDtypeStruct((m, A_WIDTH), BF16),
            jax.ShapeDtypeStruct((m, KV_WIDTH), BF16),
            jax.ShapeDtypeStruct((m, KV_WIDTH), F32),
        ],
        compiler_params=_cparams("parallel"),
        name="qk_prep",
    )(z, z, cos, sin_a, sin_b, q_norm_g.reshape(-1, 1, HEAD_DIM), k_norm_g.reshape(-1, 1, HEAD_DIM))


def _stack_heads(q_ref, kv):
    return jnp.concatenate(
        [q_ref[:, (kv * GROUP_A + g) * HEAD_DIM:(kv * GROUP_A + g + 1) * HEAD_DIM] for g in range(GROUP_A)], axis=0)


def _fold_lane_tiles(x, op):
    out = x[:, :LANES]
    for c0 in range(LANES, x.shape[1], LANES):
        out = op(out, x[:, c0:c0 + LANES])
    return out


def _sink_column(sink_ref, kv, rows):
    return jnp.concatenate(
        [jnp.full((rows, 1), sink_ref[kv * GROUP_A + g], F32) for g in range(GROUP_A)], axis=0)


def _context_attention_kernel(sink_ref, q_ref, k_ref, v_ref, o_ref, *, seq):
    for r0 in range(0, q_ref.shape[0], seq):
        rows = slice(r0, r0 + seq)
        for kv in range(N_KV_A):
            sl = slice(kv * HEAD_DIM, (kv + 1) * HEAD_DIM)
            q = jnp.concatenate(
                [q_ref[rows, (kv * GROUP_A + g) * HEAD_DIM:(kv * GROUP_A + g + 1) * HEAD_DIM]
                 for g in range(GROUP_A)], axis=0)
            s = _dot_nt(q, k_ref[rows, sl])
            sink = _sink_column(sink_ref, kv, seq)
            mx = jnp.maximum(jnp.max(_fold_lane_tiles(s, jnp.maximum), axis=-1, keepdims=True), sink)
            p = jnp.exp(s - mx)
            den = jnp.sum(_fold_lane_tiles(p, jnp.add), axis=-1, keepdims=True) + jnp.exp(sink - mx)
            o = _dot(p.astype(BF16), v_ref[rows, sl].astype(BF16)) * (1.0 / den)
            for g in range(GROUP_A):
                h = kv * GROUP_A + g
                o_ref[rows, h * HEAD_DIM:(h + 1) * HEAD_DIM] = o[g * seq:(g + 1) * seq].astype(BF16)


def _context_attention(lay, l, qn, kn, z, sink):
    per_step = next(n for n in (4, 2, 1) if lay.batch % n == 0)
    seq = lay.seq
    rows = per_step * seq
    return pl.pallas_call(
        functools.partial(_context_attention_kernel, seq=seq),
        grid=(lay.batch // per_step,),
        in_specs=[
            pl.BlockSpec(memory_space=pltpu.SMEM),
            pl.BlockSpec((rows, A_WIDTH), lambda b: (b, 0)),
            pl.BlockSpec((rows, KV_WIDTH), lambda b: (b, 0)),
            pl.BlockSpec((rows, KV_WIDTH), lambda b: (b, KV_COL_V)),
        ],
        out_specs=pl.BlockSpec((rows, A_WIDTH), lambda b: (b, 0)),
        out_shape=jax.ShapeDtypeStruct((lay.p_rows, A_WIDTH), BF16),
        compiler_params=_cparams("parallel"),
        name="context_attention",
    )(sink, qn, kn, z)


def _latent_attention_kernel(sink_ref, q_ref, kp_ref, kc_ref, kn_ref, vp_ref, vc_ref, vn_ref, ck_ref, cv_ref,
                             o_ref, *, seq_len):
    assert WINDOW <= BLOCK
    q_rows = q_ref.shape[0]
    j = pl.program_id(1)
    n_tiles = seq_len // q_rows
    r = lax.broadcasted_iota(jnp.int32, (q_rows, BLOCK), 0)
    c = lax.broadcasted_iota(jnp.int32, (q_rows, BLOCK), 1)
    valid_prev = jnp.concatenate([(r - (c - BLOCK) <= WINDOW) & (j >= 1)] * GROUP_A, axis=0)
    valid_next = jnp.concatenate([((c + q_rows) - r <= WINDOW) & (j <= n_tiles - 2)] * GROUP_A, axis=0)
    valid_own = None
    if q_rows - 1 > WINDOW:
        ro = lax.broadcasted_iota(jnp.int32, (q_rows, q_rows), 0)
        co = lax.broadcasted_iota(jnp.int32, (q_rows, q_rows), 1)
        valid_own = jnp.concatenate([jnp.abs(co - ro) <= WINDOW] * GROUP_A, axis=0)
    for kv in range(N_KV_A):
        sl = slice(kv * HEAD_DIM, (kv + 1) * HEAD_DIM)
        q = _stack_heads(q_ref, kv)
        s_p = jnp.where(valid_prev, _dot_nt(q, kp_ref[:, sl]), NEG_MASK)
        s_o = _dot_nt(q, kc_ref[:, sl])
        if valid_own is not None:
            s_o = jnp.where(valid_own, s_o, NEG_MASK)
        s_n = jnp.where(valid_next, _dot_nt(q, kn_ref[:, sl]), NEG_MASK)
        s_c = _dot_nt(q, ck_ref[:, sl].astype(BF16))
        sink = _sink_column(sink_ref, kv, q_rows)
        mx = jnp.maximum(jnp.maximum(s_p, _fold_lane_tiles(s_o, jnp.maximum)),
                         jnp.maximum(s_n, _fold_lane_tiles(s_c, jnp.maximum)))
        mx = jnp.maximum(jnp.max(mx, axis=-1, keepdims=True), sink)
        psum = None
        o = None
        for s, v_ref in ((s_p, vp_ref), (s_o, vc_ref), (s_n, vn_ref), (s_c, cv_ref)):
            p = jnp.exp(s - mx)
            pf = _fold_lane_tiles(p, jnp.add)
            psum = pf if psum is None else psum + pf
            term = _dot(p.astype(BF16), v_ref[:, sl].astype(BF16))
            o = term if o is None else o + term
        den = jnp.sum(psum, axis=-1, keepdims=True) + jnp.exp(sink - mx)
        o = o * (1.0 / den)
        for g in range(GROUP_A):
            h = kv * GROUP_A + g
            o_ref[:, h * HEAD_DIM:(h + 1) * HEAD_DIM] = o[g * q_rows:(g + 1) * q_rows].astype(BF16)


def _latent_attention(lay, l, qn, kn, z, ctx_k, ctx_v, sink):
    q_rows = ATTN_Q_ROWS
    per_tile = q_rows // BLOCK
    assert lay.p_rows % q_rows == 0 and lay.dec_seq % q_rows == 0
    nb = lay.dec_seq // BLOCK
    nt = lay.dec_seq // q_rows
    base = lay.p_rows // BLOCK
    past = ctx_k.shape[2]

    def own(col):
        return lambda b, j: (lay.p_rows // q_rows + b * nt + j, col)

    def prev(col):
        return lambda b, j: (base + b * nb + jnp.maximum(j * per_tile - 1, 0), col)

    def nxt(col):
        return lambda b, j: (base + b * nb + jnp.minimum((j + 1) * per_tile, nb - 1), col)

    def trio(col):
        return [pl.BlockSpec((BLOCK, KV_WIDTH), prev(col)), pl.BlockSpec((q_rows, KV_WIDTH), own(col)),
                pl.BlockSpec((BLOCK, KV_WIDTH), nxt(col))]

    k_spec = trio(0)
    v_spec = trio(KV_COL_V)
    ctx_spec = pl.BlockSpec((None, None, past, KV_WIDTH), lambda b, j: (b, l, 0, 0))
    return pl.pallas_call(
        functools.partial(_latent_attention_kernel, seq_len=lay.dec_seq),
        grid=(lay.dec_batch, nt),
        in_specs=[pl.BlockSpec(memory_space=pltpu.SMEM), pl.BlockSpec((q_rows, A_WIDTH), own(0))]
        + k_spec + v_spec + [ctx_spec, ctx_spec],
        out_specs=pl.BlockSpec((q_rows, A_WIDTH), lambda b, j: (b * nt + j, 0)),
        out_shape=jax.ShapeDtypeStruct((lay.s_rows, A_WIDTH), BF16),
        compiler_params=_cparams("parallel", "parallel"),
        name="latent_attention",
    )(sink, qn, kn, kn, kn, z, z, z, ctx_k, ctx_v)


def _conv_kernel(a_c, a_p, a_n, g_c, g_p, g_n, b_c, c_c, c_p, c_n, h_c, h_p, h_n,
                 dw_ref, db_ref, lg_ref, lb_ref, sw_ref, ub_ref, uc_ref, xb_s, xc_s, *, lay):
    _, pos, count = lay.seq_pos(pl.program_id(0), SEQ_TILE)
    has_prev = pos > 0
    has_next = pos < count - 1
    rb = 32
    xb_s[0, 0:HALO, :] = jnp.where(has_prev, a_p[...] * _sigmoid(g_p[...]), 0.0)
    xb_s[0, HALO + SEQ_TILE:, :] = jnp.where(has_next, a_n[...] * _sigmoid(g_n[...]), 0.0)
    xc_s[0:HALO, :] = jnp.where(has_prev, c_p[...] * h_p[...], 0.0)
    xc_s[HALO + SEQ_TILE:, :] = jnp.where(has_next, c_n[...] * h_n[...], 0.0)
    for r0 in range(0, SEQ_TILE, rb):
        rows = slice(r0, r0 + rb)
        xb_s[0, HALO + r0:HALO + r0 + rb, :] = a_c[rows, :] * _sigmoid(g_c[rows, :])
        xc_s[HALO + r0:HALO + r0 + rb, :] = c_c[rows, :] * h_c[rows, :]
    shifted_rows = SEQ_TILE + 2 * HALO - SUBLANES
    chunk = 40
    for s in range(1, SUBLANES):
        for r0 in range(0, shifted_rows, chunk):
            xb_s[s, r0:r0 + chunk, :] = xb_s[0, r0 + s:r0 + s + chunk, :]
    for r0 in range(0, SEQ_TILE, rb):
        acc = jnp.zeros((rb // SUBLANES, SUBLANES, CONF_W), F32)
        for k in range(CONF_K):
            off = HALO - CONF_K // 2 + k
            row = r0 + off - off % SUBLANES
            x = xb_s[off % SUBLANES, row:row + rb, :]
            acc = acc + dw_ref[k] * x.reshape(rb // SUBLANES, SUBLANES, CONF_W)
        u = acc.reshape(rb, CONF_W) + db_ref[...]
        uc = u - jnp.mean(u, axis=-1, keepdims=True)
        var = jnp.mean(uc * uc, axis=-1, keepdims=True)
        u = uc * lax.rsqrt(var + EPS) * lg_ref[...] + lb_ref[...]
        ub_ref[r0:r0 + rb, :] = _silu(u).astype(BF16)
        acc = jnp.zeros((rb // SUBLANES, SUBLANES, SC_W), F32)
        for k in range(SC_K):
            off = HALO - SC_K // 2 + r0 + k
            acc = acc + sw_ref[k] * xc_s[off:off + rb, :].reshape(rb // SUBLANES, SUBLANES, SC_W)
        uc_ref[r0:r0 + rb, :] = (b_c[r0:r0 + rb, :] * acc.reshape(rb, SC_W)).astype(BF16)


def _conv_mixers(lay, l, z, conf_dw_w, conf_dw_b, conf_ln_g, conf_ln_b, sc_conv_w):
    m = z.shape[0]
    n_tiles = m // SEQ_TILE
    per = SEQ_TILE // HALO
    n_halo = m // HALO

    def cur(col):
        return pl.BlockSpec((SEQ_TILE, COL_W), lambda i: (i, col))

    def prev(col):
        return pl.BlockSpec((HALO, COL_W), lambda i: (jnp.maximum(i * per - 1, 0), col))

    def nxt(col):
        return pl.BlockSpec((HALO, COL_W), lambda i: (jnp.minimum((i + 1) * per, n_halo - 1), col))

    def trio(col):
        return [cur(col), prev(col), nxt(col)]

    def par(rows):
        return pl.BlockSpec((None, rows, COL_W), lambda i: (l, 0, 0))

    def taps(n):
        return pl.BlockSpec((None, n, SUBLANES, COL_W), lambda i: (l, 0, 0, 0))

    depth = conf_dw_w.shape[0]
    dw = jnp.broadcast_to(conf_dw_w[:, :, None, :], (depth, CONF_K, SUBLANES, CONF_W))
    sw = jnp.broadcast_to(sc_conv_w[:, :, None, :], (depth, SC_K, SUBLANES, SC_W))
    return pl.pallas_call(
        functools.partial(_conv_kernel, lay=lay),
        grid=(n_tiles,),
        in_specs=trio(COL_CONF_A) + trio(COL_CONF_G) + [cur(COL_SC_B)] + trio(COL_SC_C) + trio(COL_SC_H)
        + [taps(CONF_K), par(1), par(1), par(1), taps(SC_K)],
        out_specs=[pl.BlockSpec((SEQ_TILE, CONF_W), lambda i: (i, 0)),
                   pl.BlockSpec((SEQ_TILE, SC_W), lambda i: (i, 0))],
        out_shape=[jax.ShapeDtypeStruct((m, CONF_W), BF16), jax.ShapeDtypeStruct((m, SC_W), BF16)],
        scratch_shapes=[pltpu.VMEM((SUBLANES, SEQ_TILE + 2 * HALO, CONF_W), F32),
                        pltpu.VMEM((SEQ_TILE + 2 * HALO, SC_W), F32)],
        compiler_params=_cparams("parallel"),
        name="conv_mixers",
    )(*([z] * 13), dw, conf_dw_b.reshape(depth, 1, CONF_W), conf_ln_g.reshape(depth, 1, CONF_W),
      conf_ln_b.reshape(depth, 1, CONF_W), sw)


def _hgrn_gates(hq_ref, hf_ref, lb_ref, sl, tri_b3):
    zq = hq_ref[:, sl]
    q = _silu(zq) * (HG_DK ** -0.5)
    zf = hf_ref[:, sl]
    lb = lb_ref[:, sl]
    e = jnp.exp(-jnp.abs(zf))
    r = 1.0 / (1.0 + e)
    nonneg = zf >= 0
    sig_pos = jnp.where(nonneg, r, e * r)
    sig_neg = jnp.where(nonneg, e * r, r)
    logf = jnp.log(lb + (1.0 - lb) * sig_pos)
    kk = (1.0 - lb) * sig_neg
    g1 = logf.astype(BF16)
    r1 = logf - g1.astype(F32)
    g2 = r1.astype(BF16)
    g3 = (r1 - g2.astype(F32)).astype(BF16)
    b = _dot(tri_b3, jnp.concatenate([g1, g2, g3], axis=0))
    return q, kk, b


def _block_place(x, n_chunks):
    w = x.shape[1]
    zeros = jnp.zeros((HG_CHUNK, w), x.dtype)
    row_blocks = []
    for c in range(n_chunks):
        cols = [zeros] * n_chunks
        cols[c] = x[c * HG_CHUNK:(c + 1) * HG_CHUNK, :]
        row_blocks.append(jnp.concatenate(cols, axis=1))
    return jnp.concatenate(row_blocks, axis=0)


def _hgrn_exact_scores(a_ref, q_s, k_s, b_s, reverse):
    ts = a_ref.shape[0]
    a_ref[...] = jnp.zeros_like(a_ref)

    def body(s, carry):
        c0 = pl.multiple_of((s // HG_CHUNK) * HG_CHUNK, HG_CHUNK)
        w = jnp.exp(jnp.minimum(b_s[pl.ds(c0, HG_CHUNK), :] - b_s[pl.ds(s, 1), :], 0.0))
        prod = q_s[pl.ds(c0, HG_CHUNK), :] * k_s[pl.ds(s, 1), :] * w
        trow = c0 + lax.broadcasted_iota(jnp.int32, (HG_CHUNK, 1), 0)
        ok = (trow <= s) if reverse else (trow >= s)
        onehot = (lax.broadcasted_iota(jnp.int32, (1, ts), 1) == s).astype(F32)
        colv = jnp.sum(prod, axis=-1, keepdims=True)
        a_ref[pl.ds(c0, HG_CHUNK), :] += jnp.where(ok, colv, 0.0) * onehot
        return carry

    lax.fori_loop(0, ts, body, 0)


def _hgrn_kernel(hq_ref, hf_ref, hi_ref, lb_ref, s0_ref, *refs, lay, reverse, finish):
    ts = SEQ_TILE
    n_chunks = ts // HG_CHUNK
    _, pos, count = lay.seq_pos(pl.program_id(0), ts)
    other_ref, gate_ref, gain_ref = refs[:3] if finish else (None, None, None)
    o_ref, sf_ref, st_ref, oi_s, a_ref, q_s, k_s, b_s = refs[3:] if finish else refs

    def emit(o, sl):
        if finish:
            o = o + other_ref[:, sl]
            o = o * lax.rsqrt(jnp.mean(o * o, axis=-1, keepdims=True) + EPS) * gain_ref[...]
            o_ref[:, sl] = (o * _silu(gate_ref[:, sl])).astype(BF16)
        else:
            o_ref[:, sl] = o

    @pl.when(pos == 0)
    def _():
        st_ref[...] = s0_ref[...]

    row = lax.broadcasted_iota(jnp.int32, (ts, ts), 0)
    col = lax.broadcasted_iota(jnp.int32, (ts, ts), 1)
    same = _chunk_of(row, HG_CHUNK) == _chunk_of(col, HG_CHUNK)
    tri = same & ((col >= row) if reverse else (col <= row))
    tri_b = tri.astype(F32).astype(BF16)
    tri_b3 = jnp.concatenate([tri_b] * 3, axis=1)
    order = range(n_chunks - 1, -1, -1) if reverse else range(n_chunks)
    last = 0 if reverse else HG_CHUNK - 1

    b_min = None
    for hd in range(HG_HEADS):
        sl = slice(hd * HG_DK, (hd + 1) * HG_DK)
        q, kk, b = _hgrn_gates(hq_ref, hf_ref, lb_ref, sl, tri_b3)
        head_min = jnp.min(b)
        b_min = head_min if b_min is None else jnp.minimum(b_min, head_min)
        b3 = b.reshape(n_chunks, HG_CHUNK, HG_DK)
        total3 = b3[:, last:last + 1, :]
        qt = (q * jnp.exp(b)).astype(BF16)
        kend = (kk * jnp.exp(jnp.broadcast_to(total3, b3.shape).reshape(ts, HG_DK) - b)).astype(BF16)
        decay = jnp.exp(total3.reshape(n_chunks, HG_DK))
        v_h = hi_ref[:, sl]
        kinv = (kk * jnp.exp(-b)).astype(BF16)
        scores = jnp.where(tri, _dot_nt(qt, kinv), 0.0).astype(BF16)
        o_intra = _dot(scores, v_h.astype(BF16))
        incr = _dot(v_h.T.astype(BF16), _block_place(kend, n_chunks))
        state = st_ref[hd]
        starts = [None] * n_chunks
        for c in order:
            starts[c] = state
            state = state * decay[c:c + 1, :] + incr[:, c * HG_DK:(c + 1) * HG_DK]
        st_ref[hd] = state
        o_inter = jnp.concatenate(
            [_dot_nt(qt[c * HG_CHUNK:(c + 1) * HG_CHUNK, :], starts[c].astype(BF16)) for c in range(n_chunks)],
            axis=0)
        oi_s[:, sl] = o_inter
        emit(o_intra + o_inter, sl)

    @pl.when(b_min <= HG_SAFE_LOG_DECAY)
    def _():
        for hd in range(HG_HEADS):
            sl = slice(hd * HG_DK, (hd + 1) * HG_DK)
            q, kk, b = _hgrn_gates(hq_ref, hf_ref, lb_ref, sl, tri_b3)
            q_s[...] = q
            k_s[...] = kk
            b_s[...] = b
            _hgrn_exact_scores(a_ref, q_s, k_s, b_s, reverse)
            emit(_dot(a_ref[...].astype(BF16), hi_ref[:, sl].astype(BF16)) + oi_s[:, sl], sl)

    @pl.when(pos == count - 1)
    def _():
        sf_ref[...] = st_ref[...]


def _hgrn_direction(lay, l, z, lb, s0_t, reverse, other=None, norm_g=None):
    finish = other is not None
    m = z.shape[0]
    n_tiles = m // SEQ_TILE

    def tile_of(n):
        seq_idx, pos, count = lay.seq_pos(n, SEQ_TILE)
        return (n - pos + (count - 1 - pos)) if reverse else n

    def zcol(col):
        return pl.BlockSpec((SEQ_TILE, COL_W), lambda n: (tile_of(n), col))

    state_spec = pl.BlockSpec((None, HG_HEADS, HG_DV, HG_DK), lambda n: (lay.seq_pos(n, SEQ_TILE)[0], 0, 0, 0))
    o_spec = pl.BlockSpec((SEQ_TILE, HG_W), lambda n: (tile_of(n), 0))
    finish_specs, finish_args = [], []
    if finish:
        finish_specs = [o_spec, zcol(COL_HG), pl.BlockSpec((None, 1, HG_DV), lambda n: (l, 0, 0))]
        finish_args = [other, z, norm_g.reshape(-1, 1, HG_DV)]
    return pl.pallas_call(
        functools.partial(_hgrn_kernel, lay=lay, reverse=reverse, finish=finish),
        grid=(n_tiles,),
        in_specs=[zcol(COL_HQ), zcol(COL_HF_BWD if reverse else COL_HF_FWD), zcol(COL_HI),
                  pl.BlockSpec((1, HG_W), lambda n: (0, 0)), state_spec] + finish_specs,
        out_specs=[o_spec, state_spec],
        out_shape=[jax.ShapeDtypeStruct((m, HG_W), BF16 if finish else F32),
                   jax.ShapeDtypeStruct((lay.n_seq, HG_HEADS, HG_DV, HG_DK), F32)],
        scratch_shapes=[pltpu.VMEM((HG_HEADS, HG_DV, HG_DK), F32),
                        pltpu.VMEM((SEQ_TILE, HG_W), F32),
                        pltpu.VMEM((SEQ_TILE, SEQ_TILE), F32),
                        pltpu.VMEM((SEQ_TILE, HG_DK), F32),
                        pltpu.VMEM((SEQ_TILE, HG_DK), F32),
                        pltpu.VMEM((SEQ_TILE, HG_DK), F32)],
        compiler_params=_cparams("arbitrary"),
        name="hgrn_bwd" if reverse else "hgrn_fwd",
    )(z, z, z, lb, s0_t, *finish_args)


def _merge_kernel(h_ref, ap_ref, as_ref, b_ref, c_ref, d_ref, wg0, wg1, wg2, wg3, bg0, bg1, bg2, bg3,
                  wa_ref, wb_ref, wc_ref, wd_ref, *refs, p_tiles, cast_steps):
    n_cast = len(cast_steps)
    src_refs, o_ref, dst_refs = refs[:n_cast], refs[n_cast], refs[n_cast + 1:]
    step = pl.program_id(0) * pl.num_programs(1) + pl.program_id(1)
    for src_ref, dst_ref, n_steps in zip(src_refs, dst_refs, cast_steps):
        @pl.when(step < n_steps)
        def _():
            dst_ref[...] = src_ref[...].astype(BF16)

    h = h_ref[...]
    attn = jnp.where(pl.program_id(1) < p_tiles, ap_ref[...], as_ref[...])
    acc = None
    for branch, w_ref, wg_ref, bg_ref in ((attn, wa_ref, wg0, bg0), (b_ref[...], wb_ref, wg1, bg1),
                                          (c_ref[...], wc_ref, wg2, bg2), (d_ref[...], wd_ref, wg3, bg3)):
        gate = _sigmoid(_dot(h, wg_ref[...]) + bg_ref[...])
        term = gate * _dot(branch, w_ref[...])
        acc = term if acc is None else acc + term
    o_ref[...] = acc.astype(BF16)


def _cast_plan(rows, n_steps):
    for steps in range(n_steps, 0, -1):
        slab = rows // steps
        if rows % steps == 0 and slab % (2 * SUBLANES) == 0:
            return slab, steps
    raise ValueError("no slab size for %d weight rows in %d steps" % (rows, n_steps))


def _merge(lay, l, h, attn_p, attn_s, branches, w_gate, b_gate, w_outs, cast_next=()):
    w_gate, lg = _layer_of(w_gate, l)
    m, d = h.shape
    tm = lay.row_tile(512)
    tn = 512
    nj = d // tn
    n_i = m // tm
    depth = b_gate.shape[0]
    p_tiles = lay.p_rows // tm
    s_tiles = lay.s_rows // tm

    def gate_w(n):
        return pl.BlockSpec((None, d, tn), lambda j, i: (lg, 0, n * nj + j))

    def gate_b(n):
        return pl.BlockSpec((None, 1, tn), lambda j, i: (l, 0, n * nj + j))

    cast_specs, cast_out_specs, cast_shapes, cast_steps = [], [], [], []
    for w in cast_next:
        _, rows, cols = w.shape
        slab, steps = _cast_plan(rows, nj * n_i)

        def slab_of(j, i, steps=steps):
            return jnp.minimum(j * n_i + i, steps - 1)

        cast_specs.append(pl.BlockSpec((None, slab, cols), lambda j, i, f=slab_of: (l + 1, f(j, i), 0)))
        cast_out_specs.append(pl.BlockSpec((None, slab, cols), lambda j, i, f=slab_of: (0, f(j, i), 0)))
        cast_shapes.append(jax.ShapeDtypeStruct((1, rows, cols), BF16))
        cast_steps.append(steps)

    b_gate3 = b_gate.reshape(depth, 1, N_BRANCH * d)
    out = pl.pallas_call(
        functools.partial(_merge_kernel, p_tiles=p_tiles, cast_steps=tuple(cast_steps)),
        grid=(nj, n_i),
        in_specs=[pl.BlockSpec((tm, d), lambda j, i: (i, 0)),
                  pl.BlockSpec((tm, A_WIDTH), lambda j, i: (jnp.minimum(i, p_tiles - 1), 0)),
                  pl.BlockSpec((tm, A_WIDTH), lambda j, i: (jnp.clip(i - p_tiles, 0, s_tiles - 1), 0))]
        + [pl.BlockSpec((tm, br.shape[1]), lambda j, i: (i, 0)) for br in branches]
        + [gate_w(n) for n in range(N_BRANCH)] + [gate_b(n) for n in range(N_BRANCH)]
        + [pl.BlockSpec((None, w.shape[1], tn), lambda j, i: (l, 0, j)) for w in w_outs]
        + cast_specs,
        out_specs=[pl.BlockSpec((tm, tn), lambda j, i: (i, j))] + cast_out_specs,
        out_shape=[jax.ShapeDtypeStruct((m, d), BF16)] + cast_shapes,
        compiler_params=_cparams("arbitrary", "arbitrary"),
        name="merge",
    )(h, attn_p, attn_s, *branches, *([w_gate] * N_BRANCH), *([b_gate3] * N_BRANCH), *w_outs, *cast_next)
    return out[0], out[1:]


def _rope_tables(lay):
    t = lay.dec_seq
    half = ROT_AXIS // 2
    rows = t // GRID_W
    row = jnp.repeat(jnp.arange(rows, dtype=F32), GRID_W)
    col = jnp.tile(jnp.arange(GRID_W, dtype=F32), rows)
    inv = ROPE_BASE ** (-jnp.arange(0, ROT_AXIS, 2, dtype=F32) / ROT_AXIS)
    ang = jnp.concatenate([row[:, None] * inv, row[:, None] * inv, col[:, None] * inv, col[:, None] * inv], axis=1)
    lane = jnp.arange(HEAD_DIM)
    first_half = (lane % ROT_AXIS) < half
    cos = jnp.cos(ang)
    sin = jnp.sin(ang)
    sin_a = jnp.where(first_half, -sin, 0.0)
    sin_b = jnp.where(first_half, 0.0, sin)

    def full(tab, fill):
        return jnp.concatenate([jnp.full((lay.p_rows, HEAD_DIM), fill, F32)] + [tab] * lay.dec_batch, axis=0)

    return full(cos, 1.0), full(sin_a, 0.0), full(sin_b, 0.0)


def kernel(x_prompt, x_sample, cache_k, cache_v, state_hgrn_fwd, state_hgrn_bwd, c, c_ctx, ada_w, ada_b, norm1_g, norm2_g, w_in, q_norm_g, k_norm_g, attn_sink, w_attn_out, conf_dw_w, conf_dw_b, conf_ln_g, conf_ln_b, w_conf_out, sc_conv_w, w_sc_out, hg_lb, hg_norm_g, w_hg_out, w_gate, b_gate, w_o, ffn_w1, ffn_w3, ffn_w2):
    batch, seq, d = x_prompt.shape
    dec_batch, dec_seq, _ = x_sample.shape
    depth = ada_w.shape[0]
    past = cache_k.shape[2]
    lay = _Layout(batch, seq, dec_batch, dec_seq)
    assert seq % SEQ_TILE == 0 and dec_seq % SEQ_TILE == 0 and dec_seq % GRID_W == 0
    assert 1 + dec_batch <= MOD_ROWS

    wb = {name: w.astype(BF16) for name, w in dict(
        w_attn_out=w_attn_out, w_conf_out=w_conf_out, w_sc_out=w_sc_out, w_hg_out=w_hg_out, w_o=w_o).items()}
    big = dict(w_gate=w_gate, w_in=w_in, ffn_w1=ffn_w1, ffn_w3=ffn_w3, ffn_w2=ffn_w2)
    wl = {name: (w[:1].astype(BF16), 0) for name, w in big.items()}

    cvec = jnp.concatenate([c_ctx[None], c, jnp.zeros((MOD_ROWS - 1 - dec_batch, d), F32)], axis=0)
    mods = _modulation(cvec, ada_w, ada_b)
    mods = mods.reshape(depth, MOD_ROWS, 6, d)[:, :1 + dec_batch]
    mods = jnp.pad(mods, ((0, 0), (0, 0), (0, MOD_ROWS - 6), (0, 0)))

    lb = jax.nn.softmax(hg_lb.astype(F32), axis=1)
    lb = jnp.maximum(jnp.cumsum(lb, axis=1) - lb[:, :1], 0.0)

    rope_tabs = _rope_tables(lay)
    ctx_k = cache_k.reshape(dec_batch, depth, past, KV_WIDTH)
    ctx_v = cache_v.reshape(dec_batch, depth, past, KV_WIDTH)
    zero_states = jnp.zeros((batch, HG_HEADS, HG_DV, HG_DK), F32)

    x = (x_prompt.reshape(lay.p_rows, d), x_sample.reshape(lay.s_rows, d))
    ks_out, vs_out, sf_out, sb_out = [], [], [], []
    for l in range(depth):
        z, h = _in_projection(lay, l, x, mods, norm1_g, wl["w_in"])
        qn, kn, kf = _qk_prep(lay, l, z, rope_tabs, q_norm_g, k_norm_g)
        attn_p = _context_attention(lay, l, qn, kn, z, attn_sink[l])
        attn_s = _latent_attention(lay, l, qn, kn, z, ctx_k, ctx_v, attn_sink[l])
        u_b, u_c = _conv_mixers(lay, l, z, conf_dw_w, conf_dw_b, conf_ln_g, conf_ln_b, sc_conv_w)
        s0_f = jnp.concatenate([zero_states, jnp.swapaxes(state_hgrn_fwd[:, l], -1, -2)], axis=0)
        s0_b = jnp.concatenate([zero_states, jnp.swapaxes(state_hgrn_bwd[:, l], -1, -2)], axis=0)
        o_f, s_f = _hgrn_direction(lay, l, z, lb[0, l][None], s0_f, reverse=False)
        o_d, s_b = _hgrn_direction(lay, l, z, lb[1, l][None], s0_b, reverse=True, other=o_f, norm_g=hg_norm_g)
        merged, cast = _merge(lay, l, h, attn_p, attn_s, (u_b, u_c, o_d), wl["w_gate"], b_gate,
                              (wb["w_attn_out"], wb["w_conf_out"], wb["w_sc_out"], wb["w_hg_out"]),
                              cast_next=tuple(big.values()) if l + 1 < depth else ())
        x, h2 = _out_projection(lay, l, merged, wb["w_o"], x, mods, norm2_g)
        t = _ffn_up(lay, l, h2, wl["ffn_w1"], wl["ffn_w3"])
        if l + 1 < depth:
            x = _residual_matmul(lay, l, t, wl["ffn_w2"], x, mods, MOD_GATE2, "ffn_down", tm=1024, tn=512)
            wl = {name: (w, 0) for name, w in zip(big, cast)}
        else:
            y_p, y_s = (_residual_matmul(lay, l, t, wl["ffn_w2"], x, mods, MOD_GATE2, "ffn_down", tm=1024, tn=512,
                                         rows=r) for r in ((0, lay.p_rows), (lay.p_rows, lay.rows)))
        ks_out.append(kf[:lay.p_rows].reshape(batch, seq, N_KV_A, HEAD_DIM))
        vs_out.append(z[:lay.p_rows, A_WIDTH + KV_WIDTH:A_WIDTH + 2 * KV_WIDTH].reshape(batch, seq, N_KV_A, HEAD_DIM))
        sf_out.append(jnp.swapaxes(s_f[:batch], -1, -2))
        sb_out.append(jnp.swapaxes(s_b[:batch], -1, -2))
    return (y_p.reshape(batch, seq, d), y_s.reshape(dec_batch, dec_seq, d),
            jnp.stack(ks_out, axis=1), jnp.stack(vs_out, axis=1),
            jnp.stack(sf_out, axis=1), jnp.stack(sb_out, axis=1))
```

```python
import functools

import jax
import jax.numpy as jnp
from jax import lax
from jax.experimental import pallas as pl
from jax.experimental.pallas import tpu as pltpu

F32 = jnp.float32
BF16 = jnp.bfloat16

V7X_VMEM_BYTES = 64 * 1024 * 1024
VMEM_LIMIT_BYTES = V7X_VMEM_BYTES - 8 * 1024 * 1024
LANES = 128
SUBLANES = 8
MXU_WIDTH = 256

GRID_W = 64
HEAD_DIM = 128
N_HEADS_A = 8
N_KV_A = 2
GROUP_A = N_HEADS_A // N_KV_A
A_WIDTH = N_HEADS_A * HEAD_DIM
KV_WIDTH = N_KV_A * HEAD_DIM
WINDOW = 128
BLOCK = 128
ROPE_BASE = 10000.0
ROT_AXIS = HEAD_DIM // 2
ATTN_SCALE = HEAD_DIM ** -0.5
NEG_MASK = -1e9
CONF_W = 512
CONF_K = 31
SC_W = 512
SC_K = 3
HG_HEADS = 4
HG_DK = 128
HG_DV = 128
HG_W = HG_HEADS * HG_DK
HG_CHUNK = 32
N_BRANCH = 4
EPS = 1e-6

COL_W = 512
COL_CONF_A, COL_CONF_G, COL_SC_B, COL_SC_C, COL_SC_H = 3, 4, 5, 6, 7
COL_HQ, COL_HF_FWD, COL_HF_BWD, COL_HI, COL_HG = 8, 9, 10, 11, 12
KV_COL_K, KV_COL_V = A_WIDTH // KV_WIDTH, A_WIDTH // KV_WIDTH + 1

ATTN_Q_ROWS = 512
SEQ_TILE = 256
HALO = 16
HG_SAFE_LOG_DECAY = -60.0

MOD_SHIFT1, MOD_SCALE1, MOD_GATE1, MOD_SHIFT2, MOD_SCALE2, MOD_GATE2 = range(6)
MOD_ROWS = 8


def _cparams(*semantics):
    return pltpu.CompilerParams(dimension_semantics=semantics, vmem_limit_bytes=VMEM_LIMIT_BYTES)


def _sigmoid(x):
    return 1.0 / (1.0 + jnp.exp(-x))


def _silu(x):
    return x * _sigmoid(x)


def _chunk_of(idx, size):
    assert size & (size - 1) == 0
    return jnp.right_shift(idx, size.bit_length() - 1)


def _dot(a, b):
    return jnp.dot(a, b, preferred_element_type=F32)


def _dot_nt(a, b):
    return lax.dot_general(a, b, (((1,), (1,)), ((), ())), preferred_element_type=F32)


class _Layout:
    def __init__(self, batch, seq, dec_batch, dec_seq):
        self.batch, self.seq, self.dec_batch, self.dec_seq = batch, seq, dec_batch, dec_seq
        self.p_rows = batch * seq
        self.s_rows = dec_batch * dec_seq
        self.rows = self.p_rows + self.s_rows
        self.n_seq = batch + dec_batch

    def row_tile(self, preferred):
        t = preferred
        while self.p_rows % t or self.dec_seq % t:
            t //= 2
        return t

    def mod_row(self, i, tile):
        p_tiles = self.p_rows // tile
        per = self.dec_seq // tile
        return jnp.where(i < p_tiles, 0, 1 + (i - p_tiles) // per)

    def seq_pos(self, n, tile):
        p_tiles = self.p_rows // tile
        tpp = self.seq // tile
        tps = self.dec_seq // tile
        is_p = n < p_tiles
        seq_idx = jnp.where(is_p, n // tpp, self.batch + (n - p_tiles) // tps)
        pos = jnp.where(is_p, n % tpp, (n - p_tiles) % tps)
        count = jnp.where(is_p, tpp, tps)
        return seq_idx, pos, count


def _mod_kernel(c_ref, w_ref, b_ref, o_ref):
    s = _silu(c_ref[...]).astype(BF16)
    o_ref[...] = _dot(s, w_ref[...].astype(BF16)) + b_ref[...]


def _modulation(cvec, ada_w, ada_b):
    depth, d, n = ada_w.shape
    tn = 1024
    return pl.pallas_call(
        _mod_kernel,
        grid=(depth, n // tn),
        in_specs=[
            pl.BlockSpec((MOD_ROWS, d), lambda l, j: (0, 0)),
            pl.BlockSpec((None, d, tn), lambda l, j: (l, 0, j)),
            pl.BlockSpec((None, 1, tn), lambda l, j: (l, 0, j)),
        ],
        out_specs=pl.BlockSpec((None, MOD_ROWS, tn), lambda l, j: (l, 0, j)),
        out_shape=jax.ShapeDtypeStruct((depth, MOD_ROWS, n), F32),
        compiler_params=_cparams("parallel", "parallel"),
        name="modulation",
    )(cvec, ada_w, ada_b.reshape(depth, 1, n))


NORM_ROWS = 16


def _modulated_norm(x_ref, g_ref, mod_ref, shift_row, scale_row, h_ref):
    gain = g_ref[...] * (1.0 + mod_ref[scale_row:scale_row + 1, :])
    shift = mod_ref[shift_row:shift_row + 1, :]
    block = 8 * NORM_ROWS

    def body(c, carry):
        start = pl.multiple_of(c * block, block)
        for r0 in range(0, block, NORM_ROWS):
            rows = pl.ds(start + r0, NORM_ROWS)
            x = x_ref[rows, :]
            y = x * lax.rsqrt(jnp.mean(x * x, axis=-1, keepdims=True) + EPS)
            h_ref[rows, :] = (y * gain + shift).astype(BF16)
        return carry

    lax.fori_loop(0, x_ref.shape[0] // block, body, 0)


def _split_rows_specs(lay, tm, width, tile_of):
    p_tiles = lay.p_rows // tm
    s_tiles = lay.s_rows // tm
    return [pl.BlockSpec((tm, width), lambda *g: (jnp.minimum(tile_of(*g), p_tiles - 1), 0)),
            pl.BlockSpec((tm, width), lambda *g: (jnp.clip(tile_of(*g) - p_tiles, 0, s_tiles - 1), 0))]


def _inproj_kernel(*refs, p_tiles):
    x_refs, (mod_ref, g_ref, w_ref, z_ref, h_ref) = refs[:-5], refs[-5:]
    i = pl.program_id(0)
    first_step = pl.program_id(1) == 0
    for x_ref, mine in zip(x_refs, (True,) if p_tiles is None else (i < p_tiles, i >= p_tiles)):
        @pl.when(first_step & mine)
        def _():
            _modulated_norm(x_ref, g_ref, mod_ref, MOD_SHIFT1, MOD_SCALE1, h_ref)

    z_ref[...] = _dot(h_ref[...], w_ref[...])


def _layer_of(w, l):
    return w if isinstance(w, tuple) else (w, l)


def _in_projection(lay, l, x, mods, norm_g, w_in):
    w_in, lw = _layer_of(w_in, l)
    pair = isinstance(x, tuple)
    m, d = lay.rows, w_in.shape[1]
    n = w_in.shape[-1]
    tm = lay.row_tile(1024)
    tn = 512
    x_specs = (_split_rows_specs(lay, tm, d, lambda i, j: i) if pair
               else [pl.BlockSpec((tm, d), lambda i, j: (i, 0))])
    return pl.pallas_call(
        functools.partial(_inproj_kernel, p_tiles=lay.p_rows // tm if pair else None),
        grid=(m // tm, n // tn),
        in_specs=x_specs + [
            pl.BlockSpec((None, None, MOD_ROWS, d), lambda i, j: (l, lay.mod_row(i, tm), 0, 0)),
            pl.BlockSpec((None, 1, d), lambda i, j: (l, 0, 0)),
            pl.BlockSpec((None, d, tn), lambda i, j: (lw, 0, j)),
        ],
        out_specs=[
            pl.BlockSpec((tm, tn), lambda i, j: (i, j)),
            pl.BlockSpec((tm, d), lambda i, j: (i, 0)),
        ],
        out_shape=[jax.ShapeDtypeStruct((m, n), F32), jax.ShapeDtypeStruct((m, d), BF16)],
        compiler_params=_cparams("parallel", "arbitrary"),
        name="in_projection",
    )(*(x if pair else (x,)), mods, norm_g.reshape(norm_g.shape[0], 1, d), w_in)


def _residual_matmul_kernel(a_ref, w_ref, x_ref, mod_ref, o_ref, *, gate_row):
    a = a_ref[...]
    chunk = 512
    for c0 in range(0, o_ref.shape[1], chunk):
        sl = slice(c0, c0 + chunk)
        o_ref[:, sl] = x_ref[:, sl] + mod_ref[gate_row:gate_row + 1, sl] * _dot(a, w_ref[:, sl])


def _residual_matmul(lay, l, a, w, x, mods, gate_row, name, tm, tn, rows=None):
    w, lw = _layer_of(w, l)
    m, k = a.shape
    d = w.shape[-1]
    tm = lay.row_tile(tm)
    first, stop = (0, m) if rows is None else rows
    assert first % tm == 0 and stop % tm == 0
    t0 = first // tm
    return pl.pallas_call(
        functools.partial(_residual_matmul_kernel, gate_row=gate_row),
        grid=((stop - first) // tm, d // tn),
        in_specs=[
            pl.BlockSpec((tm, k), lambda i, j: (t0 + i, 0)),
            pl.BlockSpec((None, k, tn), lambda i, j: (lw, 0, j)),
            pl.BlockSpec((tm, tn), lambda i, j: (t0 + i, j)),
            pl.BlockSpec((None, None, MOD_ROWS, tn), lambda i, j: (l, lay.mod_row(t0 + i, tm), 0, j)),
        ],
        out_specs=pl.BlockSpec((tm, tn), lambda i, j: (i, j)),
        out_shape=jax.ShapeDtypeStruct((stop - first, d), F32),
        compiler_params=_cparams("parallel", "arbitrary"),
        name=name,
    )(a, w, x, mods)


def _out_proj_kernel(a_ref, w_ref, *refs, p_tiles):
    x_refs, (mod_ref, g_ref, o_ref, h_ref) = refs[:-4], refs[-4:]
    i = pl.program_id(0)
    a = a_ref[...]
    chunk = 512
    for c0 in range(0, o_ref.shape[1], chunk):
        sl = slice(c0, c0 + chunk)
        update = mod_ref[MOD_GATE1:MOD_GATE1 + 1, sl] * _dot(a, w_ref[:, sl])
        if p_tiles is None:
            o_ref[:, sl] = x_refs[0][:, sl] + update
        else:
            o_ref[:, sl] = jnp.where(i < p_tiles, x_refs[0][:, sl], x_refs[1][:, sl]) + update
    _modulated_norm(o_ref, g_ref, mod_ref, MOD_SHIFT2, MOD_SCALE2, h_ref)


def _out_projection(lay, l, a, w_o, x, mods, norm2_g):
    pair = isinstance(x, tuple)
    m, k = a.shape
    d = w_o.shape[-1]
    tm = lay.row_tile(512)
    x_specs = _split_rows_specs(lay, tm, d, lambda i: i) if pair else [pl.BlockSpec((tm, d), lambda i: (i, 0))]
    return pl.pallas_call(
        functools.partial(_out_proj_kernel, p_tiles=lay.p_rows // tm if pair else None),
        grid=(m // tm,),
        in_specs=[
            pl.BlockSpec((tm, k), lambda i: (i, 0)),
            pl.BlockSpec((None, k, d), lambda i: (l, 0, 0)),
        ] + x_specs + [
            pl.BlockSpec((None, None, MOD_ROWS, d), lambda i: (l, lay.mod_row(i, tm), 0, 0)),
            pl.BlockSpec((None, 1, d), lambda i: (l, 0, 0)),
        ],
        out_specs=[pl.BlockSpec((tm, d), lambda i: (i, 0)), pl.BlockSpec((tm, d), lambda i: (i, 0))],
        out_shape=[jax.ShapeDtypeStruct((m, d), F32), jax.ShapeDtypeStruct((m, d), BF16)],
        compiler_params=_cparams("parallel"),
        name="out_projection",
    )(a, w_o, *(x if pair else (x,)), mods, norm2_g.reshape(norm2_g.shape[0], 1, d))


def _ffn_up_kernel(h_ref, w1_ref, w3_ref, o_ref):
    h = h_ref[...]
    tn = o_ref.shape[1]
    tail = tn % MXU_WIDTH
    c0 = 0
    while c0 < tn - tail:
        c1 = min(c0 + 512, tn - tail)
        o_ref[:, c0:c1] = (_silu(_dot(h, w1_ref[:, c0:c1])) * _dot(h, w3_ref[:, c0:c1])).astype(BF16)
        c0 = c1
    if tail:
        both = _dot(h, jnp.concatenate([w1_ref[:, c0:], w3_ref[:, c0:]], axis=1))
        o_ref[:, c0:] = (_silu(both[:, :tail]) * both[:, tail:]).astype(BF16)


def _ffn_up(lay, l, h, w1, w3):
    (w1, l1), (w3, l3) = _layer_of(w1, l), _layer_of(w3, l)
    m, d = h.shape
    n = w1.shape[-1]
    tm = lay.row_tile(1024)
    tn = n // 4
    assert n % 4 == 0 and tn % LANES == 0
    return pl.pallas_call(
        _ffn_up_kernel,
        grid=(m // tm, n // tn),
        in_specs=[
            pl.BlockSpec((tm, d), lambda i, j: (i, 0)),
            pl.BlockSpec((None, d, tn), lambda i, j: (l1, 0, j)),
            pl.BlockSpec((None, d, tn), lambda i, j: (l3, 0, j)),
        ],
        out_specs=pl.BlockSpec((tm, tn), lambda i, j: (i, j)),
        out_shape=jax.ShapeDtypeStruct((m, n), BF16),
        compiler_params=_cparams("parallel", "parallel"),
        name="ffn_up",
    )(h, w1, w3)


def _qk_prep_kernel(q_ref, k_ref, cos_ref, sa_ref, sb_ref, qg_ref, kg_ref, qo_ref, ko_ref, kf_ref):
    cos, sin_a, sin_b = cos_ref[...], sa_ref[...], sb_ref[...]

    ones = jnp.ones((HEAD_DIM, HEAD_DIM), BF16)

    def norm(x, g):
        mean_sq = _dot((x * x).astype(BF16), ones) * (1.0 / HEAD_DIM)
        return x * lax.rsqrt(mean_sq + EPS) * g

    def rope(x):
        return (x * cos + pltpu.roll(x, HEAD_DIM - ROT_AXIS // 2, 1) * sin_a
                + pltpu.roll(x, ROT_AXIS // 2, 1) * sin_b)

    for h in range(N_HEADS_A):
        sl = slice(h * HEAD_DIM, (h + 1) * HEAD_DIM)
        qo_ref[:, sl] = (rope(norm(q_ref[:, sl], qg_ref[...])) * ATTN_SCALE).astype(BF16)
    for h in range(N_KV_A):
        sl = slice(h * HEAD_DIM, (h + 1) * HEAD_DIM)
        kn = norm(k_ref[:, sl], kg_ref[...])
        kf_ref[:, sl] = kn
        ko_ref[:, sl] = rope(kn).astype(BF16)


def _qk_prep(lay, l, z, rope_tabs, q_norm_g, k_norm_g):
    m = z.shape[0]
    tm = lay.row_tile(512)
    cos, sin_a, sin_b = rope_tabs
    tab_spec = pl.BlockSpec((tm, HEAD_DIM), lambda i: (i, 0))
    g_spec = pl.BlockSpec((None, 1, HEAD_DIM), lambda i: (l, 0, 0))
    return pl.pallas_call(
        _qk_prep_kernel,
        grid=(m // tm,),
        in_specs=[
            pl.BlockSpec((tm, A_WIDTH), lambda i: (i, 0)),
            pl.BlockSpec((tm, KV_WIDTH), lambda i: (i, KV_COL_K)),
            tab_spec, tab_spec, tab_spec, g_spec, g_spec,
        ],
        out_specs=[
            pl.BlockSpec((tm, A_WIDTH), lambda i: (i, 0)),
            pl.BlockSpec((tm, KV_WIDTH), lambda i: (i, 0)),
            pl.BlockSpec((tm, KV_WIDTH), lambda i: (i, 0)),
        ],
        out_shape=[
            jax.ShapeDtypeStruct((m, A_WIDTH), BF16),
            jax.ShapeDtypeStruct((m, KV_WIDTH), BF16),
            jax.ShapeDtypeStruct((m, KV_WIDTH), F32),
        ],
        compiler_params=_cparams("parallel"),
        name="qk_prep",
    )(z, z, cos, sin_a, sin_b, q_norm_g.reshape(-1, 1, HEAD_DIM), k_norm_g.reshape(-1, 1, HEAD_DIM))


def _stack_heads(q_ref, kv):
    return jnp.concatenate(
        [q_ref[:, (kv * GROUP_A + g) * HEAD_DIM:(kv * GROUP_A + g + 1) * HEAD_DIM] for g in range(GROUP_A)], axis=0)


def _fold_lane_tiles(x, op):
    out = x[:, :LANES]
    for c0 in range(LANES, x.shape[1], LANES):
        out = op(out, x[:, c0:c0 + LANES])
    return out


def _sink_column(sink_ref, kv, rows):
    return jnp.concatenate(
        [jnp.full((rows, 1), sink_ref[kv * GROUP_A + g], F32) for g in range(GROUP_A)], axis=0)


def _context_attention_kernel(sink_ref, q_ref, k_ref, v_ref, o_ref, *, seq):
    for r0 in range(0, q_ref.shape[0], seq):
        rows = slice(r0, r0 + seq)
        for kv in range(N_KV_A):
            sl = slice(kv * HEAD_DIM, (kv + 1) * HEAD_DIM)
            q = jnp.concatenate(
                [q_ref[rows, (kv * GROUP_A + g) * HEAD_DIM:(kv * GROUP_A + g + 1) * HEAD_DIM]
                 for g in range(GROUP_A)], axis=0)
            s = _dot_nt(q, k_ref[rows, sl])
            sink = _sink_column(sink_ref, kv, seq)
            mx = jnp.maximum(jnp.max(_fold_lane_tiles(s, jnp.maximum), axis=-1, keepdims=True), sink)
            p = jnp.exp(s - mx)
            den = jnp.sum(_fold_lane_tiles(p, jnp.add), axis=-1, keepdims=True) + jnp.exp(sink - mx)
            o = _dot(p.astype(BF16), v_ref[rows, sl].astype(BF16)) * (1.0 / den)
            for g in range(GROUP_A):
                h = kv * GROUP_A + g
                o_ref[rows, h * HEAD_DIM:(h + 1) * HEAD_DIM] = o[g * seq:(g + 1) * seq].astype(BF16)


def _context_attention(lay, l, qn, kn, z, sink):
    per_step = next(n for n in (4, 2, 1) if lay.batch % n == 0)
    seq = lay.seq
    rows = per_step * seq
    return pl.pallas_call(
        functools.partial(_context_attention_kernel, seq=seq),
        grid=(lay.batch // per_step,),
        in_specs=[
            pl.BlockSpec(memory_space=pltpu.SMEM),
            pl.BlockSpec((rows, A_WIDTH), lambda b: (b, 0)),
            pl.BlockSpec((rows, KV_WIDTH), lambda b: (b, 0)),
            pl.BlockSpec((rows, KV_WIDTH), lambda b: (b, KV_COL_V)),
        ],
        out_specs=pl.BlockSpec((rows, A_WIDTH), lambda b: (b, 0)),
        out_shape=jax.ShapeDtypeStruct((lay.p_rows, A_WIDTH), BF16),
        compiler_params=_cparams("parallel"),
        name="context_attention",
    )(sink, qn, kn, z)


def _latent_attention_kernel(sink_ref, q_ref, kp_ref, kc_ref, kn_ref, vp_ref, vc_ref, vn_ref, ck_ref, cv_ref,
                             o_ref, *, seq_len):
    assert WINDOW <= BLOCK
    q_rows = q_ref.shape[0]
    j = pl.program_id(1)
    n_tiles = seq_len // q_rows
    r = lax.broadcasted_iota(jnp.int32, (q_rows, BLOCK), 0)
    c = lax.broadcasted_iota(jnp.int32, (q_rows, BLOCK), 1)
    valid_prev = jnp.concatenate([(r - (c - BLOCK) <= WINDOW) & (j >= 1)] * GROUP_A, axis=0)
    valid_next = jnp.concatenate([((c + q_rows) - r <= WINDOW) & (j <= n_tiles - 2)] * GROUP_A, axis=0)
    valid_own = None
    if q_rows - 1 > WINDOW:
        ro = lax.broadcasted_iota(jnp.int32, (q_rows, q_rows), 0)
        co = lax.broadcasted_iota(jnp.int32, (q_rows, q_rows), 1)
        valid_own = jnp.concatenate([jnp.abs(co - ro) <= WINDOW] * GROUP_A, axis=0)
    for kv in range(N_KV_A):
        sl = slice(kv * HEAD_DIM, (kv + 1) * HEAD_DIM)
        q = _stack_heads(q_ref, kv)
        s_p = jnp.where(valid_prev, _dot_nt(q, kp_ref[:, sl]), NEG_MASK)
        s_o = _dot_nt(q, kc_ref[:, sl])
        if valid_own is not None:
            s_o = jnp.where(valid_own, s_o, NEG_MASK)
        s_n = jnp.where(valid_next, _dot_nt(q, kn_ref[:, sl]), NEG_MASK)
        s_c = _dot_nt(q, ck_ref[:, sl].astype(BF16))
        sink = _sink_column(sink_ref, kv, q_rows)
        mx = jnp.maximum(jnp.maximum(s_p, _fold_lane_tiles(s_o, jnp.maximum)),
                         jnp.maximum(s_n, _fold_lane_tiles(s_c, jnp.maximum)))
        mx = jnp.maximum(jnp.max(mx, axis=-1, keepdims=True), sink)
        psum = None
        o = None
        for s, v_ref in ((s_p, vp_ref), (s_o, vc_ref), (s_n, vn_ref), (s_c, cv_ref)):
            p = jnp.exp(s - mx)
            pf = _fold_lane_tiles(p, jnp.add)
            psum = pf if psum is None else psum + pf
            term = _dot(p.astype(BF16), v_ref[:, sl].astype(BF16))
            o = term if o is None else o + term
        den = jnp.sum(psum, axis=-1, keepdims=True) + jnp.exp(sink - mx)
        o = o * (1.0 / den)
        for g in range(GROUP_A):
            h = kv * GROUP_A + g
            o_ref[:, h * HEAD_DIM:(h + 1) * HEAD_DIM] = o[g * q_rows:(g + 1) * q_rows].astype(BF16)


def _latent_attention(lay, l, qn, kn, z, ctx_k, ctx_v, sink):
    q_rows = ATTN_Q_ROWS
    per_tile = q_rows // BLOCK
    assert lay.p_rows % q_rows == 0 and lay.dec_seq % q_rows == 0
    nb = lay.dec_seq // BLOCK
    nt = lay.dec_seq // q_rows
    base = lay.p_rows // BLOCK
    past = ctx_k.shape[2]

    def own(col):
        return lambda b, j: (lay.p_rows // q_rows + b * nt + j, col)

    def prev(col):
        return lambda b, j: (base + b * nb + jnp.maximum(j * per_tile - 1, 0), col)

    def nxt(col):
        return lambda b, j: (base + b * nb + jnp.minimum((j + 1) * per_tile, nb - 1), col)

    def trio(col):
        return [pl.BlockSpec((BLOCK, KV_WIDTH), prev(col)), pl.BlockSpec((q_rows, KV_WIDTH), own(col)),
                pl.BlockSpec((BLOCK, KV_WIDTH), nxt(col))]

    k_spec = trio(0)
    v_spec = trio(KV_COL_V)
    ctx_spec = pl.BlockSpec((None, None, past, KV_WIDTH), lambda b, j: (b, l, 0, 0))
    return pl.pallas_call(
        functools.partial(_latent_attention_kernel, seq_len=lay.dec_seq),
        grid=(lay.dec_batch, nt),
        in_specs=[pl.BlockSpec(memory_space=pltpu.SMEM), pl.BlockSpec((q_rows, A_WIDTH), own(0))]
        + k_spec + v_spec + [ctx_spec, ctx_spec],
        out_specs=pl.BlockSpec((q_rows, A_WIDTH), lambda b, j: (b * nt + j, 0)),
        out_shape=jax.ShapeDtypeStruct((lay.s_rows, A_WIDTH), BF16),
        compiler_params=_cparams("parallel", "parallel"),
        name="latent_attention",
    )(sink, qn, kn, kn, kn, z, z, z, ctx_k, ctx_v)


def _conv_kernel(a_c, a_p, a_n, g_c, g_p, g_n, b_c, c_c, c_p, c_n, h_c, h_p, h_n,
                 dw_ref, db_ref, lg_ref, lb_ref, sw_ref, ub_ref, uc_ref, xb_s, xc_s, *, lay):
    _, pos, count = lay.seq_pos(pl.program_id(0), SEQ_TILE)
    has_prev = pos > 0
    has_next = pos < count - 1
    rb = 32
    xb_s[0, 0:HALO, :] = jnp.where(has_prev, a_p[...] * _sigmoid(g_p[...]), 0.0)
    xb_s[0, HALO + SEQ_TILE:, :] = jnp.where(has_next, a_n[...] * _sigmoid(g_n[...]), 0.0)
    xc_s[0:HALO, :] = jnp.where(has_prev, c_p[...] * h_p[...], 0.0)
    xc_s[HALO + SEQ_TILE:, :] = jnp.where(has_next, c_n[...] * h_n[...], 0.0)
    for r0 in range(0, SEQ_TILE, rb):
        rows = slice(r0, r0 + rb)
        xb_s[0, HALO + r0:HALO + r0 + rb, :] = a_c[rows, :] * _sigmoid(g_c[rows, :])
        xc_s[HALO + r0:HALO + r0 + rb, :] = c_c[rows, :] * h_c[rows, :]
    shifted_rows = SEQ_TILE + 2 * HALO - SUBLANES
    chunk = 40
    for s in range(1, SUBLANES):
        for r0 in range(0, shifted_rows, chunk):
            xb_s[s, r0:r0 + chunk, :] = xb_s[0, r0 + s:r0 + s + chunk, :]
    for r0 in range(0, SEQ_TILE, rb):
        acc = jnp.zeros((rb // SUBLANES, SUBLANES, CONF_W), F32)
        for k in range(CONF_K):
            off = HALO - CONF_K // 2 + k
            row = r0 + off - off % SUBLANES
            x = xb_s[off % SUBLANES, row:row + rb, :]
            acc = acc + dw_ref[k] * x.reshape(rb // SUBLANES, SUBLANES, CONF_W)
        u = acc.reshape(rb, CONF_W) + db_ref[...]
        uc = u - jnp.mean(u, axis=-1, keepdims=True)
        var = jnp.mean(uc * uc, axis=-1, keepdims=True)
        u = uc * lax.rsqrt(var + EPS) * lg_ref[...] + lb_ref[...]
        ub_ref[r0:r0 + rb, :] = _silu(u).astype(BF16)
        acc = jnp.zeros((rb // SUBLANES, SUBLANES, SC_W), F32)
        for k in range(SC_K):
            off = HALO - SC_K // 2 + r0 + k
            acc = acc + sw_ref[k] * xc_s[off:off + rb, :].reshape(rb // SUBLANES, SUBLANES, SC_W)
        uc_ref[r0:r0 + rb, :] = (b_c[r0:r0 + rb, :] * acc.reshape(rb, SC_W)).astype(BF16)


def _conv_mixers(lay, l, z, conf_dw_w, conf_dw_b, conf_ln_g, conf_ln_b, sc_conv_w):
    m = z.shape[0]
    n_tiles = m // SEQ_TILE
    per = SEQ_TILE // HALO
    n_halo = m // HALO

    def cur(col):
        return pl.BlockSpec((SEQ_TILE, COL_W), lambda i: (i, col))

    def prev(col):
        return pl.BlockSpec((HALO, COL_W), lambda i: (jnp.maximum(i * per - 1, 0), col))

    def nxt(col):
        return pl.BlockSpec((HALO, COL_W), lambda i: (jnp.minimum((i + 1) * per, n_halo - 1), col))

    def trio(col):
        return [cur(col), prev(col), nxt(col)]

    def par(rows):
        return pl.BlockSpec((None, rows, COL_W), lambda i: (l, 0, 0))

    def taps(n):
        return pl.BlockSpec((None, n, SUBLANES, COL_W), lambda i: (l, 0, 0, 0))

    depth = conf_dw_w.shape[0]
    dw = jnp.broadcast_to(conf_dw_w[:, :, None, :], (depth, CONF_K, SUBLANES, CONF_W))
    sw = jnp.broadcast_to(sc_conv_w[:, :, None, :], (depth, SC_K, SUBLANES, SC_W))
    return pl.pallas_call(
        functools.partial(_conv_kernel, lay=lay),
        grid=(n_tiles,),
        in_specs=trio(COL_CONF_A) + trio(COL_CONF_G) + [cur(COL_SC_B)] + trio(COL_SC_C) + trio(COL_SC_H)
        + [taps(CONF_K), par(1), par(1), par(1), taps(SC_K)],
        out_specs=[pl.BlockSpec((SEQ_TILE, CONF_W), lambda i: (i, 0)),
                   pl.BlockSpec((SEQ_TILE, SC_W), lambda i: (i, 0))],
        out_shape=[jax.ShapeDtypeStruct((m, CONF_W), BF16), jax.ShapeDtypeStruct((m, SC_W), BF16)],
        scratch_shapes=[pltpu.VMEM((SUBLANES, SEQ_TILE + 2 * HALO, CONF_W), F32),
                        pltpu.VMEM((SEQ_TILE + 2 * HALO, SC_W), F32)],
        compiler_params=_cparams("parallel"),
        name="conv_mixers",
    )(*([z] * 13), dw, conf_dw_b.reshape(depth, 1, CONF_W), conf_ln_g.reshape(depth, 1, CONF_W),
      conf_ln_b.reshape(depth, 1, CONF_W), sw)


def _hgrn_gates(hq_ref, hf_ref, lb_ref, sl, tri_b3):
    zq = hq_ref[:, sl]
    q = _silu(zq) * (HG_DK ** -0.5)
    zf = hf_ref[:, sl]
    lb = lb_ref[:, sl]
    e = jnp.exp(-jnp.abs(zf))
    r = 1.0 / (1.0 + e)
    nonneg = zf >= 0
    sig_pos = jnp.where(nonneg, r, e * r)
    sig_neg = jnp.where(nonneg, e * r, r)
    logf = jnp.log(lb + (1.0 - lb) * sig_pos)
    kk = (1.0 - lb) * sig_neg
    g1 = logf.astype(BF16)
    r1 = logf - g1.astype(F32)
    g2 = r1.astype(BF16)
    g3 = (r1 - g2.astype(F32)).astype(BF16)
    b = _dot(tri_b3, jnp.concatenate([g1, g2, g3], axis=0))
    return q, kk, b


def _block_place(x, n_chunks):
    w = x.shape[1]
    zeros = jnp.zeros((HG_CHUNK, w), x.dtype)
    row_blocks = []
    for c in range(n_chunks):
        cols = [zeros] * n_chunks
        cols[c] = x[c * HG_CHUNK:(c + 1) * HG_CHUNK, :]
        row_blocks.append(jnp.concatenate(cols, axis=1))
    return jnp.concatenate(row_blocks, axis=0)


def _hgrn_exact_scores(a_ref, q_s, k_s, b_s, reverse):
    ts = a_ref.shape[0]
    a_ref[...] = jnp.zeros_like(a_ref)

    def body(s, carry):
        c0 = pl.multiple_of((s // HG_CHUNK) * HG_CHUNK, HG_CHUNK)
        w = jnp.exp(jnp.minimum(b_s[pl.ds(c0, HG_CHUNK), :] - b_s[pl.ds(s, 1), :], 0.0))
        prod = q_s[pl.ds(c0, HG_CHUNK), :] * k_s[pl.ds(s, 1), :] * w
        trow = c0 + lax.broadcasted_iota(jnp.int32, (HG_CHUNK, 1), 0)
        ok = (trow <= s) if reverse else (trow >= s)
        onehot = (lax.broadcasted_iota(jnp.int32, (1, ts), 1) == s).astype(F32)
        colv = jnp.sum(prod, axis=-1, keepdims=True)
        a_ref[pl.ds(c0, HG_CHUNK), :] += jnp.where(ok, colv, 0.0) * onehot
        return carry

    lax.fori_loop(0, ts, body, 0)


def _hgrn_kernel(hq_ref, hf_ref, hi_ref, lb_ref, s0_ref, *refs, lay, reverse, finish, cast_steps):
    ts = SEQ_TILE
    n_chunks = ts // HG_CHUNK
    _, pos, count = lay.seq_pos(pl.program_id(0), ts)
    other_ref, gate_ref, gain_ref = refs[:3] if finish else (None, None, None)
    refs = refs[3:] if finish else refs
    n_cast = len(cast_steps)
    src_refs, (o_ref, sf_ref), dst_refs = refs[:n_cast], refs[n_cast:n_cast + 2], refs[n_cast + 2:2 * n_cast + 2]
    st_ref, oi_s, a_ref, q_s, k_s, b_s = refs[2 * n_cast + 2:]
    _side_cast(src_refs, dst_refs, cast_steps, pl.program_id(0))

    def emit(o, sl):
        if finish:
            o = o + other_ref[:, sl]
            o = o * lax.rsqrt(jnp.mean(o * o, axis=-1, keepdims=True) + EPS) * gain_ref[...]
            o_ref[:, sl] = (o * _silu(gate_ref[:, sl])).astype(BF16)
        else:
            o_ref[:, sl] = o

    @pl.when(pos == 0)
    def _():
        st_ref[...] = s0_ref[...]

    row = lax.broadcasted_iota(jnp.int32, (ts, ts), 0)
    col = lax.broadcasted_iota(jnp.int32, (ts, ts), 1)
    same = _chunk_of(row, HG_CHUNK) == _chunk_of(col, HG_CHUNK)
    tri = same & ((col >= row) if reverse else (col <= row))
    tri_b = tri.astype(F32).astype(BF16)
    tri_b3 = jnp.concatenate([tri_b] * 3, axis=1)
    order = range(n_chunks - 1, -1, -1) if reverse else range(n_chunks)
    last = 0 if reverse else HG_CHUNK - 1

    b_min = None
    for hd in range(HG_HEADS):
        sl = slice(hd * HG_DK, (hd + 1) * HG_DK)
        q, kk, b = _hgrn_gates(hq_ref, hf_ref, lb_ref, sl, tri_b3)
        head_min = jnp.min(b)
        b_min = head_min if b_min is None else jnp.minimum(b_min, head_min)
        b3 = b.reshape(n_chunks, HG_CHUNK, HG_DK)
        total3 = b3[:, last:last + 1, :]
        qt = (q * jnp.exp(b)).astype(BF16)
        kend = (kk * jnp.exp(jnp.broadcast_to(total3, b3.shape).reshape(ts, HG_DK) - b)).astype(BF16)
        decay = jnp.exp(total3.reshape(n_chunks, HG_DK))
        v_h = hi_ref[:, sl]
        kinv = (kk * jnp.exp(-b)).astype(BF16)
        scores = jnp.where(tri, _dot_nt(qt, kinv), 0.0).astype(BF16)
        o_intra = _dot(scores, v_h.astype(BF16))
        incr = _dot(v_h.T.astype(BF16), _block_place(kend, n_chunks))
        state = st_ref[hd]
        starts = [None] * n_chunks
        for c in order:
            starts[c] = state
            state = state * decay[c:c + 1, :] + incr[:, c * HG_DK:(c + 1) * HG_DK]
        st_ref[hd] = state
        o_inter = jnp.concatenate(
            [_dot_nt(qt[c * HG_CHUNK:(c + 1) * HG_CHUNK, :], starts[c].astype(BF16)) for c in range(n_chunks)],
            axis=0)
        oi_s[:, sl] = o_inter
        emit(o_intra + o_inter, sl)

    @pl.when(b_min <= HG_SAFE_LOG_DECAY)
    def _():
        for hd in range(HG_HEADS):
            sl = slice(hd * HG_DK, (hd + 1) * HG_DK)
            q, kk, b = _hgrn_gates(hq_ref, hf_ref, lb_ref, sl, tri_b3)
            q_s[...] = q
            k_s[...] = kk
            b_s[...] = b
            _hgrn_exact_scores(a_ref, q_s, k_s, b_s, reverse)
            emit(_dot(a_ref[...].astype(BF16), hi_ref[:, sl].astype(BF16)) + oi_s[:, sl], sl)

    @pl.when(pos == count - 1)
    def _():
        sf_ref[...] = st_ref[...]


def _hgrn_direction(lay, l, z, lb, s0_t, reverse, other=None, norm_g=None, cast=()):
    finish = other is not None
    m = z.shape[0]
    n_tiles = m // SEQ_TILE
    cast_specs, cast_out_specs, cast_shapes, cast_steps = _side_cast_specs(cast, l, n_tiles, lambda n: n)

    def tile_of(n):
        seq_idx, pos, count = lay.seq_pos(n, SEQ_TILE)
        return (n - pos + (count - 1 - pos)) if reverse else n

    def zcol(col):
        return pl.BlockSpec((SEQ_TILE, COL_W), lambda n: (tile_of(n), col))

    state_spec = pl.BlockSpec((None, HG_HEADS, HG_DV, HG_DK), lambda n: (lay.seq_pos(n, SEQ_TILE)[0], 0, 0, 0))
    o_spec = pl.BlockSpec((SEQ_TILE, HG_W), lambda n: (tile_of(n), 0))
    finish_specs, finish_args = [], []
    if finish:
        finish_specs = [o_spec, zcol(COL_HG), pl.BlockSpec((None, 1, HG_DV), lambda n: (l, 0, 0))]
        finish_args = [other, z, norm_g.reshape(-1, 1, HG_DV)]
    out = pl.pallas_call(
        functools.partial(_hgrn_kernel, lay=lay, reverse=reverse, finish=finish, cast_steps=cast_steps),
        grid=(n_tiles,),
        in_specs=[zcol(COL_HQ), zcol(COL_HF_BWD if reverse else COL_HF_FWD), zcol(COL_HI),
                  pl.BlockSpec((1, HG_W), lambda n: (0, 0)), state_spec] + finish_specs + cast_specs,
        out_specs=[o_spec, state_spec] + cast_out_specs,
        out_shape=[jax.ShapeDtypeStruct((m, HG_W), BF16 if finish else F32),
                   jax.ShapeDtypeStruct((lay.n_seq, HG_HEADS, HG_DV, HG_DK), F32)] + cast_shapes,
        scratch_shapes=[pltpu.VMEM((HG_HEADS, HG_DV, HG_DK), F32),
                        pltpu.VMEM((SEQ_TILE, HG_W), F32),
                        pltpu.VMEM((SEQ_TILE, SEQ_TILE), F32),
                        pltpu.VMEM((SEQ_TILE, HG_DK), F32),
                        pltpu.VMEM((SEQ_TILE, HG_DK), F32),
                        pltpu.VMEM((SEQ_TILE, HG_DK), F32)],
        compiler_params=_cparams("arbitrary"),
        name="hgrn_bwd" if reverse else "hgrn_fwd",
    )(z, z, z, lb, s0_t, *finish_args, *cast)
    return out[0], out[1], out[2:]


def _merge_kernel(h_ref, ap_ref, as_ref, b_ref, c_ref, d_ref, wg0, wg1, wg2, wg3, bg0, bg1, bg2, bg3,
                  wa_ref, wb_ref, wc_ref, wd_ref, *refs, p_tiles, cast_steps):
    n_cast = len(cast_steps)
    src_refs, o_ref, dst_refs = refs[:n_cast], refs[n_cast], refs[n_cast + 1:]
    _side_cast(src_refs, dst_refs, cast_steps, pl.program_id(0) * pl.num_programs(1) + pl.program_id(1))

    h = h_ref[...]
    attn = jnp.where(pl.program_id(1) < p_tiles, ap_ref[...], as_ref[...])
    acc = None
    for branch, w_ref, wg_ref, bg_ref in ((attn, wa_ref, wg0, bg0), (b_ref[...], wb_ref, wg1, bg1),
                                          (c_ref[...], wc_ref, wg2, bg2), (d_ref[...], wd_ref, wg3, bg3)):
        gate = _sigmoid(_dot(h, wg_ref[...]) + bg_ref[...])
        term = gate * _dot(branch, w_ref[...])
        acc = term if acc is None else acc + term
    o_ref[...] = acc.astype(BF16)


def _side_cast_specs(weights, layer, n_steps, step_of):
    in_specs, out_specs, shapes, steps_list = [], [], [], []
    for w in weights:
        _, rows, cols = w.shape
        slab, steps = _cast_plan(rows, n_steps)

        def slab_of(*g, steps=steps):
            return jnp.minimum(step_of(*g), steps - 1)

        in_specs.append(pl.BlockSpec((None, slab, cols), lambda *g, f=slab_of: (layer, f(*g), 0)))
        out_specs.append(pl.BlockSpec((None, slab, cols), lambda *g, f=slab_of: (0, f(*g), 0)))
        shapes.append(jax.ShapeDtypeStruct((1, rows, cols), BF16))
        steps_list.append(steps)
    return in_specs, out_specs, shapes, tuple(steps_list)


def _side_cast(src_refs, dst_refs, cast_steps, step):
    for src_ref, dst_ref, n_steps in zip(src_refs, dst_refs, cast_steps):
        @pl.when(step < n_steps)
        def _():
            dst_ref[...] = src_ref[...].astype(BF16)


def _cast_kernel(src_ref, dst_ref):
    dst_ref[...] = src_ref[...].astype(BF16)


def _cast_layer(w, l):
    _, rows, cols = w.shape
    slab, steps = _cast_plan(rows, max(1, rows * cols * 4 // CAST_SLAB_BYTES))
    return pl.pallas_call(
        _cast_kernel,
        grid=(steps,),
        in_specs=[pl.BlockSpec((None, slab, cols), lambda s: (l, s, 0))],
        out_specs=pl.BlockSpec((None, slab, cols), lambda s: (0, s, 0)),
        out_shape=jax.ShapeDtypeStruct((1, rows, cols), BF16),
        compiler_params=_cparams("parallel"),
        name="cast_weights",
    )(w)


CAST_SLAB_BYTES = 2 << 20


def _cast_plan(rows, n_steps):
    for steps in range(n_steps, 0, -1):
        slab = rows // steps
        if rows % steps == 0 and slab % (2 * SUBLANES) == 0:
            return slab, steps
    raise ValueError("no slab size for %d weight rows in %d steps" % (rows, n_steps))


def _merge(lay, l, h, attn_p, attn_s, branches, w_gate, b_gate, w_outs, cast_next=()):
    w_gate, lg = _layer_of(w_gate, l)
    m, d = h.shape
    tm = lay.row_tile(512)
    tn = 512
    nj = d // tn
    n_i = m // tm
    depth = b_gate.shape[0]
    p_tiles = lay.p_rows // tm
    s_tiles = lay.s_rows // tm

    def gate_w(n):
        return pl.BlockSpec((None, d, tn), lambda j, i: (lg, 0, n * nj + j))

    def gate_b(n):
        return pl.BlockSpec((None, 1, tn), lambda j, i: (l, 0, n * nj + j))

    cast_specs, cast_out_specs, cast_shapes, cast_steps = _side_cast_specs(
        cast_next, l + 1, nj * n_i, lambda j, i: j * n_i + i)

    b_gate3 = b_gate.reshape(depth, 1, N_BRANCH * d)
    out = pl.pallas_call(
        functools.partial(_merge_kernel, p_tiles=p_tiles, cast_steps=tuple(cast_steps)),
        grid=(nj, n_i),
        in_specs=[pl.BlockSpec((tm, d), lambda j, i: (i, 0)),
                  pl.BlockSpec((tm, A_WIDTH), lambda j, i: (jnp.minimum(i, p_tiles - 1), 0)),
                  pl.BlockSpec((tm, A_WIDTH), lambda j, i: (jnp.clip(i - p_tiles, 0, s_tiles - 1), 0))]
        + [pl.BlockSpec((tm, br.shape[1]), lambda j, i: (i, 0)) for br in branches]
        + [gate_w(n) for n in range(N_BRANCH)] + [gate_b(n) for n in range(N_BRANCH)]
        + [pl.BlockSpec((None, w.shape[1], tn), lambda j, i: (l, 0, j)) for w in w_outs]
        + cast_specs,
        out_specs=[pl.BlockSpec((tm, tn), lambda j, i: (i, j))] + cast_out_specs,
        out_shape=[jax.ShapeDtypeStruct((m, d), BF16)] + cast_shapes,
        compiler_params=_cparams("arbitrary", "arbitrary"),
        name="merge",
    )(h, attn_p, attn_s, *branches, *([w_gate] * N_BRANCH), *([b_gate3] * N_BRANCH), *w_outs, *cast_next)
    return out[0], out[1:]


def _rope_tables(lay):
    t = lay.dec_seq
    half = ROT_AXIS // 2
    rows = t // GRID_W
    row = jnp.repeat(jnp.arange(rows, dtype=F32), GRID_W)
    col = jnp.tile(jnp.arange(GRID_W, dtype=F32), rows)
    inv = ROPE_BASE ** (-jnp.arange(0, ROT_AXIS, 2, dtype=F32) / ROT_AXIS)
    ang = jnp.concatenate([row[:, None] * inv, row[:, None] * inv, col[:, None] * inv, col[:, None] * inv], axis=1)
    lane = jnp.arange(HEAD_DIM)
    first_half = (lane % ROT_AXIS) < half
    cos = jnp.cos(ang)
    sin = jnp.sin(ang)
    sin_a = jnp.where(first_half, -sin, 0.0)
    sin_b = jnp.where(first_half, 0.0, sin)

    def full(tab, fill):
        return jnp.concatenate([jnp.full((lay.p_rows, HEAD_DIM), fill, F32)] + [tab] * lay.dec_batch, axis=0)

    return full(cos, 1.0), full(sin_a, 0.0), full(sin_b, 0.0)


def kernel(x_prompt, x_sample, cache_k, cache_v, state_hgrn_fwd, state_hgrn_bwd, c, c_ctx, ada_w, ada_b, norm1_g, norm2_g, w_in, q_norm_g, k_norm_g, attn_sink, w_attn_out, conf_dw_w, conf_dw_b, conf_ln_g, conf_ln_b, w_conf_out, sc_conv_w, w_sc_out, hg_lb, hg_norm_g, w_hg_out, w_gate, b_gate, w_o, ffn_w1, ffn_w3, ffn_w2):
    batch, seq, d = x_prompt.shape
    dec_batch, dec_seq, _ = x_sample.shape
    depth = ada_w.shape[0]
    past = cache_k.shape[2]
    lay = _Layout(batch, seq, dec_batch, dec_seq)
    assert seq % SEQ_TILE == 0 and dec_seq % SEQ_TILE == 0 and dec_seq % GRID_W == 0
    assert 1 + dec_batch <= MOD_ROWS

    wb = {name: w.astype(BF16) for name, w in dict(
        w_attn_out=w_attn_out, w_conf_out=w_conf_out, w_sc_out=w_sc_out, w_hg_out=w_hg_out, w_o=w_o).items()}
    big = dict(w_gate=w_gate, w_in=w_in, ffn_w1=ffn_w1, ffn_w3=ffn_w3, ffn_w2=ffn_w2)
    wl = {"w_in": (_cast_layer(w_in, 0), 0)}

    cvec = jnp.concatenate([c_ctx[None], c, jnp.zeros((MOD_ROWS - 1 - dec_batch, d), F32)], axis=0)
    mods = _modulation(cvec, ada_w, ada_b)
    mods = mods.reshape(depth, MOD_ROWS, 6, d)[:, :1 + dec_batch]
    mods = jnp.pad(mods, ((0, 0), (0, 0), (0, MOD_ROWS - 6), (0, 0)))

    lb = jax.nn.softmax(hg_lb.astype(F32), axis=1)
    lb = jnp.maximum(jnp.cumsum(lb, axis=1) - lb[:, :1], 0.0)

    rope_tabs = _rope_tables(lay)
    ctx_k = cache_k.reshape(dec_batch, depth, past, KV_WIDTH)
    ctx_v = cache_v.reshape(dec_batch, depth, past, KV_WIDTH)
    zero_states = jnp.zeros((batch, HG_HEADS, HG_DV, HG_DK), F32)

    x = (x_prompt.reshape(lay.p_rows, d), x_sample.reshape(lay.s_rows, d))
    ks_out, vs_out, sf_out, sb_out = [], [], [], []
    for l in range(depth):
        z, h = _in_projection(lay, l, x, mods, norm1_g, wl["w_in"])
        qn, kn, kf = _qk_prep(lay, l, z, rope_tabs, q_norm_g, k_norm_g)
        attn_p = _context_attention(lay, l, qn, kn, z, attn_sink[l])
        attn_s = _latent_attention(lay, l, qn, kn, z, ctx_k, ctx_v, attn_sink[l])
        u_b, u_c = _conv_mixers(lay, l, z, conf_dw_w, conf_dw_b, conf_ln_g, conf_ln_b, sc_conv_w)
        s0_f = jnp.concatenate([zero_states, jnp.swapaxes(state_hgrn_fwd[:, l], -1, -2)], axis=0)
        s0_b = jnp.concatenate([zero_states, jnp.swapaxes(state_hgrn_bwd[:, l], -1, -2)], axis=0)
        first = l == 0
        o_f, s_f, cast_f = _hgrn_direction(lay, l, z, lb[0, l][None], s0_f, reverse=False,
                                           cast=(w_gate,) if first else ())
        o_d, s_b, cast_b = _hgrn_direction(lay, l, z, lb[1, l][None], s0_b, reverse=True, other=o_f,
                                           norm_g=hg_norm_g, cast=(ffn_w1, ffn_w3, ffn_w2) if first else ())
        if first:
            wl.update(w_gate=(cast_f[0], 0), ffn_w1=(cast_b[0], 0), ffn_w3=(cast_b[1], 0), ffn_w2=(cast_b[2], 0))
        merged, cast = _merge(lay, l, h, attn_p, attn_s, (u_b, u_c, o_d), wl["w_gate"], b_gate,
                              (wb["w_attn_out"], wb["w_conf_out"], wb["w_sc_out"], wb["w_hg_out"]),
                              cast_next=tuple(big.values()) if l + 1 < depth else ())
        x, h2 = _out_projection(lay, l, merged, wb["w_o"], x, mods, norm2_g)
        t = _ffn_up(lay, l, h2, wl["ffn_w1"], wl["ffn_w3"])
        if l + 1 < depth:
            x = _residual_matmul(lay, l, t, wl["ffn_w2"], x, mods, MOD_GATE2, "ffn_down", tm=1024, tn=512)
            wl = {name: (w, 0) for name, w in zip(big, cast)}
        else:
            y_p, y_s = (_residual_matmul(lay, l, t, wl["ffn_w2"], x, mods, MOD_GATE2, "ffn_down", tm=1024, tn=512,
                                         rows=r) for r in ((0, lay.p_rows), (lay.p_rows, lay.rows)))
        ks_out.append(kf[:lay.p_rows].reshape(batch, seq, N_KV_A, HEAD_DIM))
        vs_out.append(z[:lay.p_rows, A_WIDTH + KV_WIDTH:A_WIDTH + 2 * KV_WIDTH].reshape(batch, seq, N_KV_A, HEAD_DIM))
        sf_out.append(jnp.swapaxes(s_f[:batch], -1, -2))
        sb_out.append(jnp.swapaxes(s_b[:batch], -1, -2))
    return (y_p.reshape(batch, seq, d), y_s.reshape(dec_batch, dec_seq, d),
            jnp.stack(ks_out, axis=1), jnp.stack(vs_out, axis=1),
            jnp.stack(sf_out, axis=1), jnp.stack(sb_out, axis=1))
```

```python
import functools

import jax
import jax.numpy as jnp
from jax import lax
from jax.experimental import pallas as pl
from jax.experimental.pallas import tpu as pltpu

F32 = jnp.float32
BF16 = jnp.bfloat16

V7X_VMEM_BYTES = 64 * 1024 * 1024
VMEM_LIMIT_BYTES = V7X_VMEM_BYTES - 8 * 1024 * 1024
LANES = 128
SUBLANES = 8
MXU_WIDTH = 256

GRID_W = 64
HEAD_DIM = 128
N_HEADS_A = 8
N_KV_A = 2
GROUP_A = N_HEADS_A // N_KV_A
A_WIDTH = N_HEADS_A * HEAD_DIM
KV_WIDTH = N_KV_A * HEAD_DIM
WINDOW = 128
BLOCK = 128
ROPE_BASE = 10000.0
ROT_AXIS = HEAD_DIM // 2
ATTN_SCALE = HEAD_DIM ** -0.5
NEG_MASK = -1e9
CONF_W = 512
CONF_K = 31
SC_W = 512
SC_K = 3
HG_HEADS = 4
HG_DK = 128
HG_DV = 128
HG_W = HG_HEADS * HG_DK
HG_CHUNK = 32
N_BRANCH = 4
EPS = 1e-6

COL_W = 512
COL_CONF_A, COL_CONF_G, COL_SC_B, COL_SC_C, COL_SC_H = 3, 4, 5, 6, 7
COL_HQ, COL_HF_FWD, COL_HF_BWD, COL_HI, COL_HG = 8, 9, 10, 11, 12
KV_COL_K, KV_COL_V = A_WIDTH // KV_WIDTH, A_WIDTH // KV_WIDTH + 1

ATTN_Q_ROWS = 512
SEQ_TILE = 256
HALO = 16
HG_SAFE_LOG_DECAY = -60.0

MOD_SHIFT1, MOD_SCALE1, MOD_GATE1, MOD_SHIFT2, MOD_SCALE2, MOD_GATE2 = range(6)
MOD_ROWS = 8


def _cparams(*semantics):
    return pltpu.CompilerParams(dimension_semantics=semantics, vmem_limit_bytes=VMEM_LIMIT_BYTES)


def _sigmoid(x):
    return 1.0 / (1.0 + jnp.exp(-x))


def _silu(x):
    return x * _sigmoid(x)


def _chunk_of(idx, size):
    assert size & (size - 1) == 0
    return jnp.right_shift(idx, size.bit_length() - 1)


def _dot(a, b):
    return jnp.dot(a, b, preferred_element_type=F32)


def _dot_nt(a, b):
    return lax.dot_general(a, b, (((1,), (1,)), ((), ())), preferred_element_type=F32)


class _Layout:
    def __init__(self, batch, seq, dec_batch, dec_seq):
        self.batch, self.seq, self.dec_batch, self.dec_seq = batch, seq, dec_batch, dec_seq
        self.p_rows = batch * seq
        self.s_rows = dec_batch * dec_seq
        self.rows = self.p_rows + self.s_rows
        self.n_seq = batch + dec_batch

    def row_tile(self, preferred):
        t = preferred
        while self.p_rows % t or self.dec_seq % t:
            t //= 2
        return t

    def mod_row(self, i, tile):
        p_tiles = self.p_rows // tile
        per = self.dec_seq // tile
        return jnp.where(i < p_tiles, 0, 1 + (i - p_tiles) // per)

    def seq_pos(self, n, tile):
        p_tiles = self.p_rows // tile
        tpp = self.seq // tile
        tps = self.dec_seq // tile
        is_p = n < p_tiles
        seq_idx = jnp.where(is_p, n // tpp, self.batch + (n - p_tiles) // tps)
        pos = jnp.where(is_p, n % tpp, (n - p_tiles) % tps)
        count = jnp.where(is_p, tpp, tps)
        return seq_idx, pos, count


def _mod_kernel(c_ref, w_ref, b_ref, o_ref):
    s = _silu(c_ref[...]).astype(BF16)
    o_ref[...] = _dot(s, w_ref[...].astype(BF16)) + b_ref[...]


def _modulation(cvec, ada_w, ada_b):
    depth, d, n = ada_w.shape
    tn = 1024
    return pl.pallas_call(
        _mod_kernel,
        grid=(depth, n // tn),
        in_specs=[
            pl.BlockSpec((MOD_ROWS, d), lambda l, j: (0, 0)),
            pl.BlockSpec((None, d, tn), lambda l, j: (l, 0, j)),
            pl.BlockSpec((None, 1, tn), lambda l, j: (l, 0, j)),
        ],
        out_specs=pl.BlockSpec((None, MOD_ROWS, tn), lambda l, j: (l, 0, j)),
        out_shape=jax.ShapeDtypeStruct((depth, MOD_ROWS, n), F32),
        compiler_params=_cparams("parallel", "parallel"),
        name="modulation",
    )(cvec, ada_w, ada_b.reshape(depth, 1, n))


NORM_ROWS = 16


def _modulated_norm(x_ref, g_ref, mod_ref, shift_row, scale_row, h_ref):
    gain = g_ref[...] * (1.0 + mod_ref[scale_row:scale_row + 1, :])
    shift = mod_ref[shift_row:shift_row + 1, :]
    block = 8 * NORM_ROWS

    def body(c, carry):
        start = pl.multiple_of(c * block, block)
        for r0 in range(0, block, NORM_ROWS):
            rows = pl.ds(start + r0, NORM_ROWS)
            x = x_ref[rows, :]
            y = x * lax.rsqrt(jnp.mean(x * x, axis=-1, keepdims=True) + EPS)
            h_ref[rows, :] = (y * gain + shift).astype(BF16)
        return carry

    lax.fori_loop(0, x_ref.shape[0] // block, body, 0)


def _split_rows_specs(lay, tm, width, tile_of):
    p_tiles = lay.p_rows // tm
    s_tiles = lay.s_rows // tm
    return [pl.BlockSpec((tm, width), lambda *g: (jnp.minimum(tile_of(*g), p_tiles - 1), 0)),
            pl.BlockSpec((tm, width), lambda *g: (jnp.clip(tile_of(*g) - p_tiles, 0, s_tiles - 1), 0))]


def _inproj_kernel(*refs, p_tiles):
    x_refs, (mod_ref, g_ref, w_ref, z_ref, h_ref) = refs[:-5], refs[-5:]
    i = pl.program_id(0)
    first_step = pl.program_id(1) == 0
    for x_ref, mine in zip(x_refs, (True,) if p_tiles is None else (i < p_tiles, i >= p_tiles)):
        @pl.when(first_step & mine)
        def _():
            _modulated_norm(x_ref, g_ref, mod_ref, MOD_SHIFT1, MOD_SCALE1, h_ref)

    z_ref[...] = _dot(h_ref[...], w_ref[...])


def _layer_of(w, l):
    return w if isinstance(w, tuple) else (w, l)


def _in_projection(lay, l, x, mods, norm_g, w_in):
    w_in, lw = _layer_of(w_in, l)
    pair = isinstance(x, tuple)
    m, d = lay.rows, w_in.shape[1]
    n = w_in.shape[-1]
    tm = lay.row_tile(1024)
    tn = 512
    x_specs = (_split_rows_specs(lay, tm, d, lambda i, j: i) if pair
               else [pl.BlockSpec((tm, d), lambda i, j: (i, 0))])
    return pl.pallas_call(
        functools.partial(_inproj_kernel, p_tiles=lay.p_rows // tm if pair else None),
        grid=(m // tm, n // tn),
        in_specs=x_specs + [
            pl.BlockSpec((None, None, MOD_ROWS, d), lambda i, j: (l, lay.mod_row(i, tm), 0, 0)),
            pl.BlockSpec((None, 1, d), lambda i, j: (l, 0, 0)),
            pl.BlockSpec((None, d, tn), lambda i, j: (lw, 0, j)),
        ],
        out_specs=[
            pl.BlockSpec((tm, tn), lambda i, j: (i, j)),
            pl.BlockSpec((tm, d), lambda i, j: (i, 0)),
        ],
        out_shape=[jax.ShapeDtypeStruct((m, n), F32), jax.ShapeDtypeStruct((m, d), BF16)],
        compiler_params=_cparams("parallel", "arbitrary"),
        name="in_projection",
    )(*(x if pair else (x,)), mods, norm_g.reshape(norm_g.shape[0], 1, d), w_in)


def _residual_matmul_kernel(a_ref, w_ref, x_ref, mod_ref, o_ref, *, gate_row):
    a = a_ref[...]
    chunk = 512
    for c0 in range(0, o_ref.shape[1], chunk):
        sl = slice(c0, c0 + chunk)
        o_ref[:, sl] = x_ref[:, sl] + mod_ref[gate_row:gate_row + 1, sl] * _dot(a, w_ref[:, sl])


def _residual_matmul(lay, l, a, w, x, mods, gate_row, name, tm, tn, rows=None):
    w, lw = _layer_of(w, l)
    m, k = a.shape
    d = w.shape[-1]
    tm = lay.row_tile(tm)
    first, stop = (0, m) if rows is None else rows
    assert first % tm == 0 and stop % tm == 0
    t0 = first // tm
    return pl.pallas_call(
        functools.partial(_residual_matmul_kernel, gate_row=gate_row),
        grid=((stop - first) // tm, d // tn),
        in_specs=[
            pl.BlockSpec((tm, k), lambda i, j: (t0 + i, 0)),
            pl.BlockSpec((None, k, tn), lambda i, j: (lw, 0, j)),
            pl.BlockSpec((tm, tn), lambda i, j: (t0 + i, j)),
            pl.BlockSpec((None, None, MOD_ROWS, tn), lambda i, j: (l, lay.mod_row(t0 + i, tm), 0, j)),
        ],
        out_specs=pl.BlockSpec((tm, tn), lambda i, j: (i, j)),
        out_shape=jax.ShapeDtypeStruct((stop - first, d), F32),
        compiler_params=_cparams("parallel", "arbitrary"),
        name=name,
    )(a, w, x, mods)


def _out_proj_kernel(a_ref, w_ref, *refs, p_tiles):
    x_refs, (mod_ref, g_ref, o_ref, h_ref) = refs[:-4], refs[-4:]
    i = pl.program_id(0)
    a = a_ref[...]
    chunk = 512
    for c0 in range(0, o_ref.shape[1], chunk):
        sl = slice(c0, c0 + chunk)
        update = mod_ref[MOD_GATE1:MOD_GATE1 + 1, sl] * _dot(a, w_ref[:, sl])
        if p_tiles is None:
            o_ref[:, sl] = x_refs[0][:, sl] + update
        else:
            o_ref[:, sl] = jnp.where(i < p_tiles, x_refs[0][:, sl], x_refs[1][:, sl]) + update
    _modulated_norm(o_ref, g_ref, mod_ref, MOD_SHIFT2, MOD_SCALE2, h_ref)


def _out_projection(lay, l, a, w_o, x, mods, norm2_g):
    w_o, lw = _layer_of(w_o, l)
    pair = isinstance(x, tuple)
    m, k = a.shape
    d = w_o.shape[-1]
    tm = lay.row_tile(512)
    x_specs = _split_rows_specs(lay, tm, d, lambda i: i) if pair else [pl.BlockSpec((tm, d), lambda i: (i, 0))]
    return pl.pallas_call(
        functools.partial(_out_proj_kernel, p_tiles=lay.p_rows // tm if pair else None),
        grid=(m // tm,),
        in_specs=[
            pl.BlockSpec((tm, k), lambda i: (i, 0)),
            pl.BlockSpec((None, k, d), lambda i: (lw, 0, 0)),
        ] + x_specs + [
            pl.BlockSpec((None, None, MOD_ROWS, d), lambda i: (l, lay.mod_row(i, tm), 0, 0)),
            pl.BlockSpec((None, 1, d), lambda i: (l, 0, 0)),
        ],
        out_specs=[pl.BlockSpec((tm, d), lambda i: (i, 0)), pl.BlockSpec((tm, d), lambda i: (i, 0))],
        out_shape=[jax.ShapeDtypeStruct((m, d), F32), jax.ShapeDtypeStruct((m, d), BF16)],
        compiler_params=_cparams("parallel"),
        name="out_projection",
    )(a, w_o, *(x if pair else (x,)), mods, norm2_g.reshape(norm2_g.shape[0], 1, d))


def _ffn_up_kernel(h_ref, w1_ref, w3_ref, o_ref):
    h = h_ref[...]
    tn = o_ref.shape[1]
    tail = tn % MXU_WIDTH
    c0 = 0
    while c0 < tn - tail:
        c1 = min(c0 + 512, tn - tail)
        o_ref[:, c0:c1] = (_silu(_dot(h, w1_ref[:, c0:c1])) * _dot(h, w3_ref[:, c0:c1])).astype(BF16)
        c0 = c1
    if tail:
        both = _dot(h, jnp.concatenate([w1_ref[:, c0:], w3_ref[:, c0:]], axis=1))
        o_ref[:, c0:] = (_silu(both[:, :tail]) * both[:, tail:]).astype(BF16)


def _ffn_up(lay, l, h, w1, w3):
    (w1, l1), (w3, l3) = _layer_of(w1, l), _layer_of(w3, l)
    m, d = h.shape
    n = w1.shape[-1]
    tm = lay.row_tile(1024)
    tn = n // 4
    assert n % 4 == 0 and tn % LANES == 0
    return pl.pallas_call(
        _ffn_up_kernel,
        grid=(m // tm, n // tn),
        in_specs=[
            pl.BlockSpec((tm, d), lambda i, j: (i, 0)),
            pl.BlockSpec((None, d, tn), lambda i, j: (l1, 0, j)),
            pl.BlockSpec((None, d, tn), lambda i, j: (l3, 0, j)),
        ],
        out_specs=pl.BlockSpec((tm, tn), lambda i, j: (i, j)),
        out_shape=jax.ShapeDtypeStruct((m, n), BF16),
        compiler_params=_cparams("parallel", "parallel"),
        name="ffn_up",
    )(h, w1, w3)


def _qk_prep_kernel(q_ref, k_ref, cos_ref, sa_ref, sb_ref, qg_ref, kg_ref, qo_ref, ko_ref, kf_ref):
    cos, sin_a, sin_b = cos_ref[...], sa_ref[...], sb_ref[...]

    ones = jnp.ones((HEAD_DIM, HEAD_DIM), BF16)

    def norm(x, g):
        mean_sq = _dot((x * x).astype(BF16), ones) * (1.0 / HEAD_DIM)
        return x * lax.rsqrt(mean_sq + EPS) * g

    def rope(x):
        return (x * cos + pltpu.roll(x, HEAD_DIM - ROT_AXIS // 2, 1) * sin_a
                + pltpu.roll(x, ROT_AXIS // 2, 1) * sin_b)

    for h in range(N_HEADS_A):
        sl = slice(h * HEAD_DIM, (h + 1) * HEAD_DIM)
        qo_ref[:, sl] = (rope(norm(q_ref[:, sl], qg_ref[...])) * ATTN_SCALE).astype(BF16)
    for h in range(N_KV_A):
        sl = slice(h * HEAD_DIM, (h + 1) * HEAD_DIM)
        kn = norm(k_ref[:, sl], kg_ref[...])
        kf_ref[:, sl] = kn
        ko_ref[:, sl] = rope(kn).astype(BF16)


def _qk_prep(lay, l, z, rope_tabs, q_norm_g, k_norm_g):
    m = z.shape[0]
    tm = lay.row_tile(512)
    cos, sin_a, sin_b = rope_tabs
    tab_spec = pl.BlockSpec((tm, HEAD_DIM), lambda i: (i, 0))
    g_spec = pl.BlockSpec((None, 1, HEAD_DIM), lambda i: (l, 0, 0))
    return pl.pallas_call(
        _qk_prep_kernel,
        grid=(m // tm,),
        in_specs=[
            pl.BlockSpec((tm, A_WIDTH), lambda i: (i, 0)),
            pl.BlockSpec((tm, KV_WIDTH), lambda i: (i, KV_COL_K)),
            tab_spec, tab_spec, tab_spec, g_spec, g_spec,
        ],
        out_specs=[
            pl.BlockSpec((tm, A_WIDTH), lambda i: (i, 0)),
            pl.BlockSpec((tm, KV_WIDTH), lambda i: (i, 0)),
            pl.BlockSpec((tm, KV_WIDTH), lambda i: (i, 0)),
        ],
        out_shape=[
            jax.ShapeDtypeStruct((m, A_WIDTH), BF16),
            jax.ShapeDtypeStruct((m, KV_WIDTH), BF16),
            jax.ShapeDtypeStruct((m, KV_WIDTH), F32),
        ],
        compiler_params=_cparams("parallel"),
        name="qk_prep",
    )(z, z, cos, sin_a, sin_b, q_norm_g.reshape(-1, 1, HEAD_DIM), k_norm_g.reshape(-1, 1, HEAD_DIM))


def _stack_heads(q_ref, kv):
    return jnp.concatenate(
        [q_ref[:, (kv * GROUP_A + g) * HEAD_DIM:(kv * GROUP_A + g + 1) * HEAD_DIM] for g in range(GROUP_A)], axis=0)


def _fold_lane_tiles(x, op):
    out = x[:, :LANES]
    for c0 in range(LANES, x.shape[1], LANES):
        out = op(out, x[:, c0:c0 + LANES])
    return out


def _sink_column(sink_ref, kv, rows):
    return jnp.concatenate(
        [jnp.full((rows, 1), sink_ref[kv * GROUP_A + g], F32) for g in range(GROUP_A)], axis=0)


def _context_attention_kernel(sink_ref, q_ref, k_ref, v_ref, o_ref, *, seq):
    for r0 in range(0, q_ref.shape[0], seq):
        rows = slice(r0, r0 + seq)
        for kv in range(N_KV_A):
            sl = slice(kv * HEAD_DIM, (kv + 1) * HEAD_DIM)
            q = jnp.concatenate(
                [q_ref[rows, (kv * GROUP_A + g) * HEAD_DIM:(kv * GROUP_A + g + 1) * HEAD_DIM]
                 for g in range(GROUP_A)], axis=0)
            s = _dot_nt(q, k_ref[rows, sl])
            sink = _sink_column(sink_ref, kv, seq)
            mx = jnp.maximum(jnp.max(_fold_lane_tiles(s, jnp.maximum), axis=-1, keepdims=True), sink)
            p = jnp.exp(s - mx)
            den = jnp.sum(_fold_lane_tiles(p, jnp.add), axis=-1, keepdims=True) + jnp.exp(sink - mx)
            o = _dot(p.astype(BF16), v_ref[rows, sl].astype(BF16)) * (1.0 / den)
            for g in range(GROUP_A):
                h = kv * GROUP_A + g
                o_ref[rows, h * HEAD_DIM:(h + 1) * HEAD_DIM] = o[g * seq:(g + 1) * seq].astype(BF16)


def _context_attention(lay, l, qn, kn, z, sink):
    per_step = next(n for n in (4, 2, 1) if lay.batch % n == 0)
    seq = lay.seq
    rows = per_step * seq
    return pl.pallas_call(
        functools.partial(_context_attention_kernel, seq=seq),
        grid=(lay.batch // per_step,),
        in_specs=[
            pl.BlockSpec(memory_space=pltpu.SMEM),
            pl.BlockSpec((rows, A_WIDTH), lambda b: (b, 0)),
            pl.BlockSpec((rows, KV_WIDTH), lambda b: (b, 0)),
            pl.BlockSpec((rows, KV_WIDTH), lambda b: (b, KV_COL_V)),
        ],
        out_specs=pl.BlockSpec((rows, A_WIDTH), lambda b: (b, 0)),
        out_shape=jax.ShapeDtypeStruct((lay.p_rows, A_WIDTH), BF16),
        compiler_params=_cparams("parallel"),
        name="context_attention",
    )(sink, qn, kn, z)


def _latent_attention_kernel(sink_ref, q_ref, kp_ref, kc_ref, kn_ref, vp_ref, vc_ref, vn_ref, ck_ref, cv_ref,
                             o_ref, *, seq_len):
    assert WINDOW <= BLOCK
    q_rows = q_ref.shape[0]
    j = pl.program_id(1)
    n_tiles = seq_len // q_rows
    r = lax.broadcasted_iota(jnp.int32, (q_rows, BLOCK), 0)
    c = lax.broadcasted_iota(jnp.int32, (q_rows, BLOCK), 1)
    valid_prev = jnp.concatenate([(r - (c - BLOCK) <= WINDOW) & (j >= 1)] * GROUP_A, axis=0)
    valid_next = jnp.concatenate([((c + q_rows) - r <= WINDOW) & (j <= n_tiles - 2)] * GROUP_A, axis=0)
    valid_own = None
    if q_rows - 1 > WINDOW:
        ro = lax.broadcasted_iota(jnp.int32, (q_rows, q_rows), 0)
        co = lax.broadcasted_iota(jnp.int32, (q_rows, q_rows), 1)
        valid_own = jnp.concatenate([jnp.abs(co - ro) <= WINDOW] * GROUP_A, axis=0)
    for kv in range(N_KV_A):
        sl = slice(kv * HEAD_DIM, (kv + 1) * HEAD_DIM)
        q = _stack_heads(q_ref, kv)
        s_p = jnp.where(valid_prev, _dot_nt(q, kp_ref[:, sl]), NEG_MASK)
        s_o = _dot_nt(q, kc_ref[:, sl])
        if valid_own is not None:
            s_o = jnp.where(valid_own, s_o, NEG_MASK)
        s_n = jnp.where(valid_next, _dot_nt(q, kn_ref[:, sl]), NEG_MASK)
        s_c = _dot_nt(q, ck_ref[:, sl].astype(BF16))
        sink = _sink_column(sink_ref, kv, q_rows)
        mx = jnp.maximum(jnp.maximum(s_p, _fold_lane_tiles(s_o, jnp.maximum)),
                         jnp.maximum(s_n, _fold_lane_tiles(s_c, jnp.maximum)))
        mx = jnp.maximum(jnp.max(mx, axis=-1, keepdims=True), sink)
        psum = None
        o = None
        for s, v_ref in ((s_p, vp_ref), (s_o, vc_ref), (s_n, vn_ref), (s_c, cv_ref)):
            p = jnp.exp(s - mx)
            pf = _fold_lane_tiles(p, jnp.add)
            psum = pf if psum is None else psum + pf
            term = _dot(p.astype(BF16), v_ref[:, sl].astype(BF16))
            o = term if o is None else o + term
        den = jnp.sum(psum, axis=-1, keepdims=True) + jnp.exp(sink - mx)
        o = o * (1.0 / den)
        for g in range(GROUP_A):
            h = kv * GROUP_A + g
            o_ref[:, h * HEAD_DIM:(h + 1) * HEAD_DIM] = o[g * q_rows:(g + 1) * q_rows].astype(BF16)


def _latent_attention(lay, l, qn, kn, z, ctx_k, ctx_v, sink):
    q_rows = ATTN_Q_ROWS
    per_tile = q_rows // BLOCK
    assert lay.p_rows % q_rows == 0 and lay.dec_seq % q_rows == 0
    nb = lay.dec_seq // BLOCK
    nt = lay.dec_seq // q_rows
    base = lay.p_rows // BLOCK
    past = ctx_k.shape[2]

    def own(col):
        return lambda b, j: (lay.p_rows // q_rows + b * nt + j, col)

    def prev(col):
        return lambda b, j: (base + b * nb + jnp.maximum(j * per_tile - 1, 0), col)

    def nxt(col):
        return lambda b, j: (base + b * nb + jnp.minimum((j + 1) * per_tile, nb - 1), col)

    def trio(col):
        return [pl.BlockSpec((BLOCK, KV_WIDTH), prev(col)), pl.BlockSpec((q_rows, KV_WIDTH), own(col)),
                pl.BlockSpec((BLOCK, KV_WIDTH), nxt(col))]

    k_spec = trio(0)
    v_spec = trio(KV_COL_V)
    ctx_spec = pl.BlockSpec((None, None, past, KV_WIDTH), lambda b, j: (b, l, 0, 0))
    return pl.pallas_call(
        functools.partial(_latent_attention_kernel, seq_len=lay.dec_seq),
        grid=(lay.dec_batch, nt),
        in_specs=[pl.BlockSpec(memory_space=pltpu.SMEM), pl.BlockSpec((q_rows, A_WIDTH), own(0))]
        + k_spec + v_spec + [ctx_spec, ctx_spec],
        out_specs=pl.BlockSpec((q_rows, A_WIDTH), lambda b, j: (b * nt + j, 0)),
        out_shape=jax.ShapeDtypeStruct((lay.s_rows, A_WIDTH), BF16),
        compiler_params=_cparams("parallel", "parallel"),
        name="latent_attention",
    )(sink, qn, kn, kn, kn, z, z, z, ctx_k, ctx_v)


def _conv_kernel(a_c, a_p, a_n, g_c, g_p, g_n, b_c, c_c, c_p, c_n, h_c, h_p, h_n,
                 dw_ref, db_ref, lg_ref, lb_ref, sw_ref, ub_ref, uc_ref, xb_s, xc_s, *, lay):
    _, pos, count = lay.seq_pos(pl.program_id(0), SEQ_TILE)
    has_prev = pos > 0
    has_next = pos < count - 1
    rb = 32
    xb_s[0, 0:HALO, :] = jnp.where(has_prev, a_p[...] * _sigmoid(g_p[...]), 0.0)
    xb_s[0, HALO + SEQ_TILE:, :] = jnp.where(has_next, a_n[...] * _sigmoid(g_n[...]), 0.0)
    xc_s[0:HALO, :] = jnp.where(has_prev, c_p[...] * h_p[...], 0.0)
    xc_s[HALO + SEQ_TILE:, :] = jnp.where(has_next, c_n[...] * h_n[...], 0.0)
    for r0 in range(0, SEQ_TILE, rb):
        rows = slice(r0, r0 + rb)
        xb_s[0, HALO + r0:HALO + r0 + rb, :] = a_c[rows, :] * _sigmoid(g_c[rows, :])
        xc_s[HALO + r0:HALO + r0 + rb, :] = c_c[rows, :] * h_c[rows, :]
    shifted_rows = SEQ_TILE + 2 * HALO - SUBLANES
    chunk = 40
    for s in range(1, SUBLANES):
        for r0 in range(0, shifted_rows, chunk):
            xb_s[s, r0:r0 + chunk, :] = xb_s[0, r0 + s:r0 + s + chunk, :]
    for r0 in range(0, SEQ_TILE, rb):
        acc = jnp.zeros((rb // SUBLANES, SUBLANES, CONF_W), F32)
        for k in range(CONF_K):
            off = HALO - CONF_K // 2 + k
            row = r0 + off - off % SUBLANES
            x = xb_s[off % SUBLANES, row:row + rb, :]
            acc = acc + dw_ref[k] * x.reshape(rb // SUBLANES, SUBLANES, CONF_W)
        u = acc.reshape(rb, CONF_W) + db_ref[...]
        uc = u - jnp.mean(u, axis=-1, keepdims=True)
        var = jnp.mean(uc * uc, axis=-1, keepdims=True)
        u = uc * lax.rsqrt(var + EPS) * lg_ref[...] + lb_ref[...]
        ub_ref[r0:r0 + rb, :] = _silu(u).astype(BF16)
        acc = jnp.zeros((rb // SUBLANES, SUBLANES, SC_W), F32)
        for k in range(SC_K):
            off = HALO - SC_K // 2 + r0 + k
            acc = acc + sw_ref[k] * xc_s[off:off + rb, :].reshape(rb // SUBLANES, SUBLANES, SC_W)
        uc_ref[r0:r0 + rb, :] = (b_c[r0:r0 + rb, :] * acc.reshape(rb, SC_W)).astype(BF16)


def _conv_mixers(lay, l, z, conf_dw_w, conf_dw_b, conf_ln_g, conf_ln_b, sc_conv_w):
    m = z.shape[0]
    n_tiles = m // SEQ_TILE
    per = SEQ_TILE // HALO
    n_halo = m // HALO

    def cur(col):
        return pl.BlockSpec((SEQ_TILE, COL_W), lambda i: (i, col))

    def prev(col):
        return pl.BlockSpec((HALO, COL_W), lambda i: (jnp.maximum(i * per - 1, 0), col))

    def nxt(col):
        return pl.BlockSpec((HALO, COL_W), lambda i: (jnp.minimum((i + 1) * per, n_halo - 1), col))

    def trio(col):
        return [cur(col), prev(col), nxt(col)]

    def par(rows):
        return pl.BlockSpec((None, rows, COL_W), lambda i: (l, 0, 0))

    def taps(n):
        return pl.BlockSpec((None, n, SUBLANES, COL_W), lambda i: (l, 0, 0, 0))

    depth = conf_dw_w.shape[0]
    dw = jnp.broadcast_to(conf_dw_w[:, :, None, :], (depth, CONF_K, SUBLANES, CONF_W))
    sw = jnp.broadcast_to(sc_conv_w[:, :, None, :], (depth, SC_K, SUBLANES, SC_W))
    return pl.pallas_call(
        functools.partial(_conv_kernel, lay=lay),
        grid=(n_tiles,),
        in_specs=trio(COL_CONF_A) + trio(COL_CONF_G) + [cur(COL_SC_B)] + trio(COL_SC_C) + trio(COL_SC_H)
        + [taps(CONF_K), par(1), par(1), par(1), taps(SC_K)],
        out_specs=[pl.BlockSpec((SEQ_TILE, CONF_W), lambda i: (i, 0)),
                   pl.BlockSpec((SEQ_TILE, SC_W), lambda i: (i, 0))],
        out_shape=[jax.ShapeDtypeStruct((m, CONF_W), BF16), jax.ShapeDtypeStruct((m, SC_W), BF16)],
        scratch_shapes=[pltpu.VMEM((SUBLANES, SEQ_TILE + 2 * HALO, CONF_W), F32),
                        pltpu.VMEM((SEQ_TILE + 2 * HALO, SC_W), F32)],
        compiler_params=_cparams("parallel"),
        name="conv_mixers",
    )(*([z] * 13), dw, conf_dw_b.reshape(depth, 1, CONF_W), conf_ln_g.reshape(depth, 1, CONF_W),
      conf_ln_b.reshape(depth, 1, CONF_W), sw)


def _hgrn_gates(hq_ref, hf_ref, lb_ref, sl, tri_b3):
    zq = hq_ref[:, sl]
    q = _silu(zq) * (HG_DK ** -0.5)
    zf = hf_ref[:, sl]
    lb = lb_ref[:, sl]
    e = jnp.exp(-jnp.abs(zf))
    r = 1.0 / (1.0 + e)
    nonneg = zf >= 0
    sig_pos = jnp.where(nonneg, r, e * r)
    sig_neg = jnp.where(nonneg, e * r, r)
    logf = jnp.log(lb + (1.0 - lb) * sig_pos)
    kk = (1.0 - lb) * sig_neg
    g1 = logf.astype(BF16)
    r1 = logf - g1.astype(F32)
    g2 = r1.astype(BF16)
    g3 = (r1 - g2.astype(F32)).astype(BF16)
    b = _dot(tri_b3, jnp.concatenate([g1, g2, g3], axis=0))
    return q, kk, b


def _block_place(x, n_chunks):
    w = x.shape[1]
    zeros = jnp.zeros((HG_CHUNK, w), x.dtype)
    row_blocks = []
    for c in range(n_chunks):
        cols = [zeros] * n_chunks
        cols[c] = x[c * HG_CHUNK:(c + 1) * HG_CHUNK, :]
        row_blocks.append(jnp.concatenate(cols, axis=1))
    return jnp.concatenate(row_blocks, axis=0)


def _hgrn_exact_scores(a_ref, q_s, k_s, b_s, reverse):
    ts = a_ref.shape[0]
    a_ref[...] = jnp.zeros_like(a_ref)

    def body(s, carry):
        c0 = pl.multiple_of((s // HG_CHUNK) * HG_CHUNK, HG_CHUNK)
        w = jnp.exp(jnp.minimum(b_s[pl.ds(c0, HG_CHUNK), :] - b_s[pl.ds(s, 1), :], 0.0))
        prod = q_s[pl.ds(c0, HG_CHUNK), :] * k_s[pl.ds(s, 1), :] * w
        trow = c0 + lax.broadcasted_iota(jnp.int32, (HG_CHUNK, 1), 0)
        ok = (trow <= s) if reverse else (trow >= s)
        onehot = (lax.broadcasted_iota(jnp.int32, (1, ts), 1) == s).astype(F32)
        colv = jnp.sum(prod, axis=-1, keepdims=True)
        a_ref[pl.ds(c0, HG_CHUNK), :] += jnp.where(ok, colv, 0.0) * onehot
        return carry

    lax.fori_loop(0, ts, body, 0)


def _hgrn_kernel(hq_ref, hf_ref, hi_ref, lb_ref, s0_ref, *refs, lay, reverse, finish, cast_steps):
    ts = SEQ_TILE
    n_chunks = ts // HG_CHUNK
    _, pos, count = lay.seq_pos(pl.program_id(0), ts)
    other_ref, gate_ref, gain_ref = refs[:3] if finish else (None, None, None)
    refs = refs[3:] if finish else refs
    n_cast = len(cast_steps)
    src_refs, (o_ref, sf_ref), dst_refs = refs[:n_cast], refs[n_cast:n_cast + 2], refs[n_cast + 2:2 * n_cast + 2]
    st_ref, oi_s, a_ref, q_s, k_s, b_s = refs[2 * n_cast + 2:]
    _side_cast(src_refs, dst_refs, cast_steps, pl.program_id(0))

    def emit(o, sl):
        if finish:
            o = o + other_ref[:, sl]
            o = o * lax.rsqrt(jnp.mean(o * o, axis=-1, keepdims=True) + EPS) * gain_ref[...]
            o_ref[:, sl] = (o * _silu(gate_ref[:, sl])).astype(BF16)
        else:
            o_ref[:, sl] = o

    @pl.when(pos == 0)
    def _():
        st_ref[...] = s0_ref[...]

    row = lax.broadcasted_iota(jnp.int32, (ts, ts), 0)
    col = lax.broadcasted_iota(jnp.int32, (ts, ts), 1)
    same = _chunk_of(row, HG_CHUNK) == _chunk_of(col, HG_CHUNK)
    tri = same & ((col >= row) if reverse else (col <= row))
    tri_b = tri.astype(F32).astype(BF16)
    tri_b3 = jnp.concatenate([tri_b] * 3, axis=1)
    order = range(n_chunks - 1, -1, -1) if reverse else range(n_chunks)
    last = 0 if reverse else HG_CHUNK - 1

    b_min = None
    for hd in range(HG_HEADS):
        sl = slice(hd * HG_DK, (hd + 1) * HG_DK)
        q, kk, b = _hgrn_gates(hq_ref, hf_ref, lb_ref, sl, tri_b3)
        head_min = jnp.min(b)
        b_min = head_min if b_min is None else jnp.minimum(b_min, head_min)
        b3 = b.reshape(n_chunks, HG_CHUNK, HG_DK)
        total3 = b3[:, last:last + 1, :]
        qt = (q * jnp.exp(b)).astype(BF16)
        kend = (kk * jnp.exp(jnp.broadcast_to(total3, b3.shape).reshape(ts, HG_DK) - b)).astype(BF16)
        decay = jnp.exp(total3.reshape(n_chunks, HG_DK))
        v_h = hi_ref[:, sl]
        kinv = (kk * jnp.exp(-b)).astype(BF16)
        scores = jnp.where(tri, _dot_nt(qt, kinv), 0.0).astype(BF16)
        o_intra = _dot(scores, v_h.astype(BF16))
        incr = _dot(v_h.T.astype(BF16), _block_place(kend, n_chunks))
        state = st_ref[hd]
        starts = [None] * n_chunks
        for c in order:
            starts[c] = state
            state = state * decay[c:c + 1, :] + incr[:, c * HG_DK:(c + 1) * HG_DK]
        st_ref[hd] = state
        o_inter = jnp.concatenate(
            [_dot_nt(qt[c * HG_CHUNK:(c + 1) * HG_CHUNK, :], starts[c].astype(BF16)) for c in range(n_chunks)],
            axis=0)
        oi_s[:, sl] = o_inter
        emit(o_intra + o_inter, sl)

    @pl.when(b_min <= HG_SAFE_LOG_DECAY)
    def _():
        for hd in range(HG_HEADS):
            sl = slice(hd * HG_DK, (hd + 1) * HG_DK)
            q, kk, b = _hgrn_gates(hq_ref, hf_ref, lb_ref, sl, tri_b3)
            q_s[...] = q
            k_s[...] = kk
            b_s[...] = b
            _hgrn_exact_scores(a_ref, q_s, k_s, b_s, reverse)
            emit(_dot(a_ref[...].astype(BF16), hi_ref[:, sl].astype(BF16)) + oi_s[:, sl], sl)

    @pl.when(pos == count - 1)
    def _():
        sf_ref[...] = st_ref[...]


def _hgrn_direction(lay, l, z, lb, s0_t, reverse, other=None, norm_g=None, cast=()):
    finish = other is not None
    m = z.shape[0]
    n_tiles = m // SEQ_TILE
    cast_specs, cast_out_specs, cast_shapes, cast_steps = _side_cast_specs(cast, l, n_tiles, lambda n: n)

    def tile_of(n):
        seq_idx, pos, count = lay.seq_pos(n, SEQ_TILE)
        return (n - pos + (count - 1 - pos)) if reverse else n

    def zcol(col):
        return pl.BlockSpec((SEQ_TILE, COL_W), lambda n: (tile_of(n), col))

    state_spec = pl.BlockSpec((None, HG_HEADS, HG_DV, HG_DK), lambda n: (lay.seq_pos(n, SEQ_TILE)[0], 0, 0, 0))
    o_spec = pl.BlockSpec((SEQ_TILE, HG_W), lambda n: (tile_of(n), 0))
    finish_specs, finish_args = [], []
    if finish:
        finish_specs = [o_spec, zcol(COL_HG), pl.BlockSpec((None, 1, HG_DV), lambda n: (l, 0, 0))]
        finish_args = [other, z, norm_g.reshape(-1, 1, HG_DV)]
    out = pl.pallas_call(
        functools.partial(_hgrn_kernel, lay=lay, reverse=reverse, finish=finish, cast_steps=cast_steps),
        grid=(n_tiles,),
        in_specs=[zcol(COL_HQ), zcol(COL_HF_BWD if reverse else COL_HF_FWD), zcol(COL_HI),
                  pl.BlockSpec((1, HG_W), lambda n: (0, 0)), state_spec] + finish_specs + cast_specs,
        out_specs=[o_spec, state_spec] + cast_out_specs,
        out_shape=[jax.ShapeDtypeStruct((m, HG_W), BF16 if finish else F32),
                   jax.ShapeDtypeStruct((lay.n_seq, HG_HEADS, HG_DV, HG_DK), F32)] + cast_shapes,
        scratch_shapes=[pltpu.VMEM((HG_HEADS, HG_DV, HG_DK), F32),
                        pltpu.VMEM((SEQ_TILE, HG_W), F32),
                        pltpu.VMEM((SEQ_TILE, SEQ_TILE), F32),
                        pltpu.VMEM((SEQ_TILE, HG_DK), F32),
                        pltpu.VMEM((SEQ_TILE, HG_DK), F32),
                        pltpu.VMEM((SEQ_TILE, HG_DK), F32)],
        compiler_params=_cparams("arbitrary"),
        name="hgrn_bwd" if reverse else "hgrn_fwd",
    )(z, z, z, lb, s0_t, *finish_args, *cast)
    return out[0], out[1], out[2:]


def _merge_kernel(h_ref, ap_ref, as_ref, b_ref, c_ref, d_ref, wg0, wg1, wg2, wg3, bg0, bg1, bg2, bg3,
                  wa_ref, wb_ref, wc_ref, wd_ref, *refs, p_tiles, cast_steps):
    n_cast = len(cast_steps)
    src_refs, o_ref, dst_refs = refs[:n_cast], refs[n_cast], refs[n_cast + 1:]
    _side_cast(src_refs, dst_refs, cast_steps, pl.program_id(0) * pl.num_programs(1) + pl.program_id(1))

    h = h_ref[...]
    attn = jnp.where(pl.program_id(1) < p_tiles, ap_ref[...], as_ref[...])
    acc = None
    for branch, w_ref, wg_ref, bg_ref in ((attn, wa_ref, wg0, bg0), (b_ref[...], wb_ref, wg1, bg1),
                                          (c_ref[...], wc_ref, wg2, bg2), (d_ref[...], wd_ref, wg3, bg3)):
        gate = _sigmoid(_dot(h, wg_ref[...]) + bg_ref[...])
        term = gate * _dot(branch, w_ref[...])
        acc = term if acc is None else acc + term
    o_ref[...] = acc.astype(BF16)


def _side_cast_specs(weights, layer, n_steps, step_of):
    in_specs, out_specs, shapes, steps_list = [], [], [], []
    for w in weights:
        _, rows, cols = w.shape
        slab, steps = _cast_plan(rows, n_steps)

        def slab_of(*g, steps=steps):
            return jnp.minimum(step_of(*g), steps - 1)

        in_specs.append(pl.BlockSpec((None, slab, cols), lambda *g, f=slab_of: (layer, f(*g), 0)))
        out_specs.append(pl.BlockSpec((None, slab, cols), lambda *g, f=slab_of: (0, f(*g), 0)))
        shapes.append(jax.ShapeDtypeStruct((1, rows, cols), BF16))
        steps_list.append(steps)
    return in_specs, out_specs, shapes, tuple(steps_list)


def _side_cast(src_refs, dst_refs, cast_steps, step):
    for src_ref, dst_ref, n_steps in zip(src_refs, dst_refs, cast_steps):
        @pl.when(step < n_steps)
        def _():
            dst_ref[...] = src_ref[...].astype(BF16)


def _cast_kernel(src_ref, dst_ref):
    dst_ref[...] = src_ref[...].astype(BF16)


def _cast_layer(w, l):
    _, rows, cols = w.shape
    slab, steps = _cast_plan(rows, max(1, rows * cols * 4 // CAST_SLAB_BYTES))
    return pl.pallas_call(
        _cast_kernel,
        grid=(steps,),
        in_specs=[pl.BlockSpec((None, slab, cols), lambda s: (l, s, 0))],
        out_specs=pl.BlockSpec((None, slab, cols), lambda s: (0, s, 0)),
        out_shape=jax.ShapeDtypeStruct((1, rows, cols), BF16),
        compiler_params=_cparams("parallel"),
        name="cast_weights",
    )(w)


CAST_SLAB_BYTES = 2 << 20


def _cast_plan(rows, n_steps):
    for steps in range(n_steps, 0, -1):
        slab = rows // steps
        if rows % steps == 0 and slab % (2 * SUBLANES) == 0:
            return slab, steps
    raise ValueError("no slab size for %d weight rows in %d steps" % (rows, n_steps))


def _merge(lay, l, h, attn_p, attn_s, branches, w_gate, b_gate, w_outs, cast_next=()):
    w_gate, lg = _layer_of(w_gate, l)
    m, d = h.shape
    tm = lay.row_tile(512)
    tn = 512
    nj = d // tn
    n_i = m // tm
    depth = b_gate.shape[0]
    p_tiles = lay.p_rows // tm
    s_tiles = lay.s_rows // tm

    def gate_w(n):
        return pl.BlockSpec((None, d, tn), lambda j, i: (lg, 0, n * nj + j))

    def gate_b(n):
        return pl.BlockSpec((None, 1, tn), lambda j, i: (l, 0, n * nj + j))

    cast_specs, cast_out_specs, cast_shapes, cast_steps = _side_cast_specs(
        cast_next, l + 1, nj * n_i, lambda j, i: j * n_i + i)

    b_gate3 = b_gate.reshape(depth, 1, N_BRANCH * d)
    out = pl.pallas_call(
        functools.partial(_merge_kernel, p_tiles=p_tiles, cast_steps=tuple(cast_steps)),
        grid=(nj, n_i),
        in_specs=[pl.BlockSpec((tm, d), lambda j, i: (i, 0)),
                  pl.BlockSpec((tm, A_WIDTH), lambda j, i: (jnp.minimum(i, p_tiles - 1), 0)),
                  pl.BlockSpec((tm, A_WIDTH), lambda j, i: (jnp.clip(i - p_tiles, 0, s_tiles - 1), 0))]
        + [pl.BlockSpec((tm, br.shape[1]), lambda j, i: (i, 0)) for br in branches]
        + [gate_w(n) for n in range(N_BRANCH)] + [gate_b(n) for n in range(N_BRANCH)]
        + [pl.BlockSpec((None, w.shape[1], tn), lambda j, i, lw=lw: (lw, 0, j)) for w, lw in w_outs]
        + cast_specs,
        out_specs=[pl.BlockSpec((tm, tn), lambda j, i: (i, j))] + cast_out_specs,
        out_shape=[jax.ShapeDtypeStruct((m, d), BF16)] + cast_shapes,
        compiler_params=_cparams("arbitrary", "arbitrary"),
        name="merge",
    )(h, attn_p, attn_s, *branches, *([w_gate] * N_BRANCH), *([b_gate3] * N_BRANCH),
      *(w for w, _ in w_outs), *cast_next)
    return out[0], out[1:]


def _rope_tables(lay):
    t = lay.dec_seq
    half = ROT_AXIS // 2
    rows = t // GRID_W
    row = jnp.repeat(jnp.arange(rows, dtype=F32), GRID_W)
    col = jnp.tile(jnp.arange(GRID_W, dtype=F32), rows)
    inv = ROPE_BASE ** (-jnp.arange(0, ROT_AXIS, 2, dtype=F32) / ROT_AXIS)
    ang = jnp.concatenate([row[:, None] * inv, row[:, None] * inv, col[:, None] * inv, col[:, None] * inv], axis=1)
    lane = jnp.arange(HEAD_DIM)
    first_half = (lane % ROT_AXIS) < half
    cos = jnp.cos(ang)
    sin = jnp.sin(ang)
    sin_a = jnp.where(first_half, -sin, 0.0)
    sin_b = jnp.where(first_half, 0.0, sin)

    def full(tab, fill):
        return jnp.concatenate([jnp.full((lay.p_rows, HEAD_DIM), fill, F32)] + [tab] * lay.dec_batch, axis=0)

    return full(cos, 1.0), full(sin_a, 0.0), full(sin_b, 0.0)


def kernel(x_prompt, x_sample, cache_k, cache_v, state_hgrn_fwd, state_hgrn_bwd, c, c_ctx, ada_w, ada_b, norm1_g, norm2_g, w_in, q_norm_g, k_norm_g, attn_sink, w_attn_out, conf_dw_w, conf_dw_b, conf_ln_g, conf_ln_b, w_conf_out, sc_conv_w, w_sc_out, hg_lb, hg_norm_g, w_hg_out, w_gate, b_gate, w_o, ffn_w1, ffn_w3, ffn_w2):
    batch, seq, d = x_prompt.shape
    dec_batch, dec_seq, _ = x_sample.shape
    depth = ada_w.shape[0]
    past = cache_k.shape[2]
    lay = _Layout(batch, seq, dec_batch, dec_seq)
    assert seq % SEQ_TILE == 0 and dec_seq % SEQ_TILE == 0 and dec_seq % GRID_W == 0
    assert 1 + dec_batch <= MOD_ROWS

    weights = dict(w_gate=w_gate, w_in=w_in, ffn_w1=ffn_w1, ffn_w3=ffn_w3, ffn_w2=ffn_w2, w_o=w_o,
                   w_attn_out=w_attn_out, w_conf_out=w_conf_out, w_sc_out=w_sc_out, w_hg_out=w_hg_out)
    by_fwd = ("w_gate", "w_attn_out", "w_conf_out", "w_sc_out", "w_hg_out")
    by_bwd = ("ffn_w1", "ffn_w3", "ffn_w2", "w_o")
    wl = {"w_in": (_cast_layer(w_in, 0), 0)}

    cvec = jnp.concatenate([c_ctx[None], c, jnp.zeros((MOD_ROWS - 1 - dec_batch, d), F32)], axis=0)
    mods = _modulation(cvec, ada_w, ada_b)
    mods = mods.reshape(depth, MOD_ROWS, 6, d)[:, :1 + dec_batch]
    mods = jnp.pad(mods, ((0, 0), (0, 0), (0, MOD_ROWS - 6), (0, 0)))

    lb = jax.nn.softmax(hg_lb.astype(F32), axis=1)
    lb = jnp.maximum(jnp.cumsum(lb, axis=1) - lb[:, :1], 0.0)

    rope_tabs = _rope_tables(lay)
    ctx_k = cache_k.reshape(dec_batch, depth, past, KV_WIDTH)
    ctx_v = cache_v.reshape(dec_batch, depth, past, KV_WIDTH)
    zero_states = jnp.zeros((batch, HG_HEADS, HG_DV, HG_DK), F32)

    x = (x_prompt.reshape(lay.p_rows, d), x_sample.reshape(lay.s_rows, d))
    ks_out, vs_out, sf_out, sb_out = [], [], [], []
    for l in range(depth):
        z, h = _in_projection(lay, l, x, mods, norm1_g, wl["w_in"])
        qn, kn, kf = _qk_prep(lay, l, z, rope_tabs, q_norm_g, k_norm_g)
        attn_p = _context_attention(lay, l, qn, kn, z, attn_sink[l])
        attn_s = _latent_attention(lay, l, qn, kn, z, ctx_k, ctx_v, attn_sink[l])
        u_b, u_c = _conv_mixers(lay, l, z, conf_dw_w, conf_dw_b, conf_ln_g, conf_ln_b, sc_conv_w)
        s0_f = jnp.concatenate([zero_states, jnp.swapaxes(state_hgrn_fwd[:, l], -1, -2)], axis=0)
        s0_b = jnp.concatenate([zero_states, jnp.swapaxes(state_hgrn_bwd[:, l], -1, -2)], axis=0)
        first = l == 0
        o_f, s_f, cast_f = _hgrn_direction(lay, l, z, lb[0, l][None], s0_f, reverse=False,
                                           cast=tuple(weights[n] for n in by_fwd) if first else ())
        o_d, s_b, cast_b = _hgrn_direction(lay, l, z, lb[1, l][None], s0_b, reverse=True, other=o_f,
                                           norm_g=hg_norm_g,
                                           cast=tuple(weights[n] for n in by_bwd) if first else ())
        if first:
            wl.update({n: (w, 0) for n, w in zip(by_fwd + by_bwd, cast_f + cast_b)})
        merged, cast = _merge(lay, l, h, attn_p, attn_s, (u_b, u_c, o_d), wl["w_gate"], b_gate,
                              (wl["w_attn_out"], wl["w_conf_out"], wl["w_sc_out"], wl["w_hg_out"]),
                              cast_next=tuple(weights.values()) if l + 1 < depth else ())
        x, h2 = _out_projection(lay, l, merged, wl["w_o"], x, mods, norm2_g)
        t = _ffn_up(lay, l, h2, wl["ffn_w1"], wl["ffn_w3"])
        if l + 1 < depth:
            x = _residual_matmul(lay, l, t, wl["ffn_w2"], x, mods, MOD_GATE2, "ffn_down", tm=1024, tn=512)
            wl = {name: (w, 0) for name, w in zip(weights, cast)}
        else:
            y_p, y_s = (_residual_matmul(lay, l, t, wl["ffn_w2"], x, mods, MOD_GATE2, "ffn_down", tm=1024, tn=512,
                                         rows=r) for r in ((0, lay.p_rows), (lay.p_rows, lay.rows)))
        ks_out.append(kf[:lay.p_rows].reshape(batch, seq, N_KV_A, HEAD_DIM))
        vs_out.append(z[:lay.p_rows, A_WIDTH + KV_WIDTH:A_WIDTH + 2 * KV_WIDTH].reshape(batch, seq, N_KV_A, HEAD_DIM))
        sf_out.append(jnp.swapaxes(s_f[:batch], -1, -2))
        sb_out.append(jnp.swapaxes(s_b[:batch], -1, -2))
    return (y_p.reshape(batch, seq, d), y_s.reshape(dec_batch, dec_seq, d),
            jnp.stack(ks_out, axis=1), jnp.stack(vs_out, axis=1),
            jnp.stack(sf_out, axis=1), jnp.stack(sb_out, axis=1))
```

```python
import functools

import jax
import jax.numpy as jnp
from jax import lax
from jax.experimental import pallas as pl
from jax.experimental.pallas import tpu as pltpu

F32 = jnp.float32
BF16 = jnp.bfloat16

V7X_VMEM_BYTES = 64 * 1024 * 1024
VMEM_LIMIT_BYTES = V7X_VMEM_BYTES - 8 * 1024 * 1024
LANES = 128
SUBLANES = 8
MXU_WIDTH = 256

GRID_W = 64
HEAD_DIM = 128
N_HEADS_A = 8
N_KV_A = 2
GROUP_A = N_HEADS_A // N_KV_A
A_WIDTH = N_HEADS_A * HEAD_DIM
KV_WIDTH = N_KV_A * HEAD_DIM
WINDOW = 128
BLOCK = 128
ROPE_BASE = 10000.0
ROT_AXIS = HEAD_DIM // 2
ATTN_SCALE = HEAD_DIM ** -0.5
NEG_MASK = -1e9
CONF_W = 512
CONF_K = 31
SC_W = 512
SC_K = 3
HG_HEADS = 4
HG_DK = 128
HG_DV = 128
HG_W = HG_HEADS * HG_DK
HG_CHUNK = 32
N_BRANCH = 4
EPS = 1e-6

COL_W = 512
COL_CONF_A, COL_CONF_G, COL_SC_B, COL_SC_C, COL_SC_H = 3, 4, 5, 6, 7
COL_HQ, COL_HF_FWD, COL_HF_BWD, COL_HI, COL_HG = 8, 9, 10, 11, 12
KV_COL_K, KV_COL_V = A_WIDTH // KV_WIDTH, A_WIDTH // KV_WIDTH + 1

ATTN_Q_ROWS = 512
SEQ_TILE = 256
HALO = 16
HG_SAFE_LOG_DECAY = -60.0

MOD_SHIFT1, MOD_SCALE1, MOD_GATE1, MOD_SHIFT2, MOD_SCALE2, MOD_GATE2 = range(6)
MOD_ROWS = 8


def _cparams(*semantics):
    return pltpu.CompilerParams(dimension_semantics=semantics, vmem_limit_bytes=VMEM_LIMIT_BYTES)


def _sigmoid(x):
    return 1.0 / (1.0 + jnp.exp(-x))


def _silu(x):
    return x * _sigmoid(x)


def _chunk_of(idx, size):
    assert size & (size - 1) == 0
    return jnp.right_shift(idx, size.bit_length() - 1)


def _dot(a, b):
    return jnp.dot(a, b, preferred_element_type=F32)


def _dot_nt(a, b):
    return lax.dot_general(a, b, (((1,), (1,)), ((), ())), preferred_element_type=F32)


class _Layout:
    def __init__(self, batch, seq, dec_batch, dec_seq):
        self.batch, self.seq, self.dec_batch, self.dec_seq = batch, seq, dec_batch, dec_seq
        self.p_rows = batch * seq
        self.s_rows = dec_batch * dec_seq
        self.rows = self.p_rows + self.s_rows
        self.n_seq = batch + dec_batch

    def row_tile(self, preferred):
        t = preferred
        while self.p_rows % t or self.dec_seq % t:
            t //= 2
        return t

    def mod_row(self, i, tile):
        p_tiles = self.p_rows // tile
        per = self.dec_seq // tile
        return jnp.where(i < p_tiles, 0, 1 + (i - p_tiles) // per)

    def seq_pos(self, n, tile):
        p_tiles = self.p_rows // tile
        tpp = self.seq // tile
        tps = self.dec_seq // tile
        is_p = n < p_tiles
        seq_idx = jnp.where(is_p, n // tpp, self.batch + (n - p_tiles) // tps)
        pos = jnp.where(is_p, n % tpp, (n - p_tiles) % tps)
        count = jnp.where(is_p, tpp, tps)
        return seq_idx, pos, count


def _mod_kernel(c_ref, w_ref, b_ref, o_ref):
    s = _silu(c_ref[...]).astype(BF16)
    o_ref[...] = _dot(s, w_ref[...].astype(BF16)) + b_ref[...]


def _modulation(cvec, ada_w, ada_b):
    depth, d, n = ada_w.shape
    tn = 1024
    return pl.pallas_call(
        _mod_kernel,
        grid=(depth, n // tn),
        in_specs=[
            pl.BlockSpec((MOD_ROWS, d), lambda l, j: (0, 0)),
            pl.BlockSpec((None, d, tn), lambda l, j: (l, 0, j)),
            pl.BlockSpec((None, 1, tn), lambda l, j: (l, 0, j)),
        ],
        out_specs=pl.BlockSpec((None, MOD_ROWS, tn), lambda l, j: (l, 0, j)),
        out_shape=jax.ShapeDtypeStruct((depth, MOD_ROWS, n), F32),
        compiler_params=_cparams("parallel", "parallel"),
        name="modulation",
    )(cvec, ada_w, ada_b.reshape(depth, 1, n))


NORM_ROWS = 16


def _modulated_norm(x_ref, g_ref, mod_ref, shift_row, scale_row, h_ref):
    gain = g_ref[...] * (1.0 + mod_ref[scale_row:scale_row + 1, :])
    shift = mod_ref[shift_row:shift_row + 1, :]
    block = 8 * NORM_ROWS

    def body(c, carry):
        start = pl.multiple_of(c * block, block)
        for r0 in range(0, block, NORM_ROWS):
            rows = pl.ds(start + r0, NORM_ROWS)
            x = x_ref[rows, :]
            y = x * lax.rsqrt(jnp.mean(x * x, axis=-1, keepdims=True) + EPS)
            h_ref[rows, :] = (y * gain + shift).astype(BF16)
        return carry

    lax.fori_loop(0, x_ref.shape[0] // block, body, 0)


def _split_rows_specs(lay, tm, width, tile_of):
    p_tiles = lay.p_rows // tm
    s_tiles = lay.s_rows // tm
    return [pl.BlockSpec((tm, width), lambda *g: (jnp.minimum(tile_of(*g), p_tiles - 1), 0)),
            pl.BlockSpec((tm, width), lambda *g: (jnp.clip(tile_of(*g) - p_tiles, 0, s_tiles - 1), 0))]


def _inproj_kernel(*refs, p_tiles):
    x_refs, (mod_ref, g_ref, w_ref, z_ref, h_ref) = refs[:-5], refs[-5:]
    i = pl.program_id(0)
    first_step = pl.program_id(1) == 0
    for x_ref, mine in zip(x_refs, (True,) if p_tiles is None else (i < p_tiles, i >= p_tiles)):
        @pl.when(first_step & mine)
        def _():
            _modulated_norm(x_ref, g_ref, mod_ref, MOD_SHIFT1, MOD_SCALE1, h_ref)

    z_ref[...] = _dot(h_ref[...], w_ref[...])


def _layer_of(w, l):
    return w if isinstance(w, tuple) else (w, l)


def _in_projection(lay, l, x, mods, norm_g, w_in):
    w_in, lw = _layer_of(w_in, l)
    pair = isinstance(x, tuple)
    m, d = lay.rows, w_in.shape[1]
    n = w_in.shape[-1]
    tm = lay.row_tile(1024)
    tn = 512
    x_specs = (_split_rows_specs(lay, tm, d, lambda i, j: i) if pair
               else [pl.BlockSpec((tm, d), lambda i, j: (i, 0))])
    return pl.pallas_call(
        functools.partial(_inproj_kernel, p_tiles=lay.p_rows // tm if pair else None),
        grid=(m // tm, n // tn),
        in_specs=x_specs + [
            pl.BlockSpec((None, None, MOD_ROWS, d), lambda i, j: (l, lay.mod_row(i, tm), 0, 0)),
            pl.BlockSpec((None, 1, d), lambda i, j: (l, 0, 0)),
            pl.BlockSpec((None, d, tn), lambda i, j: (lw, 0, j)),
        ],
        out_specs=[
            pl.BlockSpec((tm, tn), lambda i, j: (i, j)),
            pl.BlockSpec((tm, d), lambda i, j: (i, 0)),
        ],
        out_shape=[jax.ShapeDtypeStruct((m, n), F32), jax.ShapeDtypeStruct((m, d), BF16)],
        compiler_params=_cparams("parallel", "arbitrary"),
        name="in_projection",
    )(*(x if pair else (x,)), mods, norm_g.reshape(norm_g.shape[0], 1, d), w_in)


def _residual_matmul_kernel(a_ref, w_ref, x_ref, mod_ref, o_ref, *, gate_row):
    a = a_ref[...]
    chunk = 512
    for c0 in range(0, o_ref.shape[1], chunk):
        sl = slice(c0, c0 + chunk)
        o_ref[:, sl] = x_ref[:, sl] + mod_ref[gate_row:gate_row + 1, sl] * _dot(a, w_ref[:, sl])


def _residual_matmul(lay, l, a, w, x, mods, gate_row, name, tm, tn, rows=None):
    w, lw = _layer_of(w, l)
    m, k = a.shape
    d = w.shape[-1]
    tm = lay.row_tile(tm)
    first, stop = (0, m) if rows is None else rows
    assert first % tm == 0 and stop % tm == 0
    t0 = first // tm
    return pl.pallas_call(
        functools.partial(_residual_matmul_kernel, gate_row=gate_row),
        grid=((stop - first) // tm, d // tn),
        in_specs=[
            pl.BlockSpec((tm, k), lambda i, j: (t0 + i, 0)),
            pl.BlockSpec((None, k, tn), lambda i, j: (lw, 0, j)),
            pl.BlockSpec((tm, tn), lambda i, j: (t0 + i, j)),
            pl.BlockSpec((None, None, MOD_ROWS, tn), lambda i, j: (l, lay.mod_row(t0 + i, tm), 0, j)),
        ],
        out_specs=pl.BlockSpec((tm, tn), lambda i, j: (i, j)),
        out_shape=jax.ShapeDtypeStruct((stop - first, d), F32),
        compiler_params=_cparams("parallel", "arbitrary"),
        name=name,
    )(a, w, x, mods)


def _out_proj_kernel(a_ref, w_ref, *refs, p_tiles):
    x_refs, (mod_ref, g_ref, o_ref, h_ref) = refs[:-4], refs[-4:]
    i = pl.program_id(0)
    a = a_ref[...]
    chunk = 512
    for c0 in range(0, o_ref.shape[1], chunk):
        sl = slice(c0, c0 + chunk)
        update = mod_ref[MOD_GATE1:MOD_GATE1 + 1, sl] * _dot(a, w_ref[:, sl])
        if p_tiles is None:
            o_ref[:, sl] = x_refs[0][:, sl] + update
        else:
            o_ref[:, sl] = jnp.where(i < p_tiles, x_refs[0][:, sl], x_refs[1][:, sl]) + update
    _modulated_norm(o_ref, g_ref, mod_ref, MOD_SHIFT2, MOD_SCALE2, h_ref)


def _out_projection(lay, l, a, w_o, x, mods, norm2_g):
    w_o, lw = _layer_of(w_o, l)
    pair = isinstance(x, tuple)
    m, k = a.shape
    d = w_o.shape[-1]
    tm = lay.row_tile(512)
    x_specs = _split_rows_specs(lay, tm, d, lambda i: i) if pair else [pl.BlockSpec((tm, d), lambda i: (i, 0))]
    return pl.pallas_call(
        functools.partial(_out_proj_kernel, p_tiles=lay.p_rows // tm if pair else None),
        grid=(m // tm,),
        in_specs=[
            pl.BlockSpec((tm, k), lambda i: (i, 0)),
            pl.BlockSpec((None, k, d), lambda i: (lw, 0, 0)),
        ] + x_specs + [
            pl.BlockSpec((None, None, MOD_ROWS, d), lambda i: (l, lay.mod_row(i, tm), 0, 0)),
            pl.BlockSpec((None, 1, d), lambda i: (l, 0, 0)),
        ],
        out_specs=[pl.BlockSpec((tm, d), lambda i: (i, 0)), pl.BlockSpec((tm, d), lambda i: (i, 0))],
        out_shape=[jax.ShapeDtypeStruct((m, d), F32), jax.ShapeDtypeStruct((m, d), BF16)],
        compiler_params=_cparams("parallel"),
        name="out_projection",
    )(a, w_o, *(x if pair else (x,)), mods, norm2_g.reshape(norm2_g.shape[0], 1, d))


def _ffn_up_kernel(h_ref, w1_ref, w3_ref, o_ref):
    h = h_ref[...]
    tn = o_ref.shape[1]
    tail = tn % MXU_WIDTH
    c0 = 0
    while c0 < tn - tail:
        c1 = min(c0 + 512, tn - tail)
        o_ref[:, c0:c1] = (_silu(_dot(h, w1_ref[:, c0:c1])) * _dot(h, w3_ref[:, c0:c1])).astype(BF16)
        c0 = c1
    if tail:
        both = _dot(h, jnp.concatenate([w1_ref[:, c0:], w3_ref[:, c0:]], axis=1))
        o_ref[:, c0:] = (_silu(both[:, :tail]) * both[:, tail:]).astype(BF16)


def _ffn_up(lay, l, h, w1, w3):
    (w1, l1), (w3, l3) = _layer_of(w1, l), _layer_of(w3, l)
    m, d = h.shape
    n = w1.shape[-1]
    tm = lay.row_tile(1024)
    tn = n // 4
    assert n % 4 == 0 and tn % LANES == 0
    return pl.pallas_call(
        _ffn_up_kernel,
        grid=(m // tm, n // tn),
        in_specs=[
            pl.BlockSpec((tm, d), lambda i, j: (i, 0)),
            pl.BlockSpec((None, d, tn), lambda i, j: (l1, 0, j)),
            pl.BlockSpec((None, d, tn), lambda i, j: (l3, 0, j)),
        ],
        out_specs=pl.BlockSpec((tm, tn), lambda i, j: (i, j)),
        out_shape=jax.ShapeDtypeStruct((m, n), BF16),
        compiler_params=_cparams("parallel", "parallel"),
        name="ffn_up",
    )(h, w1, w3)


def _qk_prep_kernel(q_ref, k_ref, v_ref, cos_ref, sa_ref, sb_ref, qg_ref, kg_ref, qo_ref, ko_ref, kf_ref, vf_ref,
                    *, p_tiles):
    is_context = pl.program_id(0) < p_tiles
    cos, sin_a, sin_b = cos_ref[...], sa_ref[...], sb_ref[...]

    @pl.when(is_context)
    def _():
        vf_ref[...] = v_ref[...]

    ones = jnp.ones((HEAD_DIM, HEAD_DIM), BF16)

    def norm(x, g):
        mean_sq = _dot((x * x).astype(BF16), ones) * (1.0 / HEAD_DIM)
        return x * lax.rsqrt(mean_sq + EPS) * g

    def rope(x):
        return (x * cos + pltpu.roll(x, HEAD_DIM - ROT_AXIS // 2, 1) * sin_a
                + pltpu.roll(x, ROT_AXIS // 2, 1) * sin_b)

    for h in range(N_HEADS_A):
        sl = slice(h * HEAD_DIM, (h + 1) * HEAD_DIM)
        qo_ref[:, sl] = (rope(norm(q_ref[:, sl], qg_ref[...])) * ATTN_SCALE).astype(BF16)
    for h in range(N_KV_A):
        sl = slice(h * HEAD_DIM, (h + 1) * HEAD_DIM)
        kn = norm(k_ref[:, sl], kg_ref[...])
        ko_ref[:, sl] = rope(kn).astype(BF16)

        @pl.when(is_context)
        def _():
            kf_ref[:, sl] = kn


def _qk_prep(lay, l, z, rope_tabs, q_norm_g, k_norm_g):
    m = z.shape[0]
    tm = lay.row_tile(512)
    cos, sin_a, sin_b = rope_tabs
    tab_spec = pl.BlockSpec((tm, HEAD_DIM), lambda i: (i, 0))
    g_spec = pl.BlockSpec((None, 1, HEAD_DIM), lambda i: (l, 0, 0))
    p_tiles = lay.p_rows // tm
    cache_spec = pl.BlockSpec((tm, KV_WIDTH), lambda i: (jnp.minimum(i, p_tiles - 1), 0))
    return pl.pallas_call(
        functools.partial(_qk_prep_kernel, p_tiles=p_tiles),
        grid=(m // tm,),
        in_specs=[
            pl.BlockSpec((tm, A_WIDTH), lambda i: (i, 0)),
            pl.BlockSpec((tm, KV_WIDTH), lambda i: (i, KV_COL_K)),
            pl.BlockSpec((tm, KV_WIDTH), lambda i: (jnp.minimum(i, p_tiles - 1), KV_COL_V)),
            tab_spec, tab_spec, tab_spec, g_spec, g_spec,
        ],
        out_specs=[
            pl.BlockSpec((tm, A_WIDTH), lambda i: (i, 0)),
            pl.BlockSpec((tm, KV_WIDTH), lambda i: (i, 0)),
            cache_spec, cache_spec,
        ],
        out_shape=[
            jax.ShapeDtypeStruct((m, A_WIDTH), BF16),
            jax.ShapeDtypeStruct((m, KV_WIDTH), BF16),
            jax.ShapeDtypeStruct((lay.p_rows, KV_WIDTH), F32),
            jax.ShapeDtypeStruct((lay.p_rows, KV_WIDTH), F32),
        ],
        compiler_params=_cparams("arbitrary"),
        name="qk_prep",
    )(z, z, z, cos, sin_a, sin_b, q_norm_g.reshape(-1, 1, HEAD_DIM), k_norm_g.reshape(-1, 1, HEAD_DIM))


def _stack_heads(q_ref, kv):
    return jnp.concatenate(
        [q_ref[:, (kv * GROUP_A + g) * HEAD_DIM:(kv * GROUP_A + g + 1) * HEAD_DIM] for g in range(GROUP_A)], axis=0)


def _fold_lane_tiles(x, op):
    out = x[:, :LANES]
    for c0 in range(LANES, x.shape[1], LANES):
        out = op(out, x[:, c0:c0 + LANES])
    return out


def _sink_column(sink_ref, kv, rows):
    return jnp.concatenate(
        [jnp.full((rows, 1), sink_ref[kv * GROUP_A + g], F32) for g in range(GROUP_A)], axis=0)


def _context_attention_kernel(sink_ref, q_ref, k_ref, v_ref, o_ref, *, seq):
    for r0 in range(0, q_ref.shape[0], seq):
        rows = slice(r0, r0 + seq)
        for kv in range(N_KV_A):
            sl = slice(kv * HEAD_DIM, (kv + 1) * HEAD_DIM)
            q = jnp.concatenate(
                [q_ref[rows, (kv * GROUP_A + g) * HEAD_DIM:(kv * GROUP_A + g + 1) * HEAD_DIM]
                 for g in range(GROUP_A)], axis=0)
            s = _dot_nt(q, k_ref[rows, sl])
            sink = _sink_column(sink_ref, kv, seq)
            mx = jnp.maximum(jnp.max(_fold_lane_tiles(s, jnp.maximum), axis=-1, keepdims=True), sink)
            p = jnp.exp(s - mx)
            den = jnp.sum(_fold_lane_tiles(p, jnp.add), axis=-1, keepdims=True) + jnp.exp(sink - mx)
            o = _dot(p.astype(BF16), v_ref[rows, sl].astype(BF16)) * (1.0 / den)
            for g in range(GROUP_A):
                h = kv * GROUP_A + g
                o_ref[rows, h * HEAD_DIM:(h + 1) * HEAD_DIM] = o[g * seq:(g + 1) * seq].astype(BF16)


def _context_attention(lay, l, qn, kn, z, sink):
    per_step = next(n for n in (4, 2, 1) if lay.batch % n == 0)
    seq = lay.seq
    rows = per_step * seq
    return pl.pallas_call(
        functools.partial(_context_attention_kernel, seq=seq),
        grid=(lay.batch // per_step,),
        in_specs=[
            pl.BlockSpec(memory_space=pltpu.SMEM),
            pl.BlockSpec((rows, A_WIDTH), lambda b: (b, 0)),
            pl.BlockSpec((rows, KV_WIDTH), lambda b: (b, 0)),
            pl.BlockSpec((rows, KV_WIDTH), lambda b: (b, KV_COL_V)),
        ],
        out_specs=pl.BlockSpec((rows, A_WIDTH), lambda b: (b, 0)),
        out_shape=jax.ShapeDtypeStruct((lay.p_rows, A_WIDTH), BF16),
        compiler_params=_cparams("parallel"),
        name="context_attention",
    )(sink, qn, kn, z)


def _latent_attention_kernel(sink_ref, q_ref, kp_ref, kc_ref, kn_ref, vp_ref, vc_ref, vn_ref, ck_ref, cv_ref,
                             o_ref, *, seq_len):
    assert WINDOW <= BLOCK
    q_rows = q_ref.shape[0]
    j = pl.program_id(1)
    n_tiles = seq_len // q_rows
    r = lax.broadcasted_iota(jnp.int32, (q_rows, BLOCK), 0)
    c = lax.broadcasted_iota(jnp.int32, (q_rows, BLOCK), 1)
    valid_prev = jnp.concatenate([(r - (c - BLOCK) <= WINDOW) & (j >= 1)] * GROUP_A, axis=0)
    valid_next = jnp.concatenate([((c + q_rows) - r <= WINDOW) & (j <= n_tiles - 2)] * GROUP_A, axis=0)
    valid_own = None
    if q_rows - 1 > WINDOW:
        ro = lax.broadcasted_iota(jnp.int32, (q_rows, q_rows), 0)
        co = lax.broadcasted_iota(jnp.int32, (q_rows, q_rows), 1)
        valid_own = jnp.concatenate([jnp.abs(co - ro) <= WINDOW] * GROUP_A, axis=0)
    for kv in range(N_KV_A):
        sl = slice(kv * HEAD_DIM, (kv + 1) * HEAD_DIM)
        q = _stack_heads(q_ref, kv)
        s_p = jnp.where(valid_prev, _dot_nt(q, kp_ref[:, sl]), NEG_MASK)
        s_o = _dot_nt(q, kc_ref[:, sl])
        if valid_own is not None:
            s_o = jnp.where(valid_own, s_o, NEG_MASK)
        s_n = jnp.where(valid_next, _dot_nt(q, kn_ref[:, sl]), NEG_MASK)
        s_c = _dot_nt(q, ck_ref[:, sl].astype(BF16))
        sink = _sink_column(sink_ref, kv, q_rows)
        mx = jnp.maximum(jnp.maximum(s_p, _fold_lane_tiles(s_o, jnp.maximum)),
                         jnp.maximum(s_n, _fold_lane_tiles(s_c, jnp.maximum)))
        mx = jnp.maximum(jnp.max(mx, axis=-1, keepdims=True), sink)
        psum = None
        o = None
        for s, v_ref in ((s_p, vp_ref), (s_o, vc_ref), (s_n, vn_ref), (s_c, cv_ref)):
            p = jnp.exp(s - mx)
            pf = _fold_lane_tiles(p, jnp.add)
            psum = pf if psum is None else psum + pf
            term = _dot(p.astype(BF16), v_ref[:, sl].astype(BF16))
            o = term if o is None else o + term
        den = jnp.sum(psum, axis=-1, keepdims=True) + jnp.exp(sink - mx)
        o = o * (1.0 / den)
        for g in range(GROUP_A):
            h = kv * GROUP_A + g
            o_ref[:, h * HEAD_DIM:(h + 1) * HEAD_DIM] = o[g * q_rows:(g + 1) * q_rows].astype(BF16)


def _latent_attention(lay, l, qn, kn, z, ctx_k, ctx_v, sink):
    q_rows = ATTN_Q_ROWS
    per_tile = q_rows // BLOCK
    assert lay.p_rows % q_rows == 0 and lay.dec_seq % q_rows == 0
    nb = lay.dec_seq // BLOCK
    nt = lay.dec_seq // q_rows
    base = lay.p_rows // BLOCK
    past = ctx_k.shape[2]

    def own(col):
        return lambda b, j: (lay.p_rows // q_rows + b * nt + j, col)

    def prev(col):
        return lambda b, j: (base + b * nb + jnp.maximum(j * per_tile - 1, 0), col)

    def nxt(col):
        return lambda b, j: (base + b * nb + jnp.minimum((j + 1) * per_tile, nb - 1), col)

    def trio(col):
        return [pl.BlockSpec((BLOCK, KV_WIDTH), prev(col)), pl.BlockSpec((q_rows, KV_WIDTH), own(col)),
                pl.BlockSpec((BLOCK, KV_WIDTH), nxt(col))]

    k_spec = trio(0)
    v_spec = trio(KV_COL_V)
    ctx_spec = pl.BlockSpec((None, None, past, KV_WIDTH), lambda b, j: (b, l, 0, 0))
    return pl.pallas_call(
        functools.partial(_latent_attention_kernel, seq_len=lay.dec_seq),
        grid=(lay.dec_batch, nt),
        in_specs=[pl.BlockSpec(memory_space=pltpu.SMEM), pl.BlockSpec((q_rows, A_WIDTH), own(0))]
        + k_spec + v_spec + [ctx_spec, ctx_spec],
        out_specs=pl.BlockSpec((q_rows, A_WIDTH), lambda b, j: (b * nt + j, 0)),
        out_shape=jax.ShapeDtypeStruct((lay.s_rows, A_WIDTH), BF16),
        compiler_params=_cparams("parallel", "parallel"),
        name="latent_attention",
    )(sink, qn, kn, kn, kn, z, z, z, ctx_k, ctx_v)


def _conv_kernel(a_c, a_p, a_n, g_c, g_p, g_n, b_c, c_c, c_p, c_n, h_c, h_p, h_n,
                 dw_ref, db_ref, lg_ref, lb_ref, sw_ref, ub_ref, uc_ref, xb_s, xc_s, *, lay):
    _, pos, count = lay.seq_pos(pl.program_id(0), SEQ_TILE)
    has_prev = pos > 0
    has_next = pos < count - 1
    rb = 32
    xb_s[0, 0:HALO, :] = jnp.where(has_prev, a_p[...] * _sigmoid(g_p[...]), 0.0)
    xb_s[0, HALO + SEQ_TILE:, :] = jnp.where(has_next, a_n[...] * _sigmoid(g_n[...]), 0.0)
    xc_s[0:HALO, :] = jnp.where(has_prev, c_p[...] * h_p[...], 0.0)
    xc_s[HALO + SEQ_TILE:, :] = jnp.where(has_next, c_n[...] * h_n[...], 0.0)
    for r0 in range(0, SEQ_TILE, rb):
        rows = slice(r0, r0 + rb)
        xb_s[0, HALO + r0:HALO + r0 + rb, :] = a_c[rows, :] * _sigmoid(g_c[rows, :])
        xc_s[HALO + r0:HALO + r0 + rb, :] = c_c[rows, :] * h_c[rows, :]
    shifted_rows = SEQ_TILE + 2 * HALO - SUBLANES
    chunk = 40
    for s in range(1, SUBLANES):
        for r0 in range(0, shifted_rows, chunk):
            xb_s[s, r0:r0 + chunk, :] = xb_s[0, r0 + s:r0 + s + chunk, :]
    for r0 in range(0, SEQ_TILE, rb):
        acc = jnp.zeros((rb // SUBLANES, SUBLANES, CONF_W), F32)
        for k in range(CONF_K):
            off = HALO - CONF_K // 2 + k
            row = r0 + off - off % SUBLANES
            x = xb_s[off % SUBLANES, row:row + rb, :]
            acc = acc + dw_ref[k] * x.reshape(rb // SUBLANES, SUBLANES, CONF_W)
        u = acc.reshape(rb, CONF_W) + db_ref[...]
        uc = u - jnp.mean(u, axis=-1, keepdims=True)
        var = jnp.mean(uc * uc, axis=-1, keepdims=True)
        u = uc * lax.rsqrt(var + EPS) * lg_ref[...] + lb_ref[...]
        ub_ref[r0:r0 + rb, :] = _silu(u).astype(BF16)
        acc = jnp.zeros((rb // SUBLANES, SUBLANES, SC_W), F32)
        for k in range(SC_K):
            off = HALO - SC_K // 2 + r0 + k
            acc = acc + sw_ref[k] * xc_s[off:off + rb, :].reshape(rb // SUBLANES, SUBLANES, SC_W)
        uc_ref[r0:r0 + rb, :] = (b_c[r0:r0 + rb, :] * acc.reshape(rb, SC_W)).astype(BF16)


def _conv_mixers(lay, l, z, conf_dw_w, conf_dw_b, conf_ln_g, conf_ln_b, sc_conv_w):
    m = z.shape[0]
    n_tiles = m // SEQ_TILE
    per = SEQ_TILE // HALO
    n_halo = m // HALO

    def cur(col):
        return pl.BlockSpec((SEQ_TILE, COL_W), lambda i: (i, col))

    def prev(col):
        return pl.BlockSpec((HALO, COL_W), lambda i: (jnp.maximum(i * per - 1, 0), col))

    def nxt(col):
        return pl.BlockSpec((HALO, COL_W), lambda i: (jnp.minimum((i + 1) * per, n_halo - 1), col))

    def trio(col):
        return [cur(col), prev(col), nxt(col)]

    def par(rows):
        return pl.BlockSpec((None, rows, COL_W), lambda i: (l, 0, 0))

    def taps(n):
        return pl.BlockSpec((None, n, SUBLANES, COL_W), lambda i: (l, 0, 0, 0))

    depth = conf_dw_w.shape[0]
    dw = jnp.broadcast_to(conf_dw_w[:, :, None, :], (depth, CONF_K, SUBLANES, CONF_W))
    sw = jnp.broadcast_to(sc_conv_w[:, :, None, :], (depth, SC_K, SUBLANES, SC_W))
    return pl.pallas_call(
        functools.partial(_conv_kernel, lay=lay),
        grid=(n_tiles,),
        in_specs=trio(COL_CONF_A) + trio(COL_CONF_G) + [cur(COL_SC_B)] + trio(COL_SC_C) + trio(COL_SC_H)
        + [taps(CONF_K), par(1), par(1), par(1), taps(SC_K)],
        out_specs=[pl.BlockSpec((SEQ_TILE, CONF_W), lambda i: (i, 0)),
                   pl.BlockSpec((SEQ_TILE, SC_W), lambda i: (i, 0))],
        out_shape=[jax.ShapeDtypeStruct((m, CONF_W), BF16), jax.ShapeDtypeStruct((m, SC_W), BF16)],
        scratch_shapes=[pltpu.VMEM((SUBLANES, SEQ_TILE + 2 * HALO, CONF_W), F32),
                        pltpu.VMEM((SEQ_TILE + 2 * HALO, SC_W), F32)],
        compiler_params=_cparams("parallel"),
        name="conv_mixers",
    )(*([z] * 13), dw, conf_dw_b.reshape(depth, 1, CONF_W), conf_ln_g.reshape(depth, 1, CONF_W),
      conf_ln_b.reshape(depth, 1, CONF_W), sw)


def _hgrn_gates(hq_ref, hf_ref, lb_ref, sl, tri_b3):
    zq = hq_ref[:, sl]
    q = _silu(zq) * (HG_DK ** -0.5)
    zf = hf_ref[:, sl]
    lb = lb_ref[:, sl]
    e = jnp.exp(-jnp.abs(zf))
    r = 1.0 / (1.0 + e)
    nonneg = zf >= 0
    sig_pos = jnp.where(nonneg, r, e * r)
    sig_neg = jnp.where(nonneg, e * r, r)
    logf = jnp.log(lb + (1.0 - lb) * sig_pos)
    kk = (1.0 - lb) * sig_neg
    g1 = logf.astype(BF16)
    r1 = logf - g1.astype(F32)
    g2 = r1.astype(BF16)
    g3 = (r1 - g2.astype(F32)).astype(BF16)
    b = _dot(tri_b3, jnp.concatenate([g1, g2, g3], axis=0))
    return q, kk, b


def _block_place(x, n_chunks):
    w = x.shape[1]
    zeros = jnp.zeros((HG_CHUNK, w), x.dtype)
    row_blocks = []
    for c in range(n_chunks):
        cols = [zeros] * n_chunks
        cols[c] = x[c * HG_CHUNK:(c + 1) * HG_CHUNK, :]
        row_blocks.append(jnp.concatenate(cols, axis=1))
    return jnp.concatenate(row_blocks, axis=0)


def _hgrn_exact_scores(a_ref, q_s, k_s, b_s, reverse):
    ts = a_ref.shape[0]
    a_ref[...] = jnp.zeros_like(a_ref)

    def body(s, carry):
        c0 = pl.multiple_of((s // HG_CHUNK) * HG_CHUNK, HG_CHUNK)
        w = jnp.exp(jnp.minimum(b_s[pl.ds(c0, HG_CHUNK), :] - b_s[pl.ds(s, 1), :], 0.0))
        prod = q_s[pl.ds(c0, HG_CHUNK), :] * k_s[pl.ds(s, 1), :] * w
        trow = c0 + lax.broadcasted_iota(jnp.int32, (HG_CHUNK, 1), 0)
        ok = (trow <= s) if reverse else (trow >= s)
        onehot = (lax.broadcasted_iota(jnp.int32, (1, ts), 1) == s).astype(F32)
        colv = jnp.sum(prod, axis=-1, keepdims=True)
        a_ref[pl.ds(c0, HG_CHUNK), :] += jnp.where(ok, colv, 0.0) * onehot
        return carry

    lax.fori_loop(0, ts, body, 0)


def _hgrn_kernel(hq_ref, hf_ref, hi_ref, lb_ref, s0_ref, *refs, lay, reverse, finish, cast_steps):
    ts = SEQ_TILE
    n_chunks = ts // HG_CHUNK
    _, pos, count = lay.seq_pos(pl.program_id(0), ts)
    other_ref, gate_ref, gain_ref = refs[:3] if finish else (None, None, None)
    refs = refs[3:] if finish else refs
    n_cast = len(cast_steps)
    src_refs, (o_ref, sf_ref), dst_refs = refs[:n_cast], refs[n_cast:n_cast + 2], refs[n_cast + 2:2 * n_cast + 2]
    st_ref, oi_s, a_ref, q_s, k_s, b_s = refs[2 * n_cast + 2:]
    _side_cast(src_refs, dst_refs, cast_steps, pl.program_id(0))

    def emit(o, sl):
        if finish:
            o = o + other_ref[:, sl]
            o = o * lax.rsqrt(jnp.mean(o * o, axis=-1, keepdims=True) + EPS) * gain_ref[...]
            o_ref[:, sl] = (o * _silu(gate_ref[:, sl])).astype(BF16)
        else:
            o_ref[:, sl] = o

    @pl.when(pos == 0)
    def _():
        st_ref[...] = s0_ref[...]

    row = lax.broadcasted_iota(jnp.int32, (ts, ts), 0)
    col = lax.broadcasted_iota(jnp.int32, (ts, ts), 1)
    same = _chunk_of(row, HG_CHUNK) == _chunk_of(col, HG_CHUNK)
    tri = same & ((col >= row) if reverse else (col <= row))
    tri_b = tri.astype(F32).astype(BF16)
    tri_b3 = jnp.concatenate([tri_b] * 3, axis=1)
    order = range(n_chunks - 1, -1, -1) if reverse else range(n_chunks)
    last = 0 if reverse else HG_CHUNK - 1

    b_min = None
    for hd in range(HG_HEADS):
        sl = slice(hd * HG_DK, (hd + 1) * HG_DK)
        q, kk, b = _hgrn_gates(hq_ref, hf_ref, lb_ref, sl, tri_b3)
        head_min = jnp.min(b)
        b_min = head_min if b_min is None else jnp.minimum(b_min, head_min)
        b3 = b.reshape(n_chunks, HG_CHUNK, HG_DK)
        total3 = b3[:, last:last + 1, :]
        qt = (q * jnp.exp(b)).astype(BF16)
        kend = (kk * jnp.exp(jnp.broadcast_to(total3, b3.shape).reshape(ts, HG_DK) - b)).astype(BF16)
        decay = jnp.exp(total3.reshape(n_chunks, HG_DK))
        v_h = hi_ref[:, sl]
        kinv = (kk * jnp.exp(-b)).astype(BF16)
        scores = jnp.where(tri, _dot_nt(qt, kinv), 0.0).astype(BF16)
        o_intra = _dot(scores, v_h.astype(BF16))
        incr = _dot(v_h.T.astype(BF16), _block_place(kend, n_chunks))
        state = st_ref[hd]
        starts = [None] * n_chunks
        for c in order:
            starts[c] = state
            state = state * decay[c:c + 1, :] + incr[:, c * HG_DK:(c + 1) * HG_DK]
        st_ref[hd] = state
        o_inter = jnp.concatenate(
            [_dot_nt(qt[c * HG_CHUNK:(c + 1) * HG_CHUNK, :], starts[c].astype(BF16)) for c in range(n_chunks)],
            axis=0)
        oi_s[:, sl] = o_inter
        emit(o_intra + o_inter, sl)

    @pl.when(b_min <= HG_SAFE_LOG_DECAY)
    def _():
        for hd in range(HG_HEADS):
            sl = slice(hd * HG_DK, (hd + 1) * HG_DK)
            q, kk, b = _hgrn_gates(hq_ref, hf_ref, lb_ref, sl, tri_b3)
            q_s[...] = q
            k_s[...] = kk
            b_s[...] = b
            _hgrn_exact_scores(a_ref, q_s, k_s, b_s, reverse)
            emit(_dot(a_ref[...].astype(BF16), hi_ref[:, sl].astype(BF16)) + oi_s[:, sl], sl)

    @pl.when(pos == count - 1)
    def _():
        sf_ref[...] = st_ref[...]


def _hgrn_direction(lay, l, z, lb, s0_t, reverse, other=None, norm_g=None, cast=()):
    finish = other is not None
    m = z.shape[0]
    n_tiles = m // SEQ_TILE
    cast_specs, cast_out_specs, cast_shapes, cast_steps = _side_cast_specs(cast, l, n_tiles, lambda n: n)

    def tile_of(n):
        seq_idx, pos, count = lay.seq_pos(n, SEQ_TILE)
        return (n - pos + (count - 1 - pos)) if reverse else n

    def zcol(col):
        return pl.BlockSpec((SEQ_TILE, COL_W), lambda n: (tile_of(n), col))

    state_spec = pl.BlockSpec((None, HG_HEADS, HG_DV, HG_DK), lambda n: (lay.seq_pos(n, SEQ_TILE)[0], 0, 0, 0))
    o_spec = pl.BlockSpec((SEQ_TILE, HG_W), lambda n: (tile_of(n), 0))
    finish_specs, finish_args = [], []
    if finish:
        finish_specs = [o_spec, zcol(COL_HG), pl.BlockSpec((None, 1, HG_DV), lambda n: (l, 0, 0))]
        finish_args = [other, z, norm_g.reshape(-1, 1, HG_DV)]
    out = pl.pallas_call(
        functools.partial(_hgrn_kernel, lay=lay, reverse=reverse, finish=finish, cast_steps=cast_steps),
        grid=(n_tiles,),
        in_specs=[zcol(COL_HQ), zcol(COL_HF_BWD if reverse else COL_HF_FWD), zcol(COL_HI),
                  pl.BlockSpec((1, HG_W), lambda n: (0, 0)), state_spec] + finish_specs + cast_specs,
        out_specs=[o_spec, state_spec] + cast_out_specs,
        out_shape=[jax.ShapeDtypeStruct((m, HG_W), BF16 if finish else F32),
                   jax.ShapeDtypeStruct((lay.n_seq, HG_HEADS, HG_DV, HG_DK), F32)] + cast_shapes,
        scratch_shapes=[pltpu.VMEM((HG_HEADS, HG_DV, HG_DK), F32),
                        pltpu.VMEM((SEQ_TILE, HG_W), F32),
                        pltpu.VMEM((SEQ_TILE, SEQ_TILE), F32),
                        pltpu.VMEM((SEQ_TILE, HG_DK), F32),
                        pltpu.VMEM((SEQ_TILE, HG_DK), F32),
                        pltpu.VMEM((SEQ_TILE, HG_DK), F32)],
        compiler_params=_cparams("arbitrary"),
        name="hgrn_bwd" if reverse else "hgrn_fwd",
    )(z, z, z, lb, s0_t, *finish_args, *cast)
    return out[0], out[1], out[2:]


def _merge_kernel(h_ref, ap_ref, as_ref, b_ref, c_ref, d_ref, wg0, wg1, wg2, wg3, bg0, bg1, bg2, bg3,
                  wa_ref, wb_ref, wc_ref, wd_ref, *refs, p_tiles, cast_steps):
    n_cast = len(cast_steps)
    src_refs, o_ref, dst_refs = refs[:n_cast], refs[n_cast], refs[n_cast + 1:]
    _side_cast(src_refs, dst_refs, cast_steps, pl.program_id(0) * pl.num_programs(1) + pl.program_id(1))

    h = h_ref[...]
    attn = jnp.where(pl.program_id(1) < p_tiles, ap_ref[...], as_ref[...])
    acc = None
    for branch, w_ref, wg_ref, bg_ref in ((attn, wa_ref, wg0, bg0), (b_ref[...], wb_ref, wg1, bg1),
                                          (c_ref[...], wc_ref, wg2, bg2), (d_ref[...], wd_ref, wg3, bg3)):
        gate = _sigmoid(_dot(h, wg_ref[...]) + bg_ref[...])
        term = gate * _dot(branch, w_ref[...])
        acc = term if acc is None else acc + term
    o_ref[...] = acc.astype(BF16)


def _side_cast_specs(weights, layer, n_steps, step_of):
    in_specs, out_specs, shapes, steps_list = [], [], [], []
    for w in weights:
        _, rows, cols = w.shape
        slab, steps = _cast_plan(rows, n_steps)

        def slab_of(*g, steps=steps):
            return jnp.minimum(step_of(*g), steps - 1)

        in_specs.append(pl.BlockSpec((None, slab, cols), lambda *g, f=slab_of: (layer, f(*g), 0)))
        out_specs.append(pl.BlockSpec((None, slab, cols), lambda *g, f=slab_of: (0, f(*g), 0)))
        shapes.append(jax.ShapeDtypeStruct((1, rows, cols), BF16))
        steps_list.append(steps)
    return in_specs, out_specs, shapes, tuple(steps_list)


def _side_cast(src_refs, dst_refs, cast_steps, step):
    for src_ref, dst_ref, n_steps in zip(src_refs, dst_refs, cast_steps):
        @pl.when(step < n_steps)
        def _():
            dst_ref[...] = src_ref[...].astype(BF16)


def _cast_kernel(src_ref, dst_ref):
    dst_ref[...] = src_ref[...].astype(BF16)


def _cast_layer(w, l):
    _, rows, cols = w.shape
    slab, steps = _cast_plan(rows, max(1, rows * cols * 4 // CAST_SLAB_BYTES))
    return pl.pallas_call(
        _cast_kernel,
        grid=(steps,),
        in_specs=[pl.BlockSpec((None, slab, cols), lambda s: (l, s, 0))],
        out_specs=pl.BlockSpec((None, slab, cols), lambda s: (0, s, 0)),
        out_shape=jax.ShapeDtypeStruct((1, rows, cols), BF16),
        compiler_params=_cparams("parallel"),
        name="cast_weights",
    )(w)


CAST_SLAB_BYTES = 2 << 20


def _cast_plan(rows, n_steps):
    for steps in range(n_steps, 0, -1):
        slab = rows // steps
        if rows % steps == 0 and slab % (2 * SUBLANES) == 0:
            return slab, steps
    raise ValueError("no slab size for %d weight rows in %d steps" % (rows, n_steps))


def _merge(lay, l, h, attn_p, attn_s, branches, w_gate, b_gate, w_outs, cast_next=()):
    w_gate, lg = _layer_of(w_gate, l)
    m, d = h.shape
    tm = lay.row_tile(512)
    tn = 512
    nj = d // tn
    n_i = m // tm
    depth = b_gate.shape[0]
    p_tiles = lay.p_rows // tm
    s_tiles = lay.s_rows // tm

    def gate_w(n):
        return pl.BlockSpec((None, d, tn), lambda j, i: (lg, 0, n * nj + j))

    def gate_b(n):
        return pl.BlockSpec((None, 1, tn), lambda j, i: (l, 0, n * nj + j))

    cast_specs, cast_out_specs, cast_shapes, cast_steps = _side_cast_specs(
        cast_next, l + 1, nj * n_i, lambda j, i: j * n_i + i)

    b_gate3 = b_gate.reshape(depth, 1, N_BRANCH * d)
    out = pl.pallas_call(
        functools.partial(_merge_kernel, p_tiles=p_tiles, cast_steps=tuple(cast_steps)),
        grid=(nj, n_i),
        in_specs=[pl.BlockSpec((tm, d), lambda j, i: (i, 0)),
                  pl.BlockSpec((tm, A_WIDTH), lambda j, i: (jnp.minimum(i, p_tiles - 1), 0)),
                  pl.BlockSpec((tm, A_WIDTH), lambda j, i: (jnp.clip(i - p_tiles, 0, s_tiles - 1), 0))]
        + [pl.BlockSpec((tm, br.shape[1]), lambda j, i: (i, 0)) for br in branches]
        + [gate_w(n) for n in range(N_BRANCH)] + [gate_b(n) for n in range(N_BRANCH)]
        + [pl.BlockSpec((None, w.shape[1], tn), lambda j, i, lw=lw: (lw, 0, j)) for w, lw in w_outs]
        + cast_specs,
        out_specs=[pl.BlockSpec((tm, tn), lambda j, i: (i, j))] + cast_out_specs,
        out_shape=[jax.ShapeDtypeStruct((m, d), BF16)] + cast_shapes,
        compiler_params=_cparams("arbitrary", "arbitrary"),
        name="merge",
    )(h, attn_p, attn_s, *branches, *([w_gate] * N_BRANCH), *([b_gate3] * N_BRANCH),
      *(w for w, _ in w_outs), *cast_next)
    return out[0], out[1:]


def _rope_tables(lay):
    t = lay.dec_seq
    half = ROT_AXIS // 2
    rows = t // GRID_W
    row = jnp.repeat(jnp.arange(rows, dtype=F32), GRID_W)
    col = jnp.tile(jnp.arange(GRID_W, dtype=F32), rows)
    inv = ROPE_BASE ** (-jnp.arange(0, ROT_AXIS, 2, dtype=F32) / ROT_AXIS)
    ang = jnp.concatenate([row[:, None] * inv, row[:, None] * inv, col[:, None] * inv, col[:, None] * inv], axis=1)
    lane = jnp.arange(HEAD_DIM)
    first_half = (lane % ROT_AXIS) < half
    cos = jnp.cos(ang)
    sin = jnp.sin(ang)
    sin_a = jnp.where(first_half, -sin, 0.0)
    sin_b = jnp.where(first_half, 0.0, sin)

    def full(tab, fill):
        return jnp.concatenate([jnp.full((lay.p_rows, HEAD_DIM), fill, F32)] + [tab] * lay.dec_batch, axis=0)

    return full(cos, 1.0), full(sin_a, 0.0), full(sin_b, 0.0)


def kernel(x_prompt, x_sample, cache_k, cache_v, state_hgrn_fwd, state_hgrn_bwd, c, c_ctx, ada_w, ada_b, norm1_g, norm2_g, w_in, q_norm_g, k_norm_g, attn_sink, w_attn_out, conf_dw_w, conf_dw_b, conf_ln_g, conf_ln_b, w_conf_out, sc_conv_w, w_sc_out, hg_lb, hg_norm_g, w_hg_out, w_gate, b_gate, w_o, ffn_w1, ffn_w3, ffn_w2):
    batch, seq, d = x_prompt.shape
    dec_batch, dec_seq, _ = x_sample.shape
    depth = ada_w.shape[0]
    past = cache_k.shape[2]
    lay = _Layout(batch, seq, dec_batch, dec_seq)
    assert seq % SEQ_TILE == 0 and dec_seq % SEQ_TILE == 0 and dec_seq % GRID_W == 0
    assert 1 + dec_batch <= MOD_ROWS

    weights = dict(w_gate=w_gate, w_in=w_in, ffn_w1=ffn_w1, ffn_w3=ffn_w3, ffn_w2=ffn_w2, w_o=w_o,
                   w_attn_out=w_attn_out, w_conf_out=w_conf_out, w_sc_out=w_sc_out, w_hg_out=w_hg_out)
    by_fwd = ("w_gate", "w_attn_out", "w_conf_out", "w_sc_out", "w_hg_out")
    by_bwd = ("ffn_w1", "ffn_w3", "ffn_w2", "w_o")
    wl = {"w_in": (_cast_layer(w_in, 0), 0)}

    cvec = jnp.concatenate([c_ctx[None], c, jnp.zeros((MOD_ROWS - 1 - dec_batch, d), F32)], axis=0)
    mods = _modulation(cvec, ada_w, ada_b)
    mods = mods.reshape(depth, MOD_ROWS, 6, d)[:, :1 + dec_batch]
    mods = jnp.pad(mods, ((0, 0), (0, 0), (0, MOD_ROWS - 6), (0, 0)))

    lb = jax.nn.softmax(hg_lb.astype(F32), axis=1)
    lb = jnp.maximum(jnp.cumsum(lb, axis=1) - lb[:, :1], 0.0)

    rope_tabs = _rope_tables(lay)
    ctx_k = cache_k.reshape(dec_batch, depth, past, KV_WIDTH)
    ctx_v = cache_v.reshape(dec_batch, depth, past, KV_WIDTH)
    zero_states = jnp.zeros((batch, HG_HEADS, HG_DV, HG_DK), F32)

    x = (x_prompt.reshape(lay.p_rows, d), x_sample.reshape(lay.s_rows, d))
    ks_out, vs_out, sf_out, sb_out = [], [], [], []
    for l in range(depth):
        z, h = _in_projection(lay, l, x, mods, norm1_g, wl["w_in"])
        qn, kn, kf, vf = _qk_prep(lay, l, z, rope_tabs, q_norm_g, k_norm_g)
        attn_p = _context_attention(lay, l, qn, kn, z, attn_sink[l])
        attn_s = _latent_attention(lay, l, qn, kn, z, ctx_k, ctx_v, attn_sink[l])
        u_b, u_c = _conv_mixers(lay, l, z, conf_dw_w, conf_dw_b, conf_ln_g, conf_ln_b, sc_conv_w)
        s0_f = jnp.concatenate([zero_states, jnp.swapaxes(state_hgrn_fwd[:, l], -1, -2)], axis=0)
        s0_b = jnp.concatenate([zero_states, jnp.swapaxes(state_hgrn_bwd[:, l], -1, -2)], axis=0)
        first = l == 0
        o_f, s_f, cast_f = _hgrn_direction(lay, l, z, lb[0, l][None], s0_f, reverse=False,
                                           cast=tuple(weights[n] for n in by_fwd) if first else ())
        o_d, s_b, cast_b = _hgrn_direction(lay, l, z, lb[1, l][None], s0_b, reverse=True, other=o_f,
                                           norm_g=hg_norm_g,
                                           cast=tuple(weights[n] for n in by_bwd) if first else ())
        if first:
            wl.update({n: (w, 0) for n, w in zip(by_fwd + by_bwd, cast_f + cast_b)})
        merged, cast = _merge(lay, l, h, attn_p, attn_s, (u_b, u_c, o_d), wl["w_gate"], b_gate,
                              (wl["w_attn_out"], wl["w_conf_out"], wl["w_sc_out"], wl["w_hg_out"]),
                              cast_next=tuple(weights.values()) if l + 1 < depth else ())
        x, h2 = _out_projection(lay, l, merged, wl["w_o"], x, mods, norm2_g)
        t = _ffn_up(lay, l, h2, wl["ffn_w1"], wl["ffn_w3"])
        if l + 1 < depth:
            x = _residual_matmul(lay, l, t, wl["ffn_w2"], x, mods, MOD_GATE2, "ffn_down", tm=1024, tn=512)
            wl = {name: (w, 0) for name, w in zip(weights, cast)}
        else:
            y_p, y_s = (_residual_matmul(lay, l, t, wl["ffn_w2"], x, mods, MOD_GATE2, "ffn_down", tm=1024, tn=512,
                                         rows=r) for r in ((0, lay.p_rows), (lay.p_rows, lay.rows)))
        ks_out.append(kf.reshape(batch, seq, N_KV_A, HEAD_DIM))
        vs_out.append(vf.reshape(batch, seq, N_KV_A, HEAD_DIM))
        sf_out.append(jnp.swapaxes(s_f[:batch], -1, -2))
        sb_out.append(jnp.swapaxes(s_b[:batch], -1, -2))
    return (y_p.reshape(batch, seq, d), y_s.reshape(dec_batch, dec_seq, d),
            jnp.stack(ks_out, axis=1), jnp.stack(vs_out, axis=1),
            jnp.stack(sf_out, axis=1), jnp.stack(sb_out, axis=1))
```
